```python
import jax, jax.numpy as jnp
from jax import lax
import numpy as np

D_MODEL = 1024
BATCH = 2
SEQ = 8192
DEPTH = 2
DEC_BATCH = 128
DEC_SEQ = 4
PAST_LEN = 8192
PAGE_SIZE = 128

HEAD_DIM = 64
N_HEADS = 12
N_KV_HEADS = 4
MEM_HEADS = 4
N_MEM = 256
Q_W = N_HEADS * HEAD_DIM
KV_W = N_KV_HEADS * HEAD_DIM
MEM_W = MEM_HEADS * HEAD_DIM
MIX_W = Q_W + MEM_W
WINDOW = 128
SWA_BLOCK = 128
MOBA_BLOCK = 256
MOBA_TOPK = 3
MOBA_Q_CHUNK = 32
D_FF = 11 * D_MODEL // 4
CONV_W = 3
ROPE_THETA = 10000.0
LN_EPS = 1e-5
N_A_LAYERS = DEPTH // 2
N_B_LAYERS = DEPTH - N_A_LAYERS
ALPHA = (2 * DEPTH) ** 0.25
BETA = (8 * DEPTH) ** -0.25
SCALE = HEAD_DIM ** -0.5

kernel_name = 'yoco_swa_sink_moba_convffn_step'


def layer_norm(x, g, b):
    xf = x.astype(jnp.float32)
    mu = jnp.mean(xf, axis=-1, keepdims=True)
    var = jnp.mean(jnp.square(xf - mu), axis=-1, keepdims=True)
    y = (xf - mu) * lax.rsqrt(var + LN_EPS) * g.astype(jnp.float32) + b.astype(jnp.float32)
    return y.astype(x.dtype)


def rope(x, pos):
    half = x.shape[-1] // 2
    inv = jnp.power(jnp.float32(ROPE_THETA), -jnp.arange(half, dtype=jnp.float32) / half)
    ang = pos.astype(jnp.float32)[:, None] * inv[None, :]
    cos = jnp.cos(ang)[:, None, :]
    sin = jnp.sin(ang)[:, None, :]
    x1 = x[..., :half].astype(jnp.float32)
    x2 = x[..., half:].astype(jnp.float32)
    return jnp.concatenate([x1 * cos - x2 * sin, x2 * cos + x1 * sin], axis=-1).astype(x.dtype)


def heads(a, n):
    return a.reshape(a.shape[:-1] + (n, HEAD_DIM))


def sink_softmax(s, mask, sink):
    s = jnp.where(mask, s, -jnp.inf)
    m = jnp.maximum(jnp.max(s, axis=-1, keepdims=True), sink)
    p = jnp.exp(s - m)
    return p / (jnp.sum(p, axis=-1, keepdims=True) + jnp.exp(sink - m))


def swa_prompt(q, k, v, sink):
    B, T, H, hd = q.shape
    KV = k.shape[2]
    G = H // KV
    nb = T // SWA_BLOCK
    qb = q.reshape(B, nb, SWA_BLOCK, KV, G, hd)
    kb = k.reshape(B, nb, SWA_BLOCK, KV, hd)
    vb = v.reshape(B, nb, SWA_BLOCK, KV, hd)
    pad = ((0, 0), (1, 0), (0, 0), (0, 0), (0, 0))
    kcat = jnp.concatenate([jnp.pad(kb, pad)[:, :-1], kb], axis=2)
    vcat = jnp.concatenate([jnp.pad(vb, pad)[:, :-1], vb], axis=2)
    s = jnp.einsum('bnqkgd,bnskd->bnkgqs', qb, kcat).astype(jnp.float32) * SCALE
    blk = jnp.arange(nb, dtype=jnp.int32)[:, None] * SWA_BLOCK
    qpos = blk + jnp.arange(SWA_BLOCK, dtype=jnp.int32)[None, :]
    kpos = blk - SWA_BLOCK + jnp.arange(2 * SWA_BLOCK, dtype=jnp.int32)[None, :]
    kp, qp = kpos[:, None, :], qpos[:, :, None]
    mask = (kp <= qp) & (kp > qp - WINDOW) & (kp >= 0)
    sk = sink.astype(jnp.float32).reshape(1, 1, KV, G, 1, 1)
    p = sink_softmax(s, mask[None, :, None, None], sk)
    o = jnp.einsum('bnkgqs,bnskd->bnqkgd', p.astype(v.dtype), vcat)
    return o.reshape(B, T, H, hd)


def swa_sample(q, k_new, v_new, buf_k, buf_v, sink, past):
    Bd, S, H, hd = q.shape
    KV = k_new.shape[2]
    G = H // KV
    WB = buf_k.shape[1]
    k_all = jnp.concatenate([buf_k.astype(k_new.dtype), k_new], axis=1)
    v_all = jnp.concatenate([buf_v.astype(v_new.dtype), v_new], axis=1)
    qpos = past + jnp.arange(S, dtype=jnp.int32)
    kpos = past - WB + jnp.arange(WB + S, dtype=jnp.int32)
    kp, qp = kpos[None, :], qpos[:, None]
    mask = (kp <= qp) & (kp > qp - WINDOW)
    s = jnp.einsum('bqkgd,bskd->bkgqs', q.reshape(Bd, S, KV, G, hd), k_all).astype(jnp.float32) * SCALE
    sk = sink.astype(jnp.float32).reshape(1, KV, G, 1, 1)
    p = sink_softmax(s, mask, sk)
    o = jnp.einsum('bkgqs,bskd->bqkgd', p.astype(v_all.dtype), v_all).reshape(Bd, S, H, hd)
    return o, k_all[:, -WB:], v_all[:, -WB:]


def moba_blocks(k, v):
    B, L, KV, hd = k.shape
    Lp = -(-L // MOBA_BLOCK) * MOBA_BLOCK
    pad = ((0, 0), (0, Lp - L), (0, 0), (0, 0))
    nb = Lp // MOBA_BLOCK
    kbt = jnp.pad(k, pad).reshape(B, nb, MOBA_BLOCK, KV, hd).transpose(0, 3, 1, 2, 4)
    vbt = jnp.pad(v, pad).reshape(B, nb, MOBA_BLOCK, KV, hd).transpose(0, 3, 1, 2, 4)
    kmean = jnp.mean(kbt.astype(jnp.float32), axis=3)
    return kbt, vbt, kmean


def moba_core(q, q_pos, kbt, vbt, kmean):
    B, Q, H, hd = q.shape
    KV, NB = kbt.shape[1], kbt.shape[2]
    G = H // KV
    qg = q.reshape(B, Q, KV, G, hd).transpose(0, 2, 3, 1, 4)
    gate = jnp.einsum('bkgqd,bknd->bkgqn', qg.astype(jnp.float32), kmean)
    own = q_pos // MOBA_BLOCK
    eligible = jnp.arange(NB, dtype=jnp.int32)[None, :] < own[:, None]
    gate = jnp.where(eligible, gate, -jnp.inf)
    if NB < MOBA_TOPK:
        gate = jnp.pad(gate, ((0, 0),) * 4 + ((0, MOBA_TOPK - NB),), constant_values=-jnp.inf)
    top_val, top_idx = lax.top_k(gate, MOBA_TOPK)
    sel_valid = jnp.isfinite(top_val)
    top_idx = jnp.where(sel_valid, top_idx, 0).astype(jnp.int32)
    own_b = jnp.broadcast_to(own[:, None], top_idx.shape[:-1] + (1,)).astype(jnp.int32)
    ids = jnp.concatenate([top_idx, own_b], axis=-1)
    valid = jnp.concatenate([sel_valid, jnp.ones_like(sel_valid[..., :1])], axis=-1)
    bi = jnp.arange(B)[:, None, None, None, None]
    ki = jnp.arange(KV)[None, :, None, None, None]
    kg = kbt[bi, ki, ids]
    vg = vbt[bi, ki, ids]
    kpos = ids[..., None] * MOBA_BLOCK + jnp.arange(MOBA_BLOCK, dtype=jnp.int32)
    mask = valid[..., None] & (kpos <= q_pos[:, None, None])
    s = jnp.einsum('bkgqd,bkgqjsd->bkgqjs', qg, kg).astype(jnp.float32) * SCALE
    s = jnp.where(mask, s, -jnp.inf)
    shp = s.shape
    p = jax.nn.softmax(s.reshape(shp[:-2] + (-1,)), axis=-1).reshape(shp)
    o = jnp.einsum('bkgqjs,bkgqjsd->bkgqd', p.astype(vg.dtype), vg)
    return o.transpose(0, 3, 1, 2, 4).reshape(B, Q, H, hd)


def moba_prompt(q, k, v):
    B, T, H, hd = q.shape
    kbt, vbt, kmean = moba_blocks(k, v)
    nc = T // MOBA_Q_CHUNK
    qc = q.reshape(B, nc, MOBA_Q_CHUNK, H, hd).swapaxes(0, 1)
    pc = jnp.arange(T, dtype=jnp.int32).reshape(nc, MOBA_Q_CHUNK)
    out = lax.map(lambda a: moba_core(a[0], a[1], kbt, vbt, kmean), (qc, pc))
    return out.swapaxes(0, 1).reshape(B, T, H, hd)


def moba_sample(q, k_new, v_new, k_pool, v_pool, page_table):
    Bd, S, H, hd = q.shape
    KV = k_new.shape[2]
    past = page_table.shape[1] * k_pool.shape[1]
    q_pos = past + jnp.arange(S, dtype=jnp.int32)

    def one(a):
        q1, kn, vn, rows = a
        kp = k_pool[rows].reshape(past, KV, hd).astype(kn.dtype)
        vp = v_pool[rows].reshape(past, KV, hd).astype(vn.dtype)
        k_all = jnp.concatenate([kp, kn], axis=0)[None]
        v_all = jnp.concatenate([vp, vn], axis=0)[None]
        kbt, vbt, kmean = moba_blocks(k_all, v_all)
        return moba_core(q1[None], q_pos, kbt, vbt, kmean)[0]

    return lax.map(one, (q, k_new, v_new, page_table))


def mem_attend(qm, km, vm):
    s = jnp.einsum('bthd,bmhd->bhtm', qm, km.astype(qm.dtype)).astype(jnp.float32) * SCALE
    p = jax.nn.softmax(s, axis=-1)
    return jnp.einsum('bhtm,bmhd->bthd', p.astype(qm.dtype), vm.astype(qm.dtype))


def attn_merge(x, o_self, o_mem, w_o, g, b):
    B, T = x.shape[:2]
    mix = jnp.concatenate([o_self.reshape(B, T, Q_W), o_mem.reshape(B, T, MEM_W)], axis=-1)
    return layer_norm(ALPHA * x + mix @ w_o, g, b)


def conv_ffn(x, prev, w_gate, cw, cb, w_up, w_down, g, b):
    T = x.shape[1]
    u = x @ w_gate
    full = jnp.concatenate([prev.astype(u.dtype), u], axis=1)
    c = cb + sum(full[:, j:j + T] * cw[j] for j in range(CONV_W))
    h = jax.nn.gelu(c, approximate=False) * (x @ w_up)
    return layer_norm(ALPHA * x + h @ w_down, g, b), full[:, -(CONV_W - 1):]


def shared_kv(x, w_kv, pos):
    kv = x @ w_kv
    k = rope(heads(kv[..., :KV_W], N_KV_HEADS), pos)
    v = heads(kv[..., KV_W:], N_KV_HEADS)
    return k, v


def setup_inputs(seed: int = 0) -> dict:
    key = jax.random.key(seed)
    ks = jax.random.split(key, 28)
    f32 = jnp.float32
    n_pages = PAST_LEN // PAGE_SIZE
    n_used = DEC_BATCH * n_pages
    n_pool = n_used + n_used // 4
    wb = min(WINDOW, PAST_LEN)

    def nrm(k, shape, s=1.0):
        return jax.random.normal(k, shape, f32) * s

    in_a_scale = jnp.concatenate([jnp.ones((Q_W + KV_W,), f32), jnp.full((KV_W,), BETA, f32), jnp.ones((MEM_W,), f32)]) * D_MODEL ** -0.5
    kv_scale = jnp.concatenate([jnp.ones((KV_W,), f32), jnp.full((KV_W,), BETA, f32)]) * D_MODEL ** -0.5
    mem_scale = jnp.concatenate([jnp.ones((MEM_W,), f32), jnp.full((MEM_W,), BETA, f32)]) * D_MODEL ** -0.5
    page_table = jax.random.permutation(ks[5], n_pool)[:n_used].reshape(DEC_BATCH, n_pages).astype(jnp.int32)
    return {
        'x_prompt': nrm(ks[0], (BATCH, SEQ, D_MODEL)),
        'x_sample': nrm(ks[1], (DEC_BATCH, DEC_SEQ, D_MODEL)),
        'cache_win_k': nrm(ks[2], (N_A_LAYERS, DEC_BATCH, wb, N_KV_HEADS, HEAD_DIM)),
        'cache_win_v': nrm(ks[3], (N_A_LAYERS, DEC_BATCH, wb, N_KV_HEADS, HEAD_DIM), BETA),
        'cache_moba_k': nrm(ks[4], (n_pool, PAGE_SIZE, N_KV_HEADS, HEAD_DIM)),
        'cache_moba_v': nrm(ks[6], (n_pool, PAGE_SIZE, N_KV_HEADS, HEAD_DIM), BETA),
        'page_table': page_table,
        'cache_mem_k': nrm(ks[7], (DEPTH, DEC_BATCH, N_MEM, MEM_HEADS, HEAD_DIM)),
        'cache_mem_v': nrm(ks[8], (DEPTH, DEC_BATCH, N_MEM, MEM_HEADS, HEAD_DIM), BETA),
        'state_conv': nrm(ks[9], (DEPTH, DEC_BATCH, CONV_W - 1, D_FF)),
        'mem_prompt': nrm(ks[10], (BATCH, N_MEM, D_MODEL)),
        'w_in_a': nrm(ks[11], (N_A_LAYERS, D_MODEL, Q_W + 2 * KV_W + MEM_W)) * in_a_scale,
        'sink_a': nrm(ks[12], (N_A_LAYERS, N_HEADS), 0.5),
        'w_in_b': nrm(ks[13], (N_B_LAYERS, D_MODEL, Q_W + MEM_W), D_MODEL ** -0.5),
        'w_kv_shared': nrm(ks[14], (D_MODEL, 2 * KV_W)) * kv_scale,
        'w_mem_kv': nrm(ks[15], (DEPTH, D_MODEL, 2 * MEM_W)) * mem_scale,
        'w_o': nrm(ks[16], (DEPTH, MIX_W, D_MODEL), BETA * MIX_W ** -0.5),
        'ln1_g': 1.0 + nrm(ks[17], (DEPTH, D_MODEL), 0.02),
        'ln1_b': nrm(ks[18], (DEPTH, D_MODEL), 0.02),
        'w_ffn_gate': nrm(ks[19], (DEPTH, D_MODEL, D_FF), D_MODEL ** -0.5),
        'conv_w': nrm(ks[20], (DEPTH, CONV_W, D_FF), CONV_W ** -0.5),
        'conv_b': nrm(ks[21], (DEPTH, D_FF), 0.01),
        'w_ffn_up': nrm(ks[22], (DEPTH, D_MODEL, D_FF), BETA * D_MODEL ** -0.5),
        'w_ffn_down': nrm(ks[23], (DEPTH, D_FF, D_MODEL), BETA * D_FF ** -0.5),
        'ln2_g': 1.0 + nrm(ks[24], (DEPTH, D_MODEL), 0.02),
        'ln2_b': nrm(ks[25], (DEPTH, D_MODEL), 0.02),
    }


def reference(x_prompt, x_sample, cache_win_k, cache_win_v, cache_moba_k, cache_moba_v, page_table, cache_mem_k, cache_mem_v, state_conv, mem_prompt, w_in_a, sink_a, w_in_b, w_kv_shared, w_mem_kv, w_o, ln1_g, ln1_b, w_ffn_gate, conv_w, conv_b, w_ffn_up, w_ffn_down, ln2_g, ln2_b):
    B, T, _ = x_prompt.shape
    Bd, S, _ = x_sample.shape
    past = page_table.shape[1] * cache_moba_k.shape[1]
    pos_p = jnp.arange(T, dtype=jnp.int32)
    pos_s = past + jnp.arange(S, dtype=jnp.int32)
    wb_p = min(WINDOW, T)
    xp, xs = x_prompt, x_sample
    win_kp, win_vp, win_ks, win_vs = [], [], [], []
    mem_kp, mem_vp, conv_p, conv_s = [], [], [], []
    sh_kp = sh_vp = sh_ks = sh_vs = None
    for layer in range(DEPTH):
        if layer < N_A_LAYERS:
            pp = xp @ w_in_a[layer]
            ps = xs @ w_in_a[layer]
            qp = rope(heads(pp[..., :Q_W], N_HEADS), pos_p)
            kp = rope(heads(pp[..., Q_W:Q_W + KV_W], N_KV_HEADS), pos_p)
            vp = heads(pp[..., Q_W + KV_W:Q_W + 2 * KV_W], N_KV_HEADS)
            qmp = heads(pp[..., Q_W + 2 * KV_W:], MEM_HEADS)
            qs = rope(heads(ps[..., :Q_W], N_HEADS), pos_s)
            ks_ = rope(heads(ps[..., Q_W:Q_W + KV_W], N_KV_HEADS), pos_s)
            vs_ = heads(ps[..., Q_W + KV_W:Q_W + 2 * KV_W], N_KV_HEADS)
            qms = heads(ps[..., Q_W + 2 * KV_W:], MEM_HEADS)
            op = swa_prompt(qp, kp, vp, sink_a[layer])
            win_kp.append(kp[:, T - wb_p:])
            win_vp.append(vp[:, T - wb_p:])
            os_, nbk, nbv = swa_sample(qs, ks_, vs_, cache_win_k[layer], cache_win_v[layer], sink_a[layer], past)
            win_ks.append(nbk)
            win_vs.append(nbv)
        else:
            if layer == N_A_LAYERS:
                sh_kp, sh_vp = shared_kv(xp, w_kv_shared, pos_p)
                sh_ks, sh_vs = shared_kv(xs, w_kv_shared, pos_s)
            j = layer - N_A_LAYERS
            pp = xp @ w_in_b[j]
            ps = xs @ w_in_b[j]
            qp = rope(heads(pp[..., :Q_W], N_HEADS), pos_p)
            qmp = heads(pp[..., Q_W:], MEM_HEADS)
            qs = rope(heads(ps[..., :Q_W], N_HEADS), pos_s)
            qms = heads(ps[..., Q_W:], MEM_HEADS)
            op = moba_prompt(qp, sh_kp, sh_vp)
            os_ = moba_sample(qs, sh_ks, sh_vs, cache_moba_k, cache_moba_v, page_table)
        kvm = mem_prompt @ w_mem_kv[layer]
        kmp = heads(kvm[..., :MEM_W], MEM_HEADS)
        vmp = heads(kvm[..., MEM_W:], MEM_HEADS)
        mem_kp.append(kmp)
        mem_vp.append(vmp)
        omp = mem_attend(qmp, kmp, vmp)
        oms = mem_attend(qms, cache_mem_k[layer], cache_mem_v[layer])
        xp = attn_merge(xp, op, omp, w_o[layer], ln1_g[layer], ln1_b[layer])
        xs = attn_merge(xs, os_, oms, w_o[layer], ln1_g[layer], ln1_b[layer])
        xp, cp = conv_ffn(xp, jnp.zeros((B, CONV_W - 1, D_FF), xp.dtype), w_ffn_gate[layer], conv_w[layer], conv_b[layer], w_ffn_up[layer], w_ffn_down[layer], ln2_g[layer], ln2_b[layer])
        xs, cs = conv_ffn(xs, state_conv[layer], w_ffn_gate[layer], conv_w[layer], conv_b[layer], w_ffn_up[layer], w_ffn_down[layer], ln2_g[layer], ln2_b[layer])
        conv_p.append(cp)
        conv_s.append(cs)
    return (xp, xs, jnp.stack(win_kp), jnp.stack(win_vp), sh_kp, sh_vp, jnp.stack(mem_kp), jnp.stack(mem_vp), jnp.stack(conv_p), jnp.stack(win_ks), jnp.stack(win_vs), sh_ks, sh_vs, jnp.stack(conv_s))
```

```python
import functools
import math

import jax
import jax.numpy as jnp
from jax import lax
from jax.experimental import pallas as pl
from jax.experimental.pallas import tpu as pltpu

D_MODEL = 1024
HEAD_DIM = 64
N_HEADS = 12
N_KV_HEADS = 4
GROUP = N_HEADS // N_KV_HEADS
MEM_HEADS = 4
N_MEM = 256
Q_W = N_HEADS * HEAD_DIM
KV_W = N_KV_HEADS * HEAD_DIM
MEM_W = MEM_HEADS * HEAD_DIM
PROJ_W = Q_W + 2 * KV_W + MEM_W
WINDOW = 128
MOBA_BLOCK = 256
MOBA_TOPK = 3
D_FF = 2816
CONV_W = 3
ROPE_THETA = 10000.0
LN_EPS = 1e-5
DEPTH = 2
ALPHA = (2 * DEPTH) ** 0.25
SCALE = HEAD_DIM ** -0.5

LANES = 128
TOKEN_TILE = 256
VMEM_LIMIT = 56 * 1024 * 1024
NEG_INF = float("-inf")

f32 = jnp.float32
bf16 = jnp.bfloat16


def _cparams(n_grid, vmem=VMEM_LIMIT):
    return pltpu.CompilerParams(dimension_semantics=("arbitrary",) * n_grid, vmem_limit_bytes=vmem)


def _dot_nt(a, b, precision=None):
    return lax.dot_general(a, b, (((1,), (1,)), ((), ())), preferred_element_type=f32, precision=precision)


def _dot(a, b):
    return jnp.dot(a, b, preferred_element_type=f32)


def _layer_norm(r, g, b):
    mu = jnp.mean(r, axis=-1, keepdims=True)
    d = r - mu
    var = jnp.mean(d * d, axis=-1, keepdims=True)
    return d * lax.rsqrt(var + LN_EPS) * g + b


def _half_mask(width):
    lane = lax.broadcasted_iota(jnp.int32, (1, width), 1)
    return (lane % HEAD_DIM) < (HEAD_DIM // 2)


def _proj_kernel(x_ref, w_ref, cos_ref, sin_ref, q_ref, k_ref, v_ref, qm_ref, *layout_refs, layouts):
    xb = x_ref[0].astype(bf16)
    y = _dot(xb, w_ref[...])
    qk = y[:, :Q_W + KV_W]
    width = Q_W + KV_W
    reps = width // LANES
    cos = jnp.tile(cos_ref[...], (1, reps))
    sin = jnp.tile(sin_ref[...], (1, reps))
    half = HEAD_DIM // 2
    partner = jnp.where(_half_mask(width), pltpu.roll(qk, width - half, 1), pltpu.roll(qk, half, 1))
    rot = qk * cos + partner * sin
    q = rot[:, :Q_W]
    k = rot[:, Q_W:]
    v = y[:, Q_W + KV_W:Q_W + 2 * KV_W]
    qm = y[:, Q_W + 2 * KV_W:]
    q_ref[0] = q
    k_ref[0] = k
    v_ref[0] = v
    qm_ref[0] = qm
    if not layouts:
        return
    qpad_ref, qmpad_ref, k5_ref, vt5_ref, kmean_ref = layout_refs
    tm = q.shape[0]
    lane = lax.broadcasted_iota(jnp.int32, (1, LANES), 1)
    lo = lane < HEAD_DIM

    def placed(tile, src_half, dst_half):
        t = tile if src_half == dst_half else pltpu.roll(tile, HEAD_DIM, 1)
        keep = lo if dst_half == 0 else jnp.logical_not(lo)
        return jnp.where(keep, t, 0.0)

    for h in range(N_KV_HEADS):
        for g in range(GROUP):
            hq = h * GROUP + g
            tile = q[:, (hq // 2) * LANES:(hq // 2 + 1) * LANES]
            qpad_ref[0, h, g * tm:(g + 1) * tm, :] = (placed(tile, hq % 2, h % 2) * SCALE).astype(bf16)
    for h in range(MEM_HEADS):
        tile = qm[:, (h // 2) * LANES:(h // 2 + 1) * LANES]
        qmpad_ref[0, h] = (placed(tile, h % 2, h % 2) * SCALE).astype(bf16)
    for p in range(KV_W // LANES):
        k5_ref[0, p, 0] = k[:, p * LANES:(p + 1) * LANES].astype(bf16)
    vt = v.T
    for h in range(N_KV_HEADS):
        vt5_ref[0, h, 0] = vt[h * HEAD_DIM:(h + 1) * HEAD_DIM, :].astype(bf16)
    kmean_ref[0, 0] = jnp.mean(k, axis=0, keepdims=True)


def _proj(x, w, cos, sin, layouts):
    B, T, _ = x.shape
    tm = TOKEN_TILE
    nt = T // tm
    n_tab = cos.shape[0] // tm
    out_shape = [
        jax.ShapeDtypeStruct((B, T, Q_W), f32),
        jax.ShapeDtypeStruct((B, T, KV_W), f32),
        jax.ShapeDtypeStruct((B, T, KV_W), f32),
        jax.ShapeDtypeStruct((B, T, MEM_W), f32),
    ]
    out_specs = [
        pl.BlockSpec((1, tm, Q_W), lambda b, i: (b, i, 0)),
        pl.BlockSpec((1, tm, KV_W), lambda b, i: (b, i, 0)),
        pl.BlockSpec((1, tm, KV_W), lambda b, i: (b, i, 0)),
        pl.BlockSpec((1, tm, MEM_W), lambda b, i: (b, i, 0)),
    ]
    if layouts:
        out_shape += [
            jax.ShapeDtypeStruct((B, N_KV_HEADS, GROUP * T, LANES), bf16),
            jax.ShapeDtypeStruct((B, MEM_HEADS, T, LANES), bf16),
            jax.ShapeDtypeStruct((B, KV_W // LANES, nt, tm, LANES), bf16),
            jax.ShapeDtypeStruct((B, N_KV_HEADS, nt, HEAD_DIM, tm), bf16),
            jax.ShapeDtypeStruct((B, nt, 1, KV_W), f32),
        ]
        out_specs += [
            pl.BlockSpec((1, N_KV_HEADS, GROUP * tm, LANES), lambda b, i: (b, 0, i, 0)),
            pl.BlockSpec((1, MEM_HEADS, tm, LANES), lambda b, i: (b, 0, i, 0)),
            pl.BlockSpec((1, KV_W // LANES, 1, tm, LANES), lambda b, i: (b, 0, i, 0, 0)),
            pl.BlockSpec((1, N_KV_HEADS, 1, HEAD_DIM, tm), lambda b, i: (b, 0, i, 0, 0)),
            pl.BlockSpec((1, 1, 1, KV_W), lambda b, i: (b, i, 0, 0)),
        ]
    return pl.pallas_call(
        functools.partial(_proj_kernel, layouts=layouts),
        grid=(B, nt),
        in_specs=[
            pl.BlockSpec((1, tm, D_MODEL), lambda b, i: (b, i, 0)),
            pl.BlockSpec((D_MODEL, PROJ_W), lambda b, i: (0, 0)),
            pl.BlockSpec((tm, LANES), lambda b, i: (i % n_tab, 0)),
            pl.BlockSpec((tm, LANES), lambda b, i: (i % n_tab, 0)),
        ],
        out_specs=out_specs,
        out_shape=out_shape,
        compiler_params=_cparams(2),
        name="proj_layouts" if layouts else "proj",
    )(x, w, cos, sin)


def _rope_tables(pos):
    half = HEAD_DIM // 2
    inv = jnp.power(jnp.float32(ROPE_THETA), -jnp.arange(half, dtype=f32) / half)
    ang = pos.astype(f32)[:, None] * inv[None, :]
    cos = jnp.cos(ang)
    sin = jnp.sin(ang)
    reps = LANES // HEAD_DIM
    return jnp.tile(jnp.concatenate([cos, cos], axis=1), (1, reps)), jnp.tile(jnp.concatenate([-sin, sin], axis=1), (1, reps))


def _memkv_kernel(m_ref, w_ref, kv_ref, k2_ref, vt_ref):
    y = _dot(m_ref[0].astype(bf16), w_ref[0])
    kv_ref[0, 0] = y
    for p in range(MEM_W // LANES):
        k2_ref[0, 0, p] = y[:, p * LANES:(p + 1) * LANES].astype(bf16)
    vt = y[:, MEM_W:].T
    for h in range(MEM_HEADS):
        vt_ref[0, 0, h] = vt[h * HEAD_DIM:(h + 1) * HEAD_DIM, :].astype(bf16)


def _memkv(mem, w):
    B = mem.shape[0]
    return pl.pallas_call(
        _memkv_kernel,
        grid=(DEPTH, B),
        in_specs=[
            pl.BlockSpec((1, N_MEM, D_MODEL), lambda l, b: (b, 0, 0)),
            pl.BlockSpec((1, D_MODEL, 2 * MEM_W), lambda l, b: (l, 0, 0)),
        ],
        out_specs=[
            pl.BlockSpec((1, 1, N_MEM, 2 * MEM_W), lambda l, b: (l, b, 0, 0)),
            pl.BlockSpec((1, 1, MEM_W // LANES, N_MEM, LANES), lambda l, b: (l, b, 0, 0, 0)),
            pl.BlockSpec((1, 1, MEM_HEADS, HEAD_DIM, N_MEM), lambda l, b: (l, b, 0, 0, 0)),
        ],
        out_shape=[
            jax.ShapeDtypeStruct((DEPTH, B, N_MEM, 2 * MEM_W), f32),
            jax.ShapeDtypeStruct((DEPTH, B, MEM_W // LANES, N_MEM, LANES), bf16),
            jax.ShapeDtypeStruct((DEPTH, B, MEM_HEADS, HEAD_DIM, N_MEM), bf16),
        ],
        compiler_params=_cparams(2),
        name="memkv",
    )(mem, w)


def _gelu(c):
    return 0.5 * c * (1.0 + lax.erf(c * (1.0 / math.sqrt(2.0))))


def _merge_ffn_kernel(x_ref, os_ref, om_ref, wos_ref, wom_ref, g1_ref, b1_ref, wg_ref, cw_ref, cb_ref, wu_ref,
                      wd_ref, g2_ref, b2_ref, *rest, sample):
    if sample:
        inj1_ref, inj2_ref, y_ref, u_ref = rest
    else:
        y_ref, tail_ref, carry_ref = rest
    x = x_ref[0]
    attn = _dot(os_ref[0], wos_ref[...]) + _dot(om_ref[0], wom_ref[...])
    x1 = _layer_norm(ALPHA * x + attn, g1_ref[...], b1_ref[...])
    x1b = x1.astype(bf16)
    u = _dot(x1b, wg_ref[...])
    up = _dot(x1b, wu_ref[...])
    tm = u.shape[0]
    row = lax.broadcasted_iota(jnp.int32, (tm, 1), 0)
    r1 = pltpu.roll(u, 1, 0)
    r2 = pltpu.roll(u, 2, 0)
    if sample:
        s = row % 4
        prev1 = jnp.where(s >= 1, r1, inj1_ref[0])
        prev2 = jnp.where(s >= 2, r2, inj2_ref[0])
        u_ref[0] = u
    else:
        @pl.when(pl.program_id(1) == 0)
        def _():
            carry_ref[...] = jnp.zeros_like(carry_ref)
        c6 = carry_ref[6:7, :]
        c7 = carry_ref[7:8, :]
        prev1 = jnp.where(row == 0, c7, r1)
        prev2 = jnp.where(row == 0, c6, jnp.where(row == 1, c7, r2))
        carry_ref[...] = u[tm - 8:, :]
        tail_ref[0] = u[tm - 8:, :]
    cw = cw_ref[...]
    c = cb_ref[...] + prev2 * cw[0:1, :] + prev1 * cw[1:2, :] + u * cw[2:3, :]
    hid = (_gelu(c) * up).astype(bf16)
    y_ref[0] = _layer_norm(ALPHA * x1 + _dot(hid, wd_ref[...]), g2_ref[...], b2_ref[...])


def _merge_ffn(x, o_self, o_mem, lw, inj=None):
    B, T, _ = x.shape
    tm = TOKEN_TILE
    nt = T // tm
    sample = inj is not None
    tok = lambda w: pl.BlockSpec((1, tm, w), lambda b, i: (b, i, 0))
    const = lambda shape: pl.BlockSpec(shape, lambda b, i: (0,) * len(shape), pipeline_mode=pl.Buffered(1))
    in_specs = [
        tok(D_MODEL), tok(Q_W), tok(MEM_W),
        const((Q_W, D_MODEL)), const((MEM_W, D_MODEL)), const((1, D_MODEL)), const((1, D_MODEL)),
        const((D_MODEL, D_FF)), const((CONV_W, D_FF)), const((1, D_FF)), const((D_MODEL, D_FF)),
        const((D_FF, D_MODEL)), const((1, D_MODEL)), const((1, D_MODEL)),
    ]
    args = [x, o_self, o_mem, lw["wo_self"], lw["wo_mem"], lw["g1"], lw["b1"], lw["w_gate"], lw["conv_w"],
            lw["conv_b"], lw["w_up"], lw["w_down"], lw["g2"], lw["b2"]]
    if sample:
        in_specs += [tok(D_FF), tok(D_FF)]
        args += list(inj)
        out_specs = [tok(D_MODEL), tok(D_FF)]
        out_shape = [jax.ShapeDtypeStruct((B, T, D_MODEL), f32), jax.ShapeDtypeStruct((B, T, D_FF), f32)]
        scratch = []
    else:
        out_specs = [tok(D_MODEL), pl.BlockSpec((1, 8, D_FF), lambda b, i: (b, 0, 0))]
        out_shape = [jax.ShapeDtypeStruct((B, T, D_MODEL), f32), jax.ShapeDtypeStruct((B, 8, D_FF), f32)]
        scratch = [pltpu.VMEM((8, D_FF), f32)]
    return pl.pallas_call(
        functools.partial(_merge_ffn_kernel, sample=sample),
        grid=(B, nt),
        in_specs=in_specs,
        out_specs=out_specs,
        out_shape=out_shape,
        scratch_shapes=scratch,
        compiler_params=_cparams(2),
        name="merge_ffn_sample" if sample else "merge_ffn",
    )(*args)


def _store_heads_transposed(ot_ref, out_ref):
    out_ref[0] = ot_ref[...].T.astype(out_ref.dtype)


def _swa_prompt_kernel(q_ref, kc_ref, kp_ref, vc_ref, vp_ref, sink_ref, out_ref, ot_ref):
    i = pl.program_id(1)
    tm = TOKEN_TILE
    ncol = GROUP * tm
    key_c = lax.broadcasted_iota(jnp.int32, (tm, ncol), 0)
    key_p = lax.broadcasted_iota(jnp.int32, (WINDOW, ncol), 0)
    qc = lax.broadcasted_iota(jnp.int32, (1, ncol), 1) % tm
    valid_c = (key_c <= qc) & (key_c > qc - WINDOW)
    valid_p = (key_p > qc) & (i > 0)
    for h in range(N_KV_HEADS):
        p = h // 2
        qh = q_ref[0, h]
        s_c = jnp.where(valid_c, _dot_nt(kc_ref[0, p, 0], qh), NEG_INF)
        s_p = jnp.where(valid_p, _dot_nt(kp_ref[0, p, 0], qh), NEG_INF)
        sink = sink_ref[h]
        m = jnp.maximum(jnp.maximum(jnp.max(s_c, axis=0, keepdims=True), jnp.max(s_p, axis=0, keepdims=True)), sink)
        e_c = jnp.exp(s_c - m)
        e_p = jnp.exp(s_p - m)
        den = jnp.sum(e_c, axis=0, keepdims=True) + jnp.sum(e_p, axis=0, keepdims=True) + jnp.exp(sink - m)
        ot = _dot(vc_ref[0, h, 0], e_c.astype(bf16)) + _dot(vp_ref[0, h, 0], e_p.astype(bf16))
        ot = ot / den
        for g in range(GROUP):
            hq = h * GROUP + g
            ot_ref[hq * HEAD_DIM:(hq + 1) * HEAD_DIM, :] = ot[:, g * tm:(g + 1) * tm]
    _store_heads_transposed(ot_ref, out_ref)


def _swa_prompt(qpad, k5, vt5, sink_cols):
    B = qpad.shape[0]
    nt = k5.shape[2]
    tm = TOKEN_TILE
    T = nt * tm
    prev = lambda i: jnp.maximum(i - 1, 0)
    return pl.pallas_call(
        _swa_prompt_kernel,
        grid=(B, nt),
        in_specs=[
            pl.BlockSpec((1, N_KV_HEADS, GROUP * tm, LANES), lambda b, i: (b, 0, i, 0)),
            pl.BlockSpec((1, KV_W // LANES, 1, tm, LANES), lambda b, i: (b, 0, i, 0, 0)),
            pl.BlockSpec((1, KV_W // LANES, 1, WINDOW, LANES), lambda b, i: (b, 0, prev(i), tm // WINDOW - 1, 0)),
            pl.BlockSpec((1, N_KV_HEADS, 1, HEAD_DIM, tm), lambda b, i: (b, 0, i, 0, 0)),
            pl.BlockSpec((1, N_KV_HEADS, 1, HEAD_DIM, WINDOW), lambda b, i: (b, 0, prev(i), 0, tm // WINDOW - 1)),
            pl.BlockSpec((N_KV_HEADS, 1, GROUP * tm), lambda b, i: (0, 0, 0)),
        ],
        out_specs=pl.BlockSpec((1, tm, Q_W), lambda b, i: (b, i, 0)),
        out_shape=jax.ShapeDtypeStruct((B, T, Q_W), bf16),
        scratch_shapes=[pltpu.VMEM((Q_W, tm), f32)],
        compiler_params=_cparams(2),
        name="swa_prompt",
    )(qpad, k5, k5, vt5, vt5, sink_cols)


def _mem_attn_prompt_kernel(q_ref, k_ref, vt_ref, out_ref, ot_ref):
    for h in range(MEM_HEADS):
        s = _dot_nt(k_ref[0, h // 2], q_ref[0, h])
        m = jnp.max(s, axis=0, keepdims=True)
        e = jnp.exp(s - m)
        den = jnp.sum(e, axis=0, keepdims=True)
        ot_ref[h * HEAD_DIM:(h + 1) * HEAD_DIM, :] = _dot(vt_ref[0, h], e.astype(bf16)) / den
    _store_heads_transposed(ot_ref, out_ref)


def _mem_attn_prompt(qmpad, mk2, mvt):
    B, _, T, _ = qmpad.shape
    tm = TOKEN_TILE
    return pl.pallas_call(
        _mem_attn_prompt_kernel,
        grid=(B, T // tm),
        in_specs=[
            pl.BlockSpec((1, MEM_HEADS, tm, LANES), lambda b, i: (b, 0, i, 0)),
            pl.BlockSpec((1, MEM_W // LANES, N_MEM, LANES), lambda b, i: (b, 0, 0, 0)),
            pl.BlockSpec((1, MEM_HEADS, HEAD_DIM, N_MEM), lambda b, i: (b, 0, 0, 0)),
        ],
        out_specs=pl.BlockSpec((1, tm, MEM_W), lambda b, i: (b, i, 0)),
        out_shape=jax.ShapeDtypeStruct((B, T, MEM_W), bf16),
        scratch_shapes=[pltpu.VMEM((MEM_W, tm), f32)],
        compiler_params=_cparams(2),
        name="mem_attn_prompt",
    )(qmpad, mk2, mvt)


def _top3_bias(gate, eligible, axis):
    idx = lax.broadcasted_iota(jnp.int32, gate.shape, axis)
    n = gate.shape[axis]
    rest = jnp.where(eligible, gate, NEG_INF)
    chosen = jnp.zeros(gate.shape, dtype=jnp.bool_)
    for _ in range(MOBA_TOPK):
        best = jnp.max(rest, axis=axis, keepdims=True)
        first = jnp.min(jnp.where(rest == best, idx, n), axis=axis, keepdims=True)
        pick = (idx == first) & (best > NEG_INF)
        chosen = chosen | pick
        rest = jnp.where(pick, NEG_INF, rest)
    return jnp.where(chosen, 0.0, NEG_INF)


def _moba_gate_kernel(q_ref, km_ref, bias_ref):
    i = pl.program_id(1)
    tm = TOKEN_TILE
    nb = km_ref.shape[1]
    q = q_ref[0]
    km = km_ref[0]
    lane = lax.broadcasted_iota(jnp.int32, (1, LANES), 1)
    lo = lane < HEAD_DIM
    blk = lax.broadcasted_iota(jnp.int32, (nb, 1), 0)
    for h in range(N_KV_HEADS):
        kt = km[:, (h // 2) * LANES:(h // 2 + 1) * LANES]
        gates = []
        for g in range(GROUP):
            hq = h * GROUP + g
            kk = kt if hq % 2 == h % 2 else pltpu.roll(kt, HEAD_DIM, 1)
            kk = jnp.where(lo if hq % 2 == 0 else jnp.logical_not(lo), kk, 0.0)
            qt = q[:, (hq // 2) * LANES:(hq // 2 + 1) * LANES]
            gates.append(_dot_nt(kk, qt, precision=lax.Precision.HIGHEST))
        gate = jnp.concatenate(gates, axis=1)
        bias_ref[0, 0, h] = _top3_bias(gate, blk < i, 0)


def _moba_gate(q, kmean):
    B, T, _ = q.shape
    tm = TOKEN_TILE
    nb = T // tm
    return pl.pallas_call(
        _moba_gate_kernel,
        grid=(B, nb),
        in_specs=[
            pl.BlockSpec((1, tm, Q_W), lambda b, i: (b, i, 0)),
            pl.BlockSpec((1, nb, KV_W), lambda b, i: (b, 0, 0)),
        ],
        out_specs=pl.BlockSpec((1, 1, N_KV_HEADS, nb, GROUP * tm), lambda b, i: (b, i, 0, 0, 0)),
        out_shape=jax.ShapeDtypeStruct((B, nb, N_KV_HEADS, nb, GROUP * tm), f32),
        compiler_params=_cparams(2),
        name="moba_gate",
    )(q, kmean)


def _moba_prompt_kernel(q_ref, k_ref, vt_ref, bias_ref, out_ref, ot_ref):
    i = pl.program_id(1)
    tm = TOKEN_TILE
    ncol = GROUP * tm
    key = lax.broadcasted_iota(jnp.int32, (tm, ncol), 0)
    qc = lax.broadcasted_iota(jnp.int32, (1, ncol), 1) % tm
    causal = key <= qc
    for h in range(N_KV_HEADS):
        p = h // 2
        qh = q_ref[0, h]
        s = jnp.where(causal, _dot_nt(k_ref[0, p, i], qh), NEG_INF)
        m0 = jnp.max(s, axis=0, keepdims=True)
        e = jnp.exp(s - m0)
        l0 = jnp.sum(e, axis=0, keepdims=True)
        acc0 = _dot(vt_ref[0, h, i], e.astype(bf16))

        def body(j, carry, h=h, p=p, qh=qh):
            m, l, acc = carry
            s = _dot_nt(k_ref[0, p, j], qh) + bias_ref[0, 0, h, pl.ds(j, 1), :]
            m_new = jnp.maximum(m, jnp.max(s, axis=0, keepdims=True))
            a = jnp.exp(m - m_new)
            e = jnp.exp(s - m_new)
            l = a * l + jnp.sum(e, axis=0, keepdims=True)
            acc = a * acc + _dot(vt_ref[0, h, j], e.astype(bf16))
            return m_new, l, acc

        _, l, acc = lax.fori_loop(0, i, body, (m0, l0, acc0))
        ot = acc / l
        for g in range(GROUP):
            hq = h * GROUP + g
            ot_ref[hq * HEAD_DIM:(hq + 1) * HEAD_DIM, :] = ot[:, g * tm:(g + 1) * tm]
    _store_heads_transposed(ot_ref, out_ref)


def _moba_prompt(qpad, k5, vt5, bias):
    B = qpad.shape[0]
    nb = k5.shape[2]
    tm = TOKEN_TILE
    T = nb * tm
    return pl.pallas_call(
        _moba_prompt_kernel,
        grid=(B, nb),
        in_specs=[
            pl.BlockSpec((1, N_KV_HEADS, GROUP * tm, LANES), lambda b, i: (b, 0, i, 0)),
            pl.BlockSpec((1, KV_W // LANES, nb, tm, LANES), lambda b, i: (b, 0, 0, 0, 0)),
            pl.BlockSpec((1, N_KV_HEADS, nb, HEAD_DIM, tm), lambda b, i: (b, 0, 0, 0, 0)),
            pl.BlockSpec((1, 1, N_KV_HEADS, nb, GROUP * tm), lambda b, i: (b, i, 0, 0, 0)),
        ],
        out_specs=pl.BlockSpec((1, tm, Q_W), lambda b, i: (b, i, 0)),
        out_shape=jax.ShapeDtypeStruct((B, T, Q_W), bf16),
        scratch_shapes=[pltpu.VMEM((Q_W, tm), f32)],
        compiler_params=_cparams(2),
        name="moba_prompt",
    )(qpad, k5, vt5, bias)


SAMPLE_ROWS = N_HEADS * 4
NEW_PAD = 8
SEQ_GROUP = 8


def _pad_rows(a, n):
    return jnp.concatenate([a, jnp.zeros((n - a.shape[0], a.shape[1]), a.dtype)], axis=0)


def _swa_sample_kernel(q_ref, kc_ref, vc_ref, kn_ref, vn_ref, sink_ref, o_ref, nk_ref, nv_ref):
    wb = kc_ref.shape[1]
    nkeys = 2 * wb
    col = lax.broadcasted_iota(jnp.int32, (SAMPLE_ROWS, nkeys), 1)
    s_idx = lax.broadcasted_iota(jnp.int32, (SAMPLE_ROWS, 1), 0) % 4
    valid = ((col < wb) & (col > s_idx)) | ((col >= wb) & (col - wb <= s_idx))
    row = lax.broadcasted_iota(jnp.int32, (wb, 1), 0)
    sink = sink_ref[...]
    for n in range(SEQ_GROUP):
        kc = kc_ref[n]
        vc = vc_ref[n]
        kn = _pad_rows(kn_ref[n], wb)
        vn = _pad_rows(vn_ref[n], wb)
        k_all = jnp.concatenate([kc, kn], axis=0).astype(bf16)
        v_all = jnp.concatenate([vc, vn], axis=0).astype(bf16)
        s = jnp.where(valid, _dot_nt(q_ref[n], k_all), NEG_INF)
        m = jnp.maximum(jnp.max(s, axis=1, keepdims=True), sink)
        e = jnp.exp(s - m)
        den = jnp.sum(e, axis=1, keepdims=True) + jnp.exp(sink - m)
        o_ref[n] = _dot(e.astype(bf16), v_all) / den
        nk_ref[n] = jnp.where(row < wb - 4, pltpu.roll(kc, wb - 4, 0), pltpu.roll(kn, wb - 4, 0))
        nv_ref[n] = jnp.where(row < wb - 4, pltpu.roll(vc, wb - 4, 0), pltpu.roll(vn, wb - 4, 0))


def _swa_sample(qbd, kc, vc, kn, vn, sink_rows):
    n_seq, wb, _ = kc.shape
    sg = SEQ_GROUP
    seq = lambda r, w: pl.BlockSpec((sg, r, w), lambda i: (i, 0, 0))
    return pl.pallas_call(
        _swa_sample_kernel,
        grid=(n_seq // sg,),
        in_specs=[seq(SAMPLE_ROWS, KV_W), seq(wb, KV_W), seq(wb, KV_W), seq(NEW_PAD, KV_W), seq(NEW_PAD, KV_W),
                  pl.BlockSpec((SAMPLE_ROWS, 1), lambda i: (0, 0))],
        out_specs=[seq(SAMPLE_ROWS, KV_W), seq(wb, KV_W), seq(wb, KV_W)],
        out_shape=[jax.ShapeDtypeStruct((n_seq, SAMPLE_ROWS, KV_W), f32),
                   jax.ShapeDtypeStruct((n_seq, wb, KV_W), f32),
                   jax.ShapeDtypeStruct((n_seq, wb, KV_W), f32)],
        compiler_params=_cparams(1),
        name="swa_sample",
    )(qbd, kc, vc, kn, vn, sink_rows)


def _mem_sample_kernel(q_ref, k_ref, v_ref, o_ref):
    for n in range(SEQ_GROUP):
        s = _dot_nt(q_ref[n], k_ref[n].astype(bf16))
        m = jnp.max(s, axis=1, keepdims=True)
        e = jnp.exp(s - m)
        den = jnp.sum(e, axis=1, keepdims=True)
        o_ref[n] = _dot(e.astype(bf16), v_ref[n].astype(bf16)) / den


def _mem_sample(qbd, km, vm):
    n_seq = km.shape[0]
    sg = SEQ_GROUP
    rows = qbd.shape[1]
    return pl.pallas_call(
        _mem_sample_kernel,
        grid=(n_seq // sg,),
        in_specs=[pl.BlockSpec((sg, rows, MEM_W), lambda i: (i, 0, 0)),
                  pl.BlockSpec((sg, N_MEM, MEM_W), lambda i: (i, 0, 0)),
                  pl.BlockSpec((sg, N_MEM, MEM_W), lambda i: (i, 0, 0))],
        out_specs=pl.BlockSpec((sg, rows, MEM_W), lambda i: (i, 0, 0)),
        out_shape=jax.ShapeDtypeStruct((n_seq, rows, MEM_W), f32),
        compiler_params=_cparams(1),
        name="mem_sample",
    )(qbd, km, vm)


def _moba_sample_kernel(pt_ref, q_ref, qf_ref, kn_ref, vn_ref, expand_ref, kpool_ref, vpool_ref, o_ref,
                        kbuf, vbuf, sem):
    b = pl.program_id(0)
    n_seq = pl.num_programs(0)
    n_pages = pt_ref.shape[1]
    page = kpool_ref.shape[1]
    past = n_pages * page
    nb = past // MOBA_BLOCK
    slot = b % 2

    def page_copy(which, seq, slot_, pg):
        pool, buf = ((kpool_ref, kbuf), (vpool_ref, vbuf))[which]
        dst = pl.ds(pl.multiple_of(pg * page, page), page)
        return pltpu.make_async_copy(pool.at[pt_ref[seq, pg]], buf.at[slot_, dst], sem.at[which, slot_])

    def fetch(seq, slot_):
        def issue(pg, c):
            page_copy(0, seq, slot_, pg).start()
            page_copy(1, seq, slot_, pg).start()
            return c
        lax.fori_loop(0, n_pages, issue, 0)

    def wait(which, seq, slot_):
        def one(pg, c):
            page_copy(which, seq, slot_, pg).wait()
            return c
        lax.fori_loop(0, n_pages, one, 0)

    @pl.when(b == 0)
    def _():
        fetch(0, 0)

    @pl.when(b + 1 < n_seq)
    def _():
        fetch(b + 1, 1 - slot)

    wait(0, b, slot)
    kf = kbuf[slot]
    kmean = jnp.sum(kf.reshape(nb, MOBA_BLOCK, KV_W), axis=1) * (1.0 / MOBA_BLOCK)
    gate = _dot_nt(qf_ref[0], kmean, precision=lax.Precision.HIGHEST)
    bias = _top3_bias(gate, jnp.ones(gate.shape, jnp.bool_), 1)
    chosen = jnp.where(bias == 0.0, 1.0, 0.0).astype(bf16)
    keep = _dot(chosen, expand_ref[...])
    q = q_ref[0]
    s = jnp.where(keep > 0.5, _dot_nt(q, kf.astype(bf16)), NEG_INF)
    col = lax.broadcasted_iota(jnp.int32, (SAMPLE_ROWS, LANES), 1)
    s_idx = lax.broadcasted_iota(jnp.int32, (SAMPLE_ROWS, 1), 0) % 4
    kn = _pad_rows(kn_ref[0], LANES).astype(bf16)
    vn = _pad_rows(vn_ref[0], LANES).astype(bf16)
    s_n = jnp.where(col <= s_idx, _dot_nt(q, kn), NEG_INF)
    m = jnp.maximum(jnp.max(s, axis=1, keepdims=True), jnp.max(s_n, axis=1, keepdims=True))
    e = jnp.exp(s - m)
    e_n = jnp.exp(s_n - m)
    den = jnp.sum(e, axis=1, keepdims=True) + jnp.sum(e_n, axis=1, keepdims=True)
    wait(1, b, slot)
    o = _dot(e.astype(bf16), vbuf[slot].astype(bf16)) + _dot(e_n.astype(bf16), vn)
    o_ref[0] = o / den


def _moba_sample(page_table, qbd, qbd_f32, kn, vn, kpool, vpool):
    n_seq, n_pages = page_table.shape
    page = kpool.shape[1]
    past = n_pages * page
    nb = past // MOBA_BLOCK
    expand = (jnp.arange(past, dtype=jnp.int32)[None, :] // MOBA_BLOCK == jnp.arange(nb, dtype=jnp.int32)[:, None]).astype(bf16)
    seq = lambda r: pl.BlockSpec((1, r, KV_W), lambda i, pt: (i, 0, 0))
    grid_spec = pltpu.PrefetchScalarGridSpec(
        num_scalar_prefetch=1,
        grid=(n_seq,),
        in_specs=[seq(SAMPLE_ROWS), seq(SAMPLE_ROWS), seq(NEW_PAD), seq(NEW_PAD),
                  pl.BlockSpec((nb, past), lambda i, pt: (0, 0)),
                  pl.BlockSpec(memory_space=pl.ANY), pl.BlockSpec(memory_space=pl.ANY)],
        out_specs=seq(SAMPLE_ROWS),
        scratch_shapes=[pltpu.VMEM((2, past, KV_W), f32), pltpu.VMEM((2, past, KV_W), f32),
                        pltpu.SemaphoreType.DMA((2, 2))],
    )
    return pl.pallas_call(
        _moba_sample_kernel,
        grid_spec=grid_spec,
        out_shape=jax.ShapeDtypeStruct((n_seq, SAMPLE_ROWS, KV_W), f32),
        compiler_params=_cparams(1),
        name="moba_sample",
    )(page_table, qbd, qbd_f32, kn, vn, expand, kpool, vpool)


def _block_diag_rows(a, n_heads, per_kv):
    n_seq, S, _ = a.shape
    n_kv = n_heads // per_kv
    a = a.reshape(n_seq, S, n_kv, per_kv, HEAD_DIM).transpose(0, 2, 3, 1, 4)
    eye = jnp.eye(n_kv, dtype=a.dtype)
    bd = a[:, :, :, :, None, :] * eye[None, :, None, None, :, None]
    return bd.reshape(n_seq, n_heads * S, n_kv * HEAD_DIM)


def _diag_rows_out(o, n_heads, per_kv, S):
    n_seq = o.shape[0]
    n_kv = n_heads // per_kv
    o = o.reshape(n_seq, n_kv, per_kv, S, n_kv, HEAD_DIM)
    o = jnp.stack([o[:, h, :, :, h, :] for h in range(n_kv)], axis=1)
    return o.transpose(0, 3, 1, 2, 4).reshape(n_seq, S, n_heads * HEAD_DIM)


def _layer_weights(layer, w_o, ln1_g, ln1_b, w_ffn_gate, conv_w, conv_b, w_ffn_up, w_ffn_down, ln2_g, ln2_b):
    row = lambda a: a[layer][None, :]
    return dict(
        wo_self=w_o[layer, :Q_W].astype(bf16), wo_mem=w_o[layer, Q_W:].astype(bf16),
        g1=row(ln1_g), b1=row(ln1_b),
        w_gate=w_ffn_gate[layer].astype(bf16), conv_w=conv_w[layer], conv_b=row(conv_b),
        w_up=w_ffn_up[layer].astype(bf16), w_down=w_ffn_down[layer].astype(bf16),
        g2=row(ln2_g), b2=row(ln2_b),
    )


def _conv_injection(state):
    n_seq = state.shape[0]
    z = jnp.zeros((n_seq, 1, D_FF), state.dtype)
    inj1 = jnp.concatenate([state[:, 1:2], z, z, z], axis=1)
    inj2 = jnp.concatenate([state[:, 0:1], state[:, 1:2], z, z], axis=1)
    return inj1.reshape(1, n_seq * 4, D_FF), inj2.reshape(1, n_seq * 4, D_FF)


def kernel(x_prompt, x_sample, cache_win_k, cache_win_v, cache_moba_k, cache_moba_v, page_table, cache_mem_k, cache_mem_v, state_conv, mem_prompt, w_in_a, sink_a, w_in_b, w_kv_shared, w_mem_kv, w_o, ln1_g, ln1_b, w_ffn_gate, conv_w, conv_b, w_ffn_up, w_ffn_down, ln2_g, ln2_b):
    B, T, _ = x_prompt.shape
    Bd, S, _ = x_sample.shape
    n_pool, page = cache_moba_k.shape[:2]
    past = page_table.shape[1] * page
    wb = cache_win_k.shape[2]
    nb = T // TOKEN_TILE

    cos_p, sin_p = _rope_tables(jnp.arange(T, dtype=jnp.int32))
    cos_s, sin_s = _rope_tables(past + (jnp.arange(Bd * S, dtype=jnp.int32) % S))
    w_a = w_in_a[0].astype(bf16)
    w_b = jnp.concatenate([w_in_b[0][:, :Q_W], w_kv_shared, w_in_b[0][:, Q_W:]], axis=1).astype(bf16)
    lws = [_layer_weights(l, w_o, ln1_g, ln1_b, w_ffn_gate, conv_w, conv_b, w_ffn_up, w_ffn_down, ln2_g, ln2_b)
           for l in range(DEPTH)]
    xs = x_sample.reshape(1, Bd * S, D_MODEL)

    mem_kv, mem_k2, mem_vt = _memkv(mem_prompt, w_mem_kv.astype(bf16))
    mem_k_out = mem_kv[..., :MEM_W].reshape(DEPTH, B, N_MEM, MEM_HEADS, HEAD_DIM)
    mem_v_out = mem_kv[..., MEM_W:].reshape(DEPTH, B, N_MEM, MEM_HEADS, HEAD_DIM)

    def sample_rows(a, n_heads, per_kv):
        return _block_diag_rows(a.reshape(Bd, S, n_heads * HEAD_DIM), n_heads, per_kv)

    def pad_new(a):
        return jnp.pad(a.reshape(Bd, S, KV_W), ((0, 0), (0, NEW_PAD - S), (0, 0)))

    def sample_mem(qm, layer):
        qbd = (sample_rows(qm, MEM_HEADS, 1) * SCALE).astype(bf16)
        o = _mem_sample(qbd, cache_mem_k[layer].reshape(Bd, N_MEM, MEM_W), cache_mem_v[layer].reshape(Bd, N_MEM, MEM_W))
        return _diag_rows_out(o, MEM_HEADS, 1, S).reshape(1, Bd * S, MEM_W).astype(bf16)

    qa, ka, va, qma, qpad, qmpad, k5, vt5, _ = _proj(x_prompt, w_a, cos_p, sin_p, True)
    sink_cols = jnp.repeat(sink_a[0].reshape(N_KV_HEADS, GROUP), TOKEN_TILE, axis=1)[:, None, :]
    o_self = _swa_prompt(qpad, k5, vt5, sink_cols)
    o_mem = _mem_attn_prompt(qmpad, mem_k2[0], mem_vt[0])
    xp, tail0 = _merge_ffn(x_prompt, o_self, o_mem, lws[0])
    win_kp = ka[:, T - wb:].reshape(1, B, wb, N_KV_HEADS, HEAD_DIM)
    win_vp = va[:, T - wb:].reshape(1, B, wb, N_KV_HEADS, HEAD_DIM)

    qs, ks, vs, qms = _proj(xs, w_a, cos_s, sin_s, False)
    qbd = (sample_rows(qs, N_HEADS, GROUP) * SCALE).astype(bf16)
    sink_rows = jnp.repeat(sink_a[0], S)[:, None]
    o_s, win_ks, win_vs = _swa_sample(qbd, cache_win_k[0].reshape(Bd, wb, KV_W), cache_win_v[0].reshape(Bd, wb, KV_W),
                                      pad_new(ks), pad_new(vs), sink_rows)
    os_self = _diag_rows_out(o_s, N_HEADS, GROUP, S).reshape(1, Bd * S, Q_W).astype(bf16)
    xs, u0 = _merge_ffn(xs, os_self, sample_mem(qms, 0), lws[0], _conv_injection(state_conv[0]))

    qb, kb, vb, qmb, qpad, qmpad, k5, vt5, kmean = _proj(xp, w_b, cos_p, sin_p, True)
    bias = _moba_gate(qb, kmean.reshape(B, nb, KV_W))
    o_self = _moba_prompt(qpad, k5, vt5, bias)
    o_mem = _mem_attn_prompt(qmpad, mem_k2[1], mem_vt[1])
    yp, tail1 = _merge_ffn(xp, o_self, o_mem, lws[1])

    qs, ks, vs, qms = _proj(xs, w_b, cos_s, sin_s, False)
    qbd_f32 = sample_rows(qs, N_HEADS, GROUP)
    o_s = _moba_sample(page_table, (qbd_f32 * SCALE).astype(bf16), qbd_f32, pad_new(ks), pad_new(vs),
                       cache_moba_k.reshape(n_pool, page, KV_W), cache_moba_v.reshape(n_pool, page, KV_W))
    os_self = _diag_rows_out(o_s, N_HEADS, GROUP, S).reshape(1, Bd * S, Q_W).astype(bf16)
    ys, u1 = _merge_ffn(xs, os_self, sample_mem(qms, 1), lws[1], _conv_injection(state_conv[1]))

    kv_heads = lambda a, lead: a.reshape(lead + (N_KV_HEADS, HEAD_DIM))
    conv_p = jnp.stack([tail0[:, 8 - (CONV_W - 1):], tail1[:, 8 - (CONV_W - 1):]])
    conv_s = jnp.stack([u.reshape(Bd, S, D_FF)[:, S - (CONV_W - 1):] for u in (u0, u1)])
    return (
        yp, ys.reshape(Bd, S, D_MODEL),
        win_kp, win_vp,
        kv_heads(kb, (B, T)), kv_heads(vb, (B, T)),
        mem_k_out, mem_v_out,
        conv_p,
        kv_heads(win_ks, (1, Bd, wb)), kv_heads(win_vs, (1, Bd, wb)),
        kv_heads(ks, (Bd, S)), kv_heads(vs, (Bd, S)),
        conv_s,
    )
```

```python
import functools
import math

import jax
import jax.numpy as jnp
from jax import lax
from jax.experimental import pallas as pl
from jax.experimental.pallas import tpu as pltpu

D_MODEL = 1024
HEAD_DIM = 64
N_HEADS = 12
N_KV_HEADS = 4
GROUP = N_HEADS // N_KV_HEADS
MEM_HEADS = 4
N_MEM = 256
Q_W = N_HEADS * HEAD_DIM
KV_W = N_KV_HEADS * HEAD_DIM
MEM_W = MEM_HEADS * HEAD_DIM
PROJ_W = Q_W + 2 * KV_W + MEM_W
WINDOW = 128
MOBA_BLOCK = 256
MOBA_TOPK = 3
D_FF = 2816
CONV_W = 3
ROPE_THETA = 10000.0
LN_EPS = 1e-5
DEPTH = 2
ALPHA = (2 * DEPTH) ** 0.25
SCALE = HEAD_DIM ** -0.5

LANES = 128
TOKEN_TILE = 256
VMEM_LIMIT = 56 * 1024 * 1024
NEG_INF = float("-inf")

f32 = jnp.float32
bf16 = jnp.bfloat16


def _cparams(n_grid, vmem=VMEM_LIMIT):
    return pltpu.CompilerParams(dimension_semantics=("arbitrary",) * n_grid, vmem_limit_bytes=vmem)


def _dot_nt(a, b, precision=None):
    return lax.dot_general(a, b, (((1,), (1,)), ((), ())), preferred_element_type=f32, precision=precision)


def _dot(a, b):
    return jnp.dot(a, b, preferred_element_type=f32)


def _layer_norm(r, g, b):
    mu = jnp.mean(r, axis=-1, keepdims=True)
    d = r - mu
    var = jnp.mean(d * d, axis=-1, keepdims=True)
    return d * lax.rsqrt(var + LN_EPS) * g + b


def _half_mask(width):
    lane = lax.broadcasted_iota(jnp.int32, (1, width), 1)
    return (lane % HEAD_DIM) < (HEAD_DIM // 2)


def _proj_kernel(x_ref, w_ref, cos_ref, sin_ref, q_ref, k_ref, v_ref, *rest, layouts):
    xb = x_ref[0].astype(bf16)
    y = _dot(xb, w_ref[...])
    qk = y[:, :Q_W + KV_W]
    width = Q_W + KV_W
    reps = width // LANES
    cos = jnp.tile(cos_ref[...], (1, reps))
    sin = jnp.tile(sin_ref[...], (1, reps))
    half = HEAD_DIM // 2
    partner = jnp.where(_half_mask(width), pltpu.roll(qk, width - half, 1), pltpu.roll(qk, half, 1))
    rot = qk * cos + partner * sin
    q = rot[:, :Q_W]
    k = rot[:, Q_W:]
    v = y[:, Q_W + KV_W:Q_W + 2 * KV_W]
    qm = y[:, Q_W + 2 * KV_W:]
    q_ref[0] = q
    if not layouts:
        k_ref[0] = k
        v_ref[0] = v
        rest[0][0] = qm
        return
    kt = k.T
    vt = v.T
    k_ref[0] = kt
    v_ref[0] = vt
    qpad_ref, qmpad_ref, k5_ref, vt5_ref, kmean_ref = rest
    tm = q.shape[0]
    lane = lax.broadcasted_iota(jnp.int32, (1, LANES), 1)
    lo = lane < HEAD_DIM

    def placed(tile, src_half, dst_half):
        t = tile if src_half == dst_half else pltpu.roll(tile, HEAD_DIM, 1)
        keep = lo if dst_half == 0 else jnp.logical_not(lo)
        return jnp.where(keep, t, 0.0)

    for h in range(N_KV_HEADS):
        for g in range(GROUP):
            hq = h * GROUP + g
            tile = q[:, (hq // 2) * LANES:(hq // 2 + 1) * LANES]
            qpad_ref[0, h, g * tm:(g + 1) * tm, :] = (placed(tile, hq % 2, h % 2) * SCALE).astype(bf16)
    for h in range(MEM_HEADS):
        tile = qm[:, (h // 2) * LANES:(h // 2 + 1) * LANES]
        qmpad_ref[0, h] = (placed(tile, h % 2, h % 2) * SCALE).astype(bf16)
    for p in range(KV_W // LANES):
        k5_ref[0, p, 0] = k[:, p * LANES:(p + 1) * LANES].astype(bf16)
    for h in range(N_KV_HEADS):
        vt5_ref[0, h, 0] = vt[h * HEAD_DIM:(h + 1) * HEAD_DIM, :].astype(bf16)
    kmean_ref[0, 0] = jnp.mean(k, axis=0, keepdims=True)


def _proj(x, w, cos, sin, layouts):
    B, T, _ = x.shape
    tm = TOKEN_TILE
    nt = T // tm
    n_tab = cos.shape[0] // tm
    tok = lambda w_: (jax.ShapeDtypeStruct((B, T, w_), f32), pl.BlockSpec((1, tm, w_), lambda b, i: (b, i, 0)))
    feat = lambda w_: (jax.ShapeDtypeStruct((B, w_, T), f32), pl.BlockSpec((1, w_, tm), lambda b, i: (b, 0, i)))
    outs = [tok(Q_W), feat(KV_W), feat(KV_W)] if layouts else [tok(Q_W), tok(KV_W), tok(KV_W), tok(MEM_W)]
    out_shape = [o[0] for o in outs]
    out_specs = [o[1] for o in outs]
    if layouts:
        out_shape += [
            jax.ShapeDtypeStruct((B, N_KV_HEADS, GROUP * T, LANES), bf16),
            jax.ShapeDtypeStruct((B, MEM_HEADS, T, LANES), bf16),
            jax.ShapeDtypeStruct((B, KV_W // LANES, nt, tm, LANES), bf16),
            jax.ShapeDtypeStruct((B, N_KV_HEADS, nt, HEAD_DIM, tm), bf16),
            jax.ShapeDtypeStruct((B, nt, 1, KV_W), f32),
        ]
        out_specs += [
            pl.BlockSpec((1, N_KV_HEADS, GROUP * tm, LANES), lambda b, i: (b, 0, i, 0)),
            pl.BlockSpec((1, MEM_HEADS, tm, LANES), lambda b, i: (b, 0, i, 0)),
            pl.BlockSpec((1, KV_W // LANES, 1, tm, LANES), lambda b, i: (b, 0, i, 0, 0)),
            pl.BlockSpec((1, N_KV_HEADS, 1, HEAD_DIM, tm), lambda b, i: (b, 0, i, 0, 0)),
            pl.BlockSpec((1, 1, 1, KV_W), lambda b, i: (b, i, 0, 0)),
        ]
    return pl.pallas_call(
        functools.partial(_proj_kernel, layouts=layouts),
        grid=(B, nt),
        in_specs=[
            pl.BlockSpec((1, tm, D_MODEL), lambda b, i: (b, i, 0)),
            pl.BlockSpec((D_MODEL, PROJ_W), lambda b, i: (0, 0)),
            pl.BlockSpec((tm, LANES), lambda b, i: (i % n_tab, 0)),
            pl.BlockSpec((tm, LANES), lambda b, i: (i % n_tab, 0)),
        ],
        out_specs=out_specs,
        out_shape=out_shape,
        compiler_params=_cparams(2),
        name="proj_layouts" if layouts else "proj",
    )(x, w, cos, sin)


def _rope_tables(pos):
    half = HEAD_DIM // 2
    inv = jnp.power(jnp.float32(ROPE_THETA), -jnp.arange(half, dtype=f32) / half)
    ang = pos.astype(f32)[:, None] * inv[None, :]
    cos = jnp.cos(ang)
    sin = jnp.sin(ang)
    reps = LANES // HEAD_DIM
    return jnp.tile(jnp.concatenate([cos, cos], axis=1), (1, reps)), jnp.tile(jnp.concatenate([-sin, sin], axis=1), (1, reps))


def _memkv_kernel(m_ref, w_ref, kv_ref, k2_ref, vt_ref):
    y = _dot(m_ref[0].astype(bf16), w_ref[0])
    yt = y.T
    kv_ref[0, 0] = yt
    for p in range(MEM_W // LANES):
        k2_ref[0, 0, p] = y[:, p * LANES:(p + 1) * LANES].astype(bf16)
    for h in range(MEM_HEADS):
        vt_ref[0, 0, h] = yt[MEM_W + h * HEAD_DIM:MEM_W + (h + 1) * HEAD_DIM, :].astype(bf16)


def _memkv(mem, w):
    B = mem.shape[0]
    return pl.pallas_call(
        _memkv_kernel,
        grid=(DEPTH, B),
        in_specs=[
            pl.BlockSpec((1, N_MEM, D_MODEL), lambda l, b: (b, 0, 0)),
            pl.BlockSpec((1, D_MODEL, 2 * MEM_W), lambda l, b: (l, 0, 0)),
        ],
        out_specs=[
            pl.BlockSpec((1, 1, 2 * MEM_W, N_MEM), lambda l, b: (l, b, 0, 0)),
            pl.BlockSpec((1, 1, MEM_W // LANES, N_MEM, LANES), lambda l, b: (l, b, 0, 0, 0)),
            pl.BlockSpec((1, 1, MEM_HEADS, HEAD_DIM, N_MEM), lambda l, b: (l, b, 0, 0, 0)),
        ],
        out_shape=[
            jax.ShapeDtypeStruct((DEPTH, B, 2 * MEM_W, N_MEM), f32),
            jax.ShapeDtypeStruct((DEPTH, B, MEM_W // LANES, N_MEM, LANES), bf16),
            jax.ShapeDtypeStruct((DEPTH, B, MEM_HEADS, HEAD_DIM, N_MEM), bf16),
        ],
        compiler_params=_cparams(2),
        name="memkv",
    )(mem, w)


def _gelu(c):
    return 0.5 * c * (1.0 + lax.erf(c * (1.0 / math.sqrt(2.0))))


def _merge_ffn_kernel(x_ref, os_ref, om_ref, wos_ref, wom_ref, g1_ref, b1_ref, wg_ref, cw_ref, cb_ref, wu_ref,
                      wd_ref, g2_ref, b2_ref, *rest, sample):
    if sample:
        inj1_ref, inj2_ref, y_ref, u_ref = rest
    else:
        y_ref, tail_ref, carry_ref = rest
    x = x_ref[0]
    attn = _dot(os_ref[0], wos_ref[...]) + _dot(om_ref[0], wom_ref[...])
    x1 = _layer_norm(ALPHA * x + attn, g1_ref[...], b1_ref[...])
    x1b = x1.astype(bf16)
    u = _dot(x1b, wg_ref[...])
    up = _dot(x1b, wu_ref[...])
    tm = u.shape[0]
    row = lax.broadcasted_iota(jnp.int32, (tm, 1), 0)
    r1 = pltpu.roll(u, 1, 0)
    r2 = pltpu.roll(u, 2, 0)
    if sample:
        s = row % 4
        prev1 = jnp.where(s >= 1, r1, inj1_ref[0])
        prev2 = jnp.where(s >= 2, r2, inj2_ref[0])
        u_ref[0] = u
    else:
        @pl.when(pl.program_id(1) == 0)
        def _():
            carry_ref[...] = jnp.zeros_like(carry_ref)
        c6 = carry_ref[6:7, :]
        c7 = carry_ref[7:8, :]
        prev1 = jnp.where(row == 0, c7, r1)
        prev2 = jnp.where(row == 0, c6, jnp.where(row == 1, c7, r2))
        carry_ref[...] = u[tm - 8:, :]
        tail_ref[0] = u[tm - 8:, :]
    cw = cw_ref[...]
    c = cb_ref[...] + prev2 * cw[0:1, :] + prev1 * cw[1:2, :] + u * cw[2:3, :]
    hid = (_gelu(c) * up).astype(bf16)
    y_ref[0] = _layer_norm(ALPHA * x1 + _dot(hid, wd_ref[...]), g2_ref[...], b2_ref[...])


def _merge_ffn(x, o_self, o_mem, lw, inj=None):
    B, T, _ = x.shape
    tm = TOKEN_TILE
    nt = T // tm
    sample = inj is not None
    tok = lambda w: pl.BlockSpec((1, tm, w), lambda b, i: (b, i, 0))
    const = lambda shape: pl.BlockSpec(shape, lambda b, i: (0,) * len(shape), pipeline_mode=pl.Buffered(1))
    in_specs = [
        tok(D_MODEL), tok(Q_W), tok(MEM_W),
        const((Q_W, D_MODEL)), const((MEM_W, D_MODEL)), const((1, D_MODEL)), const((1, D_MODEL)),
        const((D_MODEL, D_FF)), const((CONV_W, D_FF)), const((1, D_FF)), const((D_MODEL, D_FF)),
        const((D_FF, D_MODEL)), const((1, D_MODEL)), const((1, D_MODEL)),
    ]
    args = [x, o_self, o_mem, lw["wo_self"], lw["wo_mem"], lw["g1"], lw["b1"], lw["w_gate"], lw["conv_w"],
            lw["conv_b"], lw["w_up"], lw["w_down"], lw["g2"], lw["b2"]]
    if sample:
        in_specs += [tok(D_FF), tok(D_FF)]
        args += list(inj)
        out_specs = [tok(D_MODEL), tok(D_FF)]
        out_shape = [jax.ShapeDtypeStruct((B, T, D_MODEL), f32), jax.ShapeDtypeStruct((B, T, D_FF), f32)]
        scratch = []
    else:
        out_specs = [tok(D_MODEL), pl.BlockSpec((1, 8, D_FF), lambda b, i: (b, 0, 0))]
        out_shape = [jax.ShapeDtypeStruct((B, T, D_MODEL), f32), jax.ShapeDtypeStruct((B, 8, D_FF), f32)]
        scratch = [pltpu.VMEM((8, D_FF), f32)]
    return pl.pallas_call(
        functools.partial(_merge_ffn_kernel, sample=sample),
        grid=(B, nt),
        in_specs=in_specs,
        out_specs=out_specs,
        out_shape=out_shape,
        scratch_shapes=scratch,
        compiler_params=_cparams(2),
        name="merge_ffn_sample" if sample else "merge_ffn",
    )(*args)


def _store_heads_transposed(ot_ref, out_ref):
    out_ref[0] = ot_ref[...].T.astype(out_ref.dtype)


def _swa_prompt_kernel(q_ref, kc_ref, kp_ref, vc_ref, vp_ref, sink_ref, out_ref, ot_ref):
    i = pl.program_id(1)
    tm = TOKEN_TILE
    ncol = GROUP * tm
    key_c = lax.broadcasted_iota(jnp.int32, (tm, ncol), 0)
    key_p = lax.broadcasted_iota(jnp.int32, (WINDOW, ncol), 0)
    qc = lax.broadcasted_iota(jnp.int32, (1, ncol), 1) % tm
    valid_c = (key_c <= qc) & (key_c > qc - WINDOW)
    valid_p = (key_p > qc) & (i > 0)
    for h in range(N_KV_HEADS):
        p = h // 2
        qh = q_ref[0, h]
        s_c = jnp.where(valid_c, _dot_nt(kc_ref[0, p, 0], qh), NEG_INF)
        s_p = jnp.where(valid_p, _dot_nt(kp_ref[0, p, 0], qh), NEG_INF)
        sink = sink_ref[h]
        m = jnp.maximum(jnp.maximum(jnp.max(s_c, axis=0, keepdims=True), jnp.max(s_p, axis=0, keepdims=True)), sink)
        e_c = jnp.exp(s_c - m)
        e_p = jnp.exp(s_p - m)
        den = jnp.sum(e_c, axis=0, keepdims=True) + jnp.sum(e_p, axis=0, keepdims=True) + jnp.exp(sink - m)
        ot = _dot(vc_ref[0, h, 0], e_c.astype(bf16)) + _dot(vp_ref[0, h, 0], e_p.astype(bf16))
        ot = ot / den
        for g in range(GROUP):
            hq = h * GROUP + g
            ot_ref[hq * HEAD_DIM:(hq + 1) * HEAD_DIM, :] = ot[:, g * tm:(g + 1) * tm]
    _store_heads_transposed(ot_ref, out_ref)


def _swa_prompt(qpad, k5, vt5, sink_cols):
    B = qpad.shape[0]
    nt = k5.shape[2]
    tm = TOKEN_TILE
    T = nt * tm
    prev = lambda i: jnp.maximum(i - 1, 0)
    return pl.pallas_call(
        _swa_prompt_kernel,
        grid=(B, nt),
        in_specs=[
            pl.BlockSpec((1, N_KV_HEADS, GROUP * tm, LANES), lambda b, i: (b, 0, i, 0)),
            pl.BlockSpec((1, KV_W // LANES, 1, tm, LANES), lambda b, i: (b, 0, i, 0, 0)),
            pl.BlockSpec((1, KV_W // LANES, 1, WINDOW, LANES), lambda b, i: (b, 0, prev(i), tm // WINDOW - 1, 0)),
            pl.BlockSpec((1, N_KV_HEADS, 1, HEAD_DIM, tm), lambda b, i: (b, 0, i, 0, 0)),
            pl.BlockSpec((1, N_KV_HEADS, 1, HEAD_DIM, WINDOW), lambda b, i: (b, 0, prev(i), 0, tm // WINDOW - 1)),
            pl.BlockSpec((N_KV_HEADS, 1, GROUP * tm), lambda b, i: (0, 0, 0)),
        ],
        out_specs=pl.BlockSpec((1, tm, Q_W), lambda b, i: (b, i, 0)),
        out_shape=jax.ShapeDtypeStruct((B, T, Q_W), bf16),
        scratch_shapes=[pltpu.VMEM((Q_W, tm), f32)],
        compiler_params=_cparams(2),
        name="swa_prompt",
    )(qpad, k5, k5, vt5, vt5, sink_cols)


def _mem_attn_prompt_kernel(q_ref, k_ref, vt_ref, out_ref, ot_ref):
    for h in range(MEM_HEADS):
        s = _dot_nt(k_ref[0, h // 2], q_ref[0, h])
        m = jnp.max(s, axis=0, keepdims=True)
        e = jnp.exp(s - m)
        den = jnp.sum(e, axis=0, keepdims=True)
        ot_ref[h * HEAD_DIM:(h + 1) * HEAD_DIM, :] = _dot(vt_ref[0, h], e.astype(bf16)) / den
    _store_heads_transposed(ot_ref, out_ref)


def _mem_attn_prompt(qmpad, mk2, mvt):
    B, _, T, _ = qmpad.shape
    tm = TOKEN_TILE
    return pl.pallas_call(
        _mem_attn_prompt_kernel,
        grid=(B, T // tm),
        in_specs=[
            pl.BlockSpec((1, MEM_HEADS, tm, LANES), lambda b, i: (b, 0, i, 0)),
            pl.BlockSpec((1, MEM_W // LANES, N_MEM, LANES), lambda b, i: (b, 0, 0, 0)),
            pl.BlockSpec((1, MEM_HEADS, HEAD_DIM, N_MEM), lambda b, i: (b, 0, 0, 0)),
        ],
        out_specs=pl.BlockSpec((1, tm, MEM_W), lambda b, i: (b, i, 0)),
        out_shape=jax.ShapeDtypeStruct((B, T, MEM_W), bf16),
        scratch_shapes=[pltpu.VMEM((MEM_W, tm), f32)],
        compiler_params=_cparams(2),
        name="mem_attn_prompt",
    )(qmpad, mk2, mvt)


def _top3_bias(gate, eligible, axis):
    idx = lax.broadcasted_iota(jnp.int32, gate.shape, axis)
    n = gate.shape[axis]
    rest = jnp.where(eligible, gate, NEG_INF)
    chosen = jnp.zeros(gate.shape, dtype=jnp.bool_)
    for _ in range(MOBA_TOPK):
        best = jnp.max(rest, axis=axis, keepdims=True)
        first = jnp.min(jnp.where(rest == best, idx, n), axis=axis, keepdims=True)
        pick = (idx == first) & (best > NEG_INF)
        chosen = chosen | pick
        rest = jnp.where(pick, NEG_INF, rest)
    return jnp.where(chosen, 0.0, NEG_INF)


def _moba_gate_kernel(q_ref, km_ref, bias_ref):
    i = pl.program_id(1)
    tm = TOKEN_TILE
    nb = km_ref.shape[1]
    q = q_ref[0]
    km = km_ref[0]
    lane = lax.broadcasted_iota(jnp.int32, (1, LANES), 1)
    lo = lane < HEAD_DIM
    blk = lax.broadcasted_iota(jnp.int32, (nb, 1), 0)
    for h in range(N_KV_HEADS):
        kt = km[:, (h // 2) * LANES:(h // 2 + 1) * LANES]
        gates = []
        for g in range(GROUP):
            hq = h * GROUP + g
            kk = kt if hq % 2 == h % 2 else pltpu.roll(kt, HEAD_DIM, 1)
            kk = jnp.where(lo if hq % 2 == 0 else jnp.logical_not(lo), kk, 0.0)
            qt = q[:, (hq // 2) * LANES:(hq // 2 + 1) * LANES]
            gates.append(_dot_nt(kk, qt, precision=lax.Precision.HIGHEST))
        gate = jnp.concatenate(gates, axis=1)
        bias_ref[0, 0, h] = _top3_bias(gate, blk < i, 0)


def _moba_gate(q, kmean):
    B, T, _ = q.shape
    tm = TOKEN_TILE
    nb = T // tm
    return pl.pallas_call(
        _moba_gate_kernel,
        grid=(B, nb),
        in_specs=[
            pl.BlockSpec((1, tm, Q_W), lambda b, i: (b, i, 0)),
            pl.BlockSpec((1, nb, KV_W), lambda b, i: (b, 0, 0)),
        ],
        out_specs=pl.BlockSpec((1, 1, N_KV_HEADS, nb, GROUP * tm), lambda b, i: (b, i, 0, 0, 0)),
        out_shape=jax.ShapeDtypeStruct((B, nb, N_KV_HEADS, nb, GROUP * tm), f32),
        compiler_params=_cparams(2),
        name="moba_gate",
    )(q, kmean)


def _moba_prompt_kernel(q_ref, k_ref, vt_ref, bias_ref, out_ref, ot_ref):
    i = pl.program_id(1)
    tm = TOKEN_TILE
    ncol = GROUP * tm
    key = lax.broadcasted_iota(jnp.int32, (tm, ncol), 0)
    qc = lax.broadcasted_iota(jnp.int32, (1, ncol), 1) % tm
    causal = key <= qc
    for h in range(N_KV_HEADS):
        p = h // 2
        qh = q_ref[0, h]
        s = jnp.where(causal, _dot_nt(k_ref[0, p, i], qh), NEG_INF)
        m0 = jnp.max(s, axis=0, keepdims=True)
        e = jnp.exp(s - m0)
        l0 = jnp.sum(e, axis=0, keepdims=True)
        acc0 = _dot(vt_ref[0, h, i], e.astype(bf16))

        def body(j, carry, h=h, p=p, qh=qh):
            m, l, acc = carry
            s = _dot_nt(k_ref[0, p, j], qh) + bias_ref[0, 0, h, pl.ds(j, 1), :]
            m_new = jnp.maximum(m, jnp.max(s, axis=0, keepdims=True))
            a = jnp.exp(m - m_new)
            e = jnp.exp(s - m_new)
            l = a * l + jnp.sum(e, axis=0, keepdims=True)
            acc = a * acc + _dot(vt_ref[0, h, j], e.astype(bf16))
            return m_new, l, acc

        _, l, acc = lax.fori_loop(0, i, body, (m0, l0, acc0))
        ot = acc / l
        for g in range(GROUP):
            hq = h * GROUP + g
            ot_ref[hq * HEAD_DIM:(hq + 1) * HEAD_DIM, :] = ot[:, g * tm:(g + 1) * tm]
    _store_heads_transposed(ot_ref, out_ref)


def _moba_prompt(qpad, k5, vt5, bias):
    B = qpad.shape[0]
    nb = k5.shape[2]
    tm = TOKEN_TILE
    T = nb * tm
    return pl.pallas_call(
        _moba_prompt_kernel,
        grid=(B, nb),
        in_specs=[
            pl.BlockSpec((1, N_KV_HEADS, GROUP * tm, LANES), lambda b, i: (b, 0, i, 0)),
            pl.BlockSpec((1, KV_W // LANES, nb, tm, LANES), lambda b, i: (b, 0, 0, 0, 0)),
            pl.BlockSpec((1, N_KV_HEADS, nb, HEAD_DIM, tm), lambda b, i: (b, 0, 0, 0, 0)),
            pl.BlockSpec((1, 1, N_KV_HEADS, nb, GROUP * tm), lambda b, i: (b, i, 0, 0, 0)),
        ],
        out_specs=pl.BlockSpec((1, tm, Q_W), lambda b, i: (b, i, 0)),
        out_shape=jax.ShapeDtypeStruct((B, T, Q_W), bf16),
        scratch_shapes=[pltpu.VMEM((Q_W, tm), f32)],
        compiler_params=_cparams(2),
        name="moba_prompt",
    )(qpad, k5, vt5, bias)


SAMPLE_ROWS = N_HEADS * 4
NEW_PAD = 8
SEQ_GROUP = 8


def _pad_rows(a, n):
    return jnp.concatenate([a, jnp.zeros((n - a.shape[0], a.shape[1]), a.dtype)], axis=0)


def _swa_sample_kernel(q_ref, kc_ref, vc_ref, kn_ref, vn_ref, sink_ref, o_ref, nk_ref, nv_ref):
    wb = kc_ref.shape[2]
    col = lax.broadcasted_iota(jnp.int32, (SAMPLE_ROWS, wb), 1)
    s_idx = lax.broadcasted_iota(jnp.int32, (SAMPLE_ROWS, 1), 0) % 4
    valid_c = col > s_idx
    valid_n = col <= s_idx
    lane = lax.broadcasted_iota(jnp.int32, (1, wb), 1)
    sink = sink_ref[...]
    for n in range(SEQ_GROUP):
        q = q_ref[n]
        kc = kc_ref[n]
        vc = vc_ref[n]
        kn = _pad_rows(kn_ref[n], wb)
        vn = _pad_rows(vn_ref[n], wb)
        s_c = jnp.where(valid_c, _dot(q, kc.astype(bf16)), NEG_INF)
        s_n = jnp.where(valid_n, _dot_nt(q, kn.astype(bf16)), NEG_INF)
        m = jnp.maximum(jnp.maximum(jnp.max(s_c, axis=1, keepdims=True), jnp.max(s_n, axis=1, keepdims=True)), sink)
        e_c = jnp.exp(s_c - m)
        e_n = jnp.exp(s_n - m)
        den = jnp.sum(e_c, axis=1, keepdims=True) + jnp.sum(e_n, axis=1, keepdims=True) + jnp.exp(sink - m)
        o = _dot_nt(e_c.astype(bf16), vc.astype(bf16)) + _dot(e_n.astype(bf16), vn.astype(bf16))
        o_ref[n] = o / den
        nk_ref[n] = jnp.where(lane < wb - 4, pltpu.roll(kc, wb - 4, 1), pltpu.roll(kn.T, wb - 4, 1))
        nv_ref[n] = jnp.where(lane < wb - 4, pltpu.roll(vc, wb - 4, 1), pltpu.roll(vn.T, wb - 4, 1))


def _swa_sample(qbd, kc, vc, kn, vn, sink_rows):
    n_seq, _, wb = kc.shape
    sg = SEQ_GROUP
    seq = lambda r, w: pl.BlockSpec((sg, r, w), lambda i: (i, 0, 0))
    return pl.pallas_call(
        _swa_sample_kernel,
        grid=(n_seq // sg,),
        in_specs=[seq(SAMPLE_ROWS, KV_W), seq(KV_W, wb), seq(KV_W, wb), seq(NEW_PAD, KV_W), seq(NEW_PAD, KV_W),
                  pl.BlockSpec((SAMPLE_ROWS, 1), lambda i: (0, 0))],
        out_specs=[seq(SAMPLE_ROWS, KV_W), seq(KV_W, wb), seq(KV_W, wb)],
        out_shape=[jax.ShapeDtypeStruct((n_seq, SAMPLE_ROWS, KV_W), f32),
                   jax.ShapeDtypeStruct((n_seq, KV_W, wb), f32),
                   jax.ShapeDtypeStruct((n_seq, KV_W, wb), f32)],
        compiler_params=_cparams(1),
        name="swa_sample",
    )(qbd, kc, vc, kn, vn, sink_rows)


def _mem_sample_kernel(q_ref, k_ref, v_ref, o_ref):
    for n in range(SEQ_GROUP):
        s = _dot(q_ref[n], k_ref[0, n].astype(bf16))
        m = jnp.max(s, axis=1, keepdims=True)
        e = jnp.exp(s - m)
        den = jnp.sum(e, axis=1, keepdims=True)
        o_ref[n] = _dot_nt(e.astype(bf16), v_ref[0, n].astype(bf16)) / den


def _mem_sample(qbd, km, vm, layer):
    n_seq = km.shape[1]
    sg = SEQ_GROUP
    rows = qbd.shape[1]
    cache = pl.BlockSpec((1, sg, MEM_W, N_MEM), lambda i: (layer, i, 0, 0))
    return pl.pallas_call(
        _mem_sample_kernel,
        grid=(n_seq // sg,),
        in_specs=[pl.BlockSpec((sg, rows, MEM_W), lambda i: (i, 0, 0)), cache, cache],
        out_specs=pl.BlockSpec((sg, rows, MEM_W), lambda i: (i, 0, 0)),
        out_shape=jax.ShapeDtypeStruct((n_seq, rows, MEM_W), f32),
        compiler_params=_cparams(1),
        name="mem_sample",
    )(qbd, km, vm)


def _moba_sample_kernel(pt_ref, q_ref, qf_ref, kn_ref, vn_ref, expand_ref, kpool_ref, vpool_ref, o_ref,
                        kbuf, vbuf, sem):
    b = pl.program_id(0)
    n_seq = pl.num_programs(0)
    n_pages = pt_ref.shape[1]
    page = kpool_ref.shape[2]
    past = n_pages * page
    nb = past // MOBA_BLOCK
    slot = b % 2

    def page_copy(which, seq, slot_, pg):
        pool, buf = ((kpool_ref, kbuf), (vpool_ref, vbuf))[which]
        return pltpu.make_async_copy(pool.at[pt_ref[seq, pg]], buf.at[slot_, :, pg * page:(pg + 1) * page],
                                     sem.at[which, slot_])

    def fetch(seq, slot_):
        for pg in range(n_pages):
            page_copy(0, seq, slot_, pg).start()
        for pg in range(n_pages):
            page_copy(1, seq, slot_, pg).start()

    def wait(which, seq, slot_):
        for pg in range(n_pages):
            page_copy(which, seq, slot_, pg).wait()

    @pl.when(b == 0)
    def _():
        fetch(0, 0)

    @pl.when(b + 1 < n_seq)
    def _():
        fetch(b + 1, 1 - slot)

    wait(0, b, slot)
    lane = lax.broadcasted_iota(jnp.int32, (1, LANES), 1)
    kmean = jnp.zeros((KV_W, LANES), f32)
    for j in range(nb):
        part = kbuf[slot, :, j * MOBA_BLOCK:j * MOBA_BLOCK + LANES]
        for c in range(1, MOBA_BLOCK // LANES):
            part = part + kbuf[slot, :, j * MOBA_BLOCK + c * LANES:j * MOBA_BLOCK + (c + 1) * LANES]
        kmean = jnp.where(lane == j, jnp.sum(part, axis=1, keepdims=True), kmean)
    kmean = kmean * (1.0 / MOBA_BLOCK)
    gate = jnp.dot(qf_ref[0], kmean, preferred_element_type=f32, precision=lax.Precision.HIGHEST)
    bias = _top3_bias(gate, lane < nb, 1)
    chosen = jnp.where(bias == 0.0, 1.0, 0.0).astype(bf16)
    keep = _dot(chosen, expand_ref[...])
    q = q_ref[0]
    s = jnp.where(keep > 0.5, _dot(q, kbuf[slot].astype(bf16)), NEG_INF)
    col = lax.broadcasted_iota(jnp.int32, (SAMPLE_ROWS, LANES), 1)
    s_idx = lax.broadcasted_iota(jnp.int32, (SAMPLE_ROWS, 1), 0) % 4
    kn = _pad_rows(kn_ref[0], LANES).astype(bf16)
    vn = _pad_rows(vn_ref[0], LANES).astype(bf16)
    s_n = jnp.where(col <= s_idx, _dot_nt(q, kn), NEG_INF)
    m = jnp.maximum(jnp.max(s, axis=1, keepdims=True), jnp.max(s_n, axis=1, keepdims=True))
    e = jnp.exp(s - m)
    e_n = jnp.exp(s_n - m)
    den = jnp.sum(e, axis=1, keepdims=True) + jnp.sum(e_n, axis=1, keepdims=True)
    wait(1, b, slot)
    o = _dot_nt(e.astype(bf16), vbuf[slot].astype(bf16)) + _dot(e_n.astype(bf16), vn)
    o_ref[0] = o / den


def _moba_sample(page_table, qbd, qbd_f32, kn, vn, kpool, vpool):
    n_seq, n_pages = page_table.shape
    page = kpool.shape[2]
    past = n_pages * page
    nb = past // MOBA_BLOCK
    expand = (jnp.arange(past, dtype=jnp.int32)[None, :] // MOBA_BLOCK == jnp.arange(LANES, dtype=jnp.int32)[:, None]).astype(bf16)
    seq = lambda r: pl.BlockSpec((1, r, KV_W), lambda i, pt: (i, 0, 0))
    grid_spec = pltpu.PrefetchScalarGridSpec(
        num_scalar_prefetch=1,
        grid=(n_seq,),
        in_specs=[seq(SAMPLE_ROWS), seq(SAMPLE_ROWS), seq(NEW_PAD), seq(NEW_PAD),
                  pl.BlockSpec((LANES, past), lambda i, pt: (0, 0)),
                  pl.BlockSpec(memory_space=pl.ANY), pl.BlockSpec(memory_space=pl.ANY)],
        out_specs=seq(SAMPLE_ROWS),
        scratch_shapes=[pltpu.VMEM((2, KV_W, past), f32), pltpu.VMEM((2, KV_W, past), f32),
                        pltpu.SemaphoreType.DMA((2, 2))],
    )
    return pl.pallas_call(
        _moba_sample_kernel,
        grid_spec=grid_spec,
        out_shape=jax.ShapeDtypeStruct((n_seq, SAMPLE_ROWS, KV_W), f32),
        compiler_params=_cparams(1),
        name="moba_sample",
    )(page_table, qbd, qbd_f32, kn, vn, expand, kpool, vpool)


def _block_diag_rows(a, n_heads, per_kv):
    n_seq, S, _ = a.shape
    n_kv = n_heads // per_kv
    a = a.reshape(n_seq, S, n_kv, per_kv, HEAD_DIM).transpose(0, 2, 3, 1, 4)
    eye = jnp.eye(n_kv, dtype=a.dtype)
    bd = a[:, :, :, :, None, :] * eye[None, :, None, None, :, None]
    return bd.reshape(n_seq, n_heads * S, n_kv * HEAD_DIM)


def _diag_rows_out(o, n_heads, per_kv, S):
    n_seq = o.shape[0]
    n_kv = n_heads // per_kv
    o = o.reshape(n_seq, n_kv, per_kv, S, n_kv, HEAD_DIM)
    o = jnp.stack([o[:, h, :, :, h, :] for h in range(n_kv)], axis=1)
    return o.transpose(0, 3, 1, 2, 4).reshape(n_seq, S, n_heads * HEAD_DIM)


def _layer_weights(layer, w_o, ln1_g, ln1_b, w_ffn_gate, conv_w, conv_b, w_ffn_up, w_ffn_down, ln2_g, ln2_b):
    row = lambda a: a[layer][None, :]
    return dict(
        wo_self=w_o[layer, :Q_W].astype(bf16), wo_mem=w_o[layer, Q_W:].astype(bf16),
        g1=row(ln1_g), b1=row(ln1_b),
        w_gate=w_ffn_gate[layer].astype(bf16), conv_w=conv_w[layer], conv_b=row(conv_b),
        w_up=w_ffn_up[layer].astype(bf16), w_down=w_ffn_down[layer].astype(bf16),
        g2=row(ln2_g), b2=row(ln2_b),
    )


def _conv_injection(state):
    n_seq = state.shape[0]
    z = jnp.zeros((n_seq, 1, D_FF), state.dtype)
    inj1 = jnp.concatenate([state[:, 1:2], z, z, z], axis=1)
    inj2 = jnp.concatenate([state[:, 0:1], state[:, 1:2], z, z], axis=1)
    return inj1.reshape(1, n_seq * 4, D_FF), inj2.reshape(1, n_seq * 4, D_FF)


def kernel(x_prompt, x_sample, cache_win_k, cache_win_v, cache_moba_k, cache_moba_v, page_table, cache_mem_k, cache_mem_v, state_conv, mem_prompt, w_in_a, sink_a, w_in_b, w_kv_shared, w_mem_kv, w_o, ln1_g, ln1_b, w_ffn_gate, conv_w, conv_b, w_ffn_up, w_ffn_down, ln2_g, ln2_b):
    B, T, _ = x_prompt.shape
    Bd, S, _ = x_sample.shape
    n_pool, page = cache_moba_k.shape[:2]
    past = page_table.shape[1] * page
    wb = cache_win_k.shape[2]
    nb = T // TOKEN_TILE

    cos_p, sin_p = _rope_tables(jnp.arange(T, dtype=jnp.int32))
    cos_s, sin_s = _rope_tables(past + (jnp.arange(Bd * S, dtype=jnp.int32) % S))
    w_a = w_in_a[0].astype(bf16)
    w_b = jnp.concatenate([w_in_b[0][:, :Q_W], w_kv_shared, w_in_b[0][:, Q_W:]], axis=1).astype(bf16)
    lws = [_layer_weights(l, w_o, ln1_g, ln1_b, w_ffn_gate, conv_w, conv_b, w_ffn_up, w_ffn_down, ln2_g, ln2_b)
           for l in range(DEPTH)]
    xs = x_sample.reshape(1, Bd * S, D_MODEL)

    def feature_major(a):
        a = jnp.moveaxis(a, -3, -1)
        return a.reshape(a.shape[:-3] + (a.shape[-3] * HEAD_DIM, a.shape[-1]))

    def token_major(a, n_heads):
        a = a.reshape(a.shape[:-2] + (n_heads, HEAD_DIM, a.shape[-1]))
        return jnp.moveaxis(a, -1, -3)

    mem_kvt, mem_k2, mem_vt = _memkv(mem_prompt, w_mem_kv.astype(bf16))
    mem_k_out = token_major(mem_kvt[:, :, :MEM_W], MEM_HEADS)
    mem_v_out = token_major(mem_kvt[:, :, MEM_W:], MEM_HEADS)
    cache_mem_kt = feature_major(cache_mem_k)
    cache_mem_vt = feature_major(cache_mem_v)

    def sample_rows(a, n_heads, per_kv):
        return _block_diag_rows(a.reshape(Bd, S, n_heads * HEAD_DIM), n_heads, per_kv)

    def pad_new(a):
        return jnp.pad(a.reshape(Bd, S, KV_W), ((0, 0), (0, NEW_PAD - S), (0, 0)))

    def sample_mem(qm, layer):
        qbd = (sample_rows(qm, MEM_HEADS, 1) * SCALE).astype(bf16)
        o = _mem_sample(qbd, cache_mem_kt, cache_mem_vt, layer)
        return _diag_rows_out(o, MEM_HEADS, 1, S).reshape(1, Bd * S, MEM_W).astype(bf16)

    _, kat, vat, qpad, qmpad, k5, vt5, _ = _proj(x_prompt, w_a, cos_p, sin_p, True)
    sink_cols = jnp.repeat(sink_a[0].reshape(N_KV_HEADS, GROUP), TOKEN_TILE, axis=1)[:, None, :]
    o_self = _swa_prompt(qpad, k5, vt5, sink_cols)
    o_mem = _mem_attn_prompt(qmpad, mem_k2[0], mem_vt[0])
    xp, tail0 = _merge_ffn(x_prompt, o_self, o_mem, lws[0])
    win_kp = token_major(kat[:, :, T - wb:], N_KV_HEADS)[None]
    win_vp = token_major(vat[:, :, T - wb:], N_KV_HEADS)[None]

    qs, ks, vs, qms = _proj(xs, w_a, cos_s, sin_s, False)
    qbd = (sample_rows(qs, N_HEADS, GROUP) * SCALE).astype(bf16)
    sink_rows = jnp.repeat(sink_a[0], S)[:, None]
    o_s, win_kst, win_vst = _swa_sample(qbd, feature_major(cache_win_k[0]), feature_major(cache_win_v[0]),
                                        pad_new(ks), pad_new(vs), sink_rows)
    os_self = _diag_rows_out(o_s, N_HEADS, GROUP, S).reshape(1, Bd * S, Q_W).astype(bf16)
    xs, u0 = _merge_ffn(xs, os_self, sample_mem(qms, 0), lws[0], _conv_injection(state_conv[0]))

    qb, kbt, vbt, qpad, qmpad, k5, vt5, kmean = _proj(xp, w_b, cos_p, sin_p, True)
    bias = _moba_gate(qb, kmean.reshape(B, nb, KV_W))
    o_self = _moba_prompt(qpad, k5, vt5, bias)
    o_mem = _mem_attn_prompt(qmpad, mem_k2[1], mem_vt[1])
    yp, tail1 = _merge_ffn(xp, o_self, o_mem, lws[1])

    qs, ks, vs, qms = _proj(xs, w_b, cos_s, sin_s, False)
    qbd_f32 = sample_rows(qs, N_HEADS, GROUP)
    o_s = _moba_sample(page_table, (qbd_f32 * SCALE).astype(bf16), qbd_f32, pad_new(ks), pad_new(vs),
                       feature_major(cache_moba_k), feature_major(cache_moba_v))
    os_self = _diag_rows_out(o_s, N_HEADS, GROUP, S).reshape(1, Bd * S, Q_W).astype(bf16)
    ys, u1 = _merge_ffn(xs, os_self, sample_mem(qms, 1), lws[1], _conv_injection(state_conv[1]))

    kv_heads = lambda a, lead: a.reshape(lead + (N_KV_HEADS, HEAD_DIM))
    conv_p = jnp.stack([tail0[:, 8 - (CONV_W - 1):], tail1[:, 8 - (CONV_W - 1):]])
    conv_s = jnp.stack([u.reshape(Bd, S, D_FF)[:, S - (CONV_W - 1):] for u in (u0, u1)])
    return (
        yp, ys.reshape(Bd, S, D_MODEL),
        win_kp, win_vp,
        token_major(kbt, N_KV_HEADS), token_major(vbt, N_KV_HEADS),
        mem_k_out, mem_v_out,
        conv_p,
        token_major(win_kst, N_KV_HEADS)[None], token_major(win_vst, N_KV_HEADS)[None],
        kv_heads(ks, (Bd, S)), kv_heads(vs, (Bd, S)),
        conv_s,
    )
```

```python
import functools
import math

import jax
import jax.numpy as jnp
from jax import lax
from jax.experimental import pallas as pl
from jax.experimental.pallas import tpu as pltpu

D_MODEL = 1024
HEAD_DIM = 64
N_HEADS = 12
N_KV_HEADS = 4
GROUP = N_HEADS // N_KV_HEADS
MEM_HEADS = 4
N_MEM = 256
Q_W = N_HEADS * HEAD_DIM
KV_W = N_KV_HEADS * HEAD_DIM
MEM_W = MEM_HEADS * HEAD_DIM
PROJ_W = Q_W + 2 * KV_W + MEM_W
WINDOW = 128
MOBA_BLOCK = 256
MOBA_TOPK = 3
D_FF = 2816
CONV_W = 3
ROPE_THETA = 10000.0
LN_EPS = 1e-5
DEPTH = 2
ALPHA = (2 * DEPTH) ** 0.25
SCALE = HEAD_DIM ** -0.5
LOG2E = math.log2(math.e)
SCALE_LOG2 = SCALE * LOG2E

LANES = 128
TOKEN_TILE = 256
VMEM_LIMIT = 56 * 1024 * 1024
NEG_INF = float("-inf")

f32 = jnp.float32
bf16 = jnp.bfloat16


def _cparams(n_grid, vmem=VMEM_LIMIT, flags=None):
    return pltpu.CompilerParams(dimension_semantics=("arbitrary",) * n_grid, vmem_limit_bytes=vmem, flags=flags)


def _dot_nt(a, b, precision=None):
    return lax.dot_general(a, b, (((1,), (1,)), ((), ())), preferred_element_type=f32, precision=precision)


def _dot(a, b):
    return jnp.dot(a, b, preferred_element_type=f32)


def _layer_norm(r, g, b):
    mu = jnp.mean(r, axis=-1, keepdims=True)
    d = r - mu
    var = jnp.mean(d * d, axis=-1, keepdims=True)
    return d * lax.rsqrt(var + LN_EPS) * g + b


def _half_mask(width):
    lane = lax.broadcasted_iota(jnp.int32, (1, width), 1)
    return (lane % HEAD_DIM) < (HEAD_DIM // 2)


def _proj_kernel(x_ref, w_ref, cos_ref, sin_ref, q_ref, k_ref, v_ref, *rest, layouts):
    xb = x_ref[0].astype(bf16)
    y = _dot(xb, w_ref[...])
    qk = y[:, :Q_W + KV_W]
    width = Q_W + KV_W
    reps = width // LANES
    cos = jnp.tile(cos_ref[...], (1, reps))
    sin = jnp.tile(sin_ref[...], (1, reps))
    half = HEAD_DIM // 2
    partner = jnp.where(_half_mask(width), pltpu.roll(qk, width - half, 1), pltpu.roll(qk, half, 1))
    rot = qk * cos + partner * sin
    q = rot[:, :Q_W]
    k = rot[:, Q_W:]
    v = y[:, Q_W + KV_W:Q_W + 2 * KV_W]
    qm = y[:, Q_W + 2 * KV_W:]
    q_ref[0] = q
    if not layouts:
        k_ref[0] = k
        v_ref[0] = v
        rest[0][0] = qm
        return
    kt = k.T
    vt = v.T
    k_ref[0] = kt
    v_ref[0] = vt
    qpad_ref, qmpad_ref, k5_ref, vt5_ref, kmean_ref = rest
    tm = q.shape[0]
    lane = lax.broadcasted_iota(jnp.int32, (1, LANES), 1)
    lo = lane < HEAD_DIM

    def placed(tile, src_half, dst_half):
        t = tile if src_half == dst_half else pltpu.roll(tile, HEAD_DIM, 1)
        keep = lo if dst_half == 0 else jnp.logical_not(lo)
        return jnp.where(keep, t, 0.0)

    for h in range(N_KV_HEADS):
        for g in range(GROUP):
            hq = h * GROUP + g
            tile = q[:, (hq // 2) * LANES:(hq // 2 + 1) * LANES]
            qpad_ref[0, h, g * tm:(g + 1) * tm, :] = (placed(tile, hq % 2, h % 2) * SCALE_LOG2).astype(bf16)
    for h in range(MEM_HEADS):
        tile = qm[:, (h // 2) * LANES:(h // 2 + 1) * LANES]
        qmpad_ref[0, h] = (placed(tile, h % 2, h % 2) * SCALE_LOG2).astype(bf16)
    for p in range(KV_W // LANES):
        k5_ref[0, p, 0] = k[:, p * LANES:(p + 1) * LANES].astype(bf16)
    for h in range(N_KV_HEADS):
        vt5_ref[0, h, 0] = vt[h * HEAD_DIM:(h + 1) * HEAD_DIM, :].astype(bf16)
    kmean_ref[0, 0] = jnp.mean(k, axis=0, keepdims=True)


def _proj(x, w, cos, sin, layouts):
    B, T, _ = x.shape
    tm = TOKEN_TILE
    nt = T // tm
    n_tab = cos.shape[0] // tm
    tok = lambda w_: (jax.ShapeDtypeStruct((B, T, w_), f32), pl.BlockSpec((1, tm, w_), lambda b, i: (b, i, 0)))
    feat = lambda w_: (jax.ShapeDtypeStruct((B, w_, T), f32), pl.BlockSpec((1, w_, tm), lambda b, i: (b, 0, i)))
    outs = [tok(Q_W), feat(KV_W), feat(KV_W)] if layouts else [tok(Q_W), tok(KV_W), tok(KV_W), tok(MEM_W)]
    out_shape = [o[0] for o in outs]
    out_specs = [o[1] for o in outs]
    if layouts:
        out_shape += [
            jax.ShapeDtypeStruct((B, N_KV_HEADS, GROUP * T, LANES), bf16),
            jax.ShapeDtypeStruct((B, MEM_HEADS, T, LANES), bf16),
            jax.ShapeDtypeStruct((B, KV_W // LANES, nt, tm, LANES), bf16),
            jax.ShapeDtypeStruct((B, N_KV_HEADS, nt, HEAD_DIM, tm), bf16),
            jax.ShapeDtypeStruct((B, nt, 1, KV_W), f32),
        ]
        out_specs += [
            pl.BlockSpec((1, N_KV_HEADS, GROUP * tm, LANES), lambda b, i: (b, 0, i, 0)),
            pl.BlockSpec((1, MEM_HEADS, tm, LANES), lambda b, i: (b, 0, i, 0)),
            pl.BlockSpec((1, KV_W // LANES, 1, tm, LANES), lambda b, i: (b, 0, i, 0, 0)),
            pl.BlockSpec((1, N_KV_HEADS, 1, HEAD_DIM, tm), lambda b, i: (b, 0, i, 0, 0)),
            pl.BlockSpec((1, 1, 1, KV_W), lambda b, i: (b, i, 0, 0)),
        ]
    return pl.pallas_call(
        functools.partial(_proj_kernel, layouts=layouts),
        grid=(B, nt),
        in_specs=[
            pl.BlockSpec((1, tm, D_MODEL), lambda b, i: (b, i, 0)),
            pl.BlockSpec((D_MODEL, PROJ_W), lambda b, i: (0, 0)),
            pl.BlockSpec((tm, LANES), lambda b, i: (i % n_tab, 0)),
            pl.BlockSpec((tm, LANES), lambda b, i: (i % n_tab, 0)),
        ],
        out_specs=out_specs,
        out_shape=out_shape,
        compiler_params=_cparams(2),
        name="proj_layouts" if layouts else "proj",
    )(x, w, cos, sin)


def _rope_tables(pos):
    half = HEAD_DIM // 2
    inv = jnp.power(jnp.float32(ROPE_THETA), -jnp.arange(half, dtype=f32) / half)
    ang = pos.astype(f32)[:, None] * inv[None, :]
    cos = jnp.cos(ang)
    sin = jnp.sin(ang)
    reps = LANES // HEAD_DIM
    return jnp.tile(jnp.concatenate([cos, cos], axis=1), (1, reps)), jnp.tile(jnp.concatenate([-sin, sin], axis=1), (1, reps))


def _memkv_kernel(m_ref, w_ref, kv_ref, k2_ref, vt_ref):
    y = _dot(m_ref[0].astype(bf16), w_ref[0])
    yt = y.T
    kv_ref[0, 0] = yt
    for p in range(MEM_W // LANES):
        k2_ref[0, 0, p] = y[:, p * LANES:(p + 1) * LANES].astype(bf16)
    for h in range(MEM_HEADS):
        vt_ref[0, 0, h] = yt[MEM_W + h * HEAD_DIM:MEM_W + (h + 1) * HEAD_DIM, :].astype(bf16)


def _memkv(mem, w):
    B = mem.shape[0]
    return pl.pallas_call(
        _memkv_kernel,
        grid=(DEPTH, B),
        in_specs=[
            pl.BlockSpec((1, N_MEM, D_MODEL), lambda l, b: (b, 0, 0)),
            pl.BlockSpec((1, D_MODEL, 2 * MEM_W), lambda l, b: (l, 0, 0)),
        ],
        out_specs=[
            pl.BlockSpec((1, 1, 2 * MEM_W, N_MEM), lambda l, b: (l, b, 0, 0)),
            pl.BlockSpec((1, 1, MEM_W // LANES, N_MEM, LANES), lambda l, b: (l, b, 0, 0, 0)),
            pl.BlockSpec((1, 1, MEM_HEADS, HEAD_DIM, N_MEM), lambda l, b: (l, b, 0, 0, 0)),
        ],
        out_shape=[
            jax.ShapeDtypeStruct((DEPTH, B, 2 * MEM_W, N_MEM), f32),
            jax.ShapeDtypeStruct((DEPTH, B, MEM_W // LANES, N_MEM, LANES), bf16),
            jax.ShapeDtypeStruct((DEPTH, B, MEM_HEADS, HEAD_DIM, N_MEM), bf16),
        ],
        compiler_params=_cparams(2),
        name="memkv",
    )(mem, w)


def _gelu(c):
    return 0.5 * c * (1.0 + lax.erf(c * (1.0 / math.sqrt(2.0))))


def _merge_ffn_kernel(x_ref, os_ref, om_ref, wos_ref, wom_ref, g1_ref, b1_ref, wg_ref, cw_ref, cb_ref, wu_ref,
                      wd_ref, g2_ref, b2_ref, *rest, sample):
    if sample:
        inj1_ref, inj2_ref, y_ref, u_ref = rest
    else:
        y_ref, tail_ref, carry_ref = rest
    x = x_ref[0]
    attn = _dot(os_ref[0], wos_ref[...]) + _dot(om_ref[0], wom_ref[...])
    x1 = _layer_norm(ALPHA * x + attn, g1_ref[...], b1_ref[...])
    x1b = x1.astype(bf16)
    u = _dot(x1b, wg_ref[...])
    up = _dot(x1b, wu_ref[...])
    tm = u.shape[0]
    row = lax.broadcasted_iota(jnp.int32, (tm, 1), 0)
    r1 = pltpu.roll(u, 1, 0)
    r2 = pltpu.roll(u, 2, 0)
    if sample:
        s = row % 4
        prev1 = jnp.where(s >= 1, r1, inj1_ref[0])
        prev2 = jnp.where(s >= 2, r2, inj2_ref[0])
        u_ref[0] = u
    else:
        @pl.when(pl.program_id(1) == 0)
        def _():
            carry_ref[...] = jnp.zeros_like(carry_ref)
        c6 = carry_ref[6:7, :]
        c7 = carry_ref[7:8, :]
        prev1 = jnp.where(row == 0, c7, r1)
        prev2 = jnp.where(row == 0, c6, jnp.where(row == 1, c7, r2))
        carry_ref[...] = u[tm - 8:, :]
        tail_ref[0] = u[tm - 8:, :]
    cw = cw_ref[...]
    c = cb_ref[...] + prev2 * cw[0:1, :] + prev1 * cw[1:2, :] + u * cw[2:3, :]
    hid = (_gelu(c) * up).astype(bf16)
    y_ref[0] = _layer_norm(ALPHA * x1 + _dot(hid, wd_ref[...]), g2_ref[...], b2_ref[...])


def _merge_ffn(x, o_self, o_mem, lw, inj=None):
    B, T, _ = x.shape
    tm = TOKEN_TILE
    nt = T // tm
    sample = inj is not None
    tok = lambda w: pl.BlockSpec((1, tm, w), lambda b, i: (b, i, 0))
    const = lambda shape: pl.BlockSpec(shape, lambda b, i: (0,) * len(shape), pipeline_mode=pl.Buffered(1))
    in_specs = [
        tok(D_MODEL), tok(Q_W), tok(MEM_W),
        const((Q_W, D_MODEL)), const((MEM_W, D_MODEL)), const((1, D_MODEL)), const((1, D_MODEL)),
        const((D_MODEL, D_FF)), const((CONV_W, D_FF)), const((1, D_FF)), const((D_MODEL, D_FF)),
        const((D_FF, D_MODEL)), const((1, D_MODEL)), const((1, D_MODEL)),
    ]
    args = [x, o_self, o_mem, lw["wo_self"], lw["wo_mem"], lw["g1"], lw["b1"], lw["w_gate"], lw["conv_w"],
            lw["conv_b"], lw["w_up"], lw["w_down"], lw["g2"], lw["b2"]]
    if sample:
        in_specs += [tok(D_FF), tok(D_FF)]
        args += list(inj)
        out_specs = [tok(D_MODEL), tok(D_FF)]
        out_shape = [jax.ShapeDtypeStruct((B, T, D_MODEL), f32), jax.ShapeDtypeStruct((B, T, D_FF), f32)]
        scratch = []
    else:
        out_specs = [tok(D_MODEL), pl.BlockSpec((1, 8, D_FF), lambda b, i: (b, 0, 0))]
        out_shape = [jax.ShapeDtypeStruct((B, T, D_MODEL), f32), jax.ShapeDtypeStruct((B, 8, D_FF), f32)]
        scratch = [pltpu.VMEM((8, D_FF), f32)]
    return pl.pallas_call(
        functools.partial(_merge_ffn_kernel, sample=sample),
        grid=(B, nt),
        in_specs=in_specs,
        out_specs=out_specs,
        out_shape=out_shape,
        scratch_shapes=scratch,
        compiler_params=_cparams(2),
        name="merge_ffn_sample" if sample else "merge_ffn",
    )(*args)


def _store_heads_transposed(ot_ref, out_ref):
    out_ref[0] = ot_ref[...].T.astype(out_ref.dtype)


def _swa_prompt_kernel(q_ref, kc_ref, kp_ref, vc_ref, vp_ref, sink_ref, out_ref, ot_ref):
    i = pl.program_id(1)
    tm = TOKEN_TILE
    ncol = GROUP * tm
    key_c = lax.broadcasted_iota(jnp.int32, (tm, ncol), 0)
    key_p = lax.broadcasted_iota(jnp.int32, (WINDOW, ncol), 0)
    qc = lax.broadcasted_iota(jnp.int32, (1, ncol), 1) % tm
    valid_c = (key_c <= qc) & (key_c > qc - WINDOW)
    valid_p = (key_p > qc) & (i > 0)
    for h in range(N_KV_HEADS):
        p = h // 2
        qh = q_ref[0, h]
        s_c = jnp.where(valid_c, _dot_nt(kc_ref[0, p, 0], qh), NEG_INF)
        s_p = jnp.where(valid_p, _dot_nt(kp_ref[0, p, 0], qh), NEG_INF)
        sink = sink_ref[h]
        m = jnp.maximum(jnp.maximum(jnp.max(s_c, axis=0, keepdims=True), jnp.max(s_p, axis=0, keepdims=True)), sink)
        e_c = jnp.exp2(s_c - m)
        e_p = jnp.exp2(s_p - m)
        den = jnp.sum(e_c, axis=0, keepdims=True) + jnp.sum(e_p, axis=0, keepdims=True) + jnp.exp2(sink - m)
        ot = _dot(vc_ref[0, h, 0], e_c.astype(bf16)) + _dot(vp_ref[0, h, 0], e_p.astype(bf16))
        ot = ot / den
        for g in range(GROUP):
            hq = h * GROUP + g
            ot_ref[hq * HEAD_DIM:(hq + 1) * HEAD_DIM, :] = ot[:, g * tm:(g + 1) * tm]
    _store_heads_transposed(ot_ref, out_ref)


def _swa_prompt(qpad, k5, vt5, sink_cols):
    B = qpad.shape[0]
    nt = k5.shape[2]
    tm = TOKEN_TILE
    T = nt * tm
    prev = lambda i: jnp.maximum(i - 1, 0)
    return pl.pallas_call(
        _swa_prompt_kernel,
        grid=(B, nt),
        in_specs=[
            pl.BlockSpec((1, N_KV_HEADS, GROUP * tm, LANES), lambda b, i: (b, 0, i, 0)),
            pl.BlockSpec((1, KV_W // LANES, 1, tm, LANES), lambda b, i: (b, 0, i, 0, 0)),
            pl.BlockSpec((1, KV_W // LANES, 1, WINDOW, LANES), lambda b, i: (b, 0, prev(i), tm // WINDOW - 1, 0)),
            pl.BlockSpec((1, N_KV_HEADS, 1, HEAD_DIM, tm), lambda b, i: (b, 0, i, 0, 0)),
            pl.BlockSpec((1, N_KV_HEADS, 1, HEAD_DIM, WINDOW), lambda b, i: (b, 0, prev(i), 0, tm // WINDOW - 1)),
            pl.BlockSpec((N_KV_HEADS, 1, GROUP * tm), lambda b, i: (0, 0, 0)),
        ],
        out_specs=pl.BlockSpec((1, tm, Q_W), lambda b, i: (b, i, 0)),
        out_shape=jax.ShapeDtypeStruct((B, T, Q_W), bf16),
        scratch_shapes=[pltpu.VMEM((Q_W, tm), f32)],
        compiler_params=_cparams(2),
        name="swa_prompt",
    )(qpad, k5, k5, vt5, vt5, sink_cols)


def _mem_attn_prompt_kernel(q_ref, k_ref, vt_ref, out_ref, ot_ref):
    for h in range(MEM_HEADS):
        s = _dot_nt(k_ref[0, h // 2], q_ref[0, h])
        m = jnp.max(s, axis=0, keepdims=True)
        e = jnp.exp2(s - m)
        den = jnp.sum(e, axis=0, keepdims=True)
        ot_ref[h * HEAD_DIM:(h + 1) * HEAD_DIM, :] = _dot(vt_ref[0, h], e.astype(bf16)) / den
    _store_heads_transposed(ot_ref, out_ref)


def _mem_attn_prompt(qmpad, mk2, mvt):
    B, _, T, _ = qmpad.shape
    tm = TOKEN_TILE
    return pl.pallas_call(
        _mem_attn_prompt_kernel,
        grid=(B, T // tm),
        in_specs=[
            pl.BlockSpec((1, MEM_HEADS, tm, LANES), lambda b, i: (b, 0, i, 0)),
            pl.BlockSpec((1, MEM_W // LANES, N_MEM, LANES), lambda b, i: (b, 0, 0, 0)),
            pl.BlockSpec((1, MEM_HEADS, HEAD_DIM, N_MEM), lambda b, i: (b, 0, 0, 0)),
        ],
        out_specs=pl.BlockSpec((1, tm, MEM_W), lambda b, i: (b, i, 0)),
        out_shape=jax.ShapeDtypeStruct((B, T, MEM_W), bf16),
        scratch_shapes=[pltpu.VMEM((MEM_W, tm), f32)],
        compiler_params=_cparams(2),
        name="mem_attn_prompt",
    )(qmpad, mk2, mvt)


def _top3_bias(gate, eligible, axis):
    idx = lax.broadcasted_iota(jnp.int32, gate.shape, axis)
    n = gate.shape[axis]
    rest = jnp.where(eligible, gate, NEG_INF)
    chosen = jnp.zeros(gate.shape, dtype=jnp.bool_)
    for _ in range(MOBA_TOPK):
        best = jnp.max(rest, axis=axis, keepdims=True)
        first = jnp.min(jnp.where(rest == best, idx, n), axis=axis, keepdims=True)
        pick = (idx == first) & (best > NEG_INF)
        chosen = chosen | pick
        rest = jnp.where(pick, NEG_INF, rest)
    return jnp.where(chosen, 0.0, NEG_INF)


def _moba_gate_kernel(q_ref, km_ref, bias_ref):
    i = pl.program_id(1)
    tm = TOKEN_TILE
    nb = km_ref.shape[1]
    q = q_ref[0]
    km = km_ref[0]
    lane = lax.broadcasted_iota(jnp.int32, (1, LANES), 1)
    lo = lane < HEAD_DIM
    blk = lax.broadcasted_iota(jnp.int32, (nb, 1), 0)
    for h in range(N_KV_HEADS):
        kt = km[:, (h // 2) * LANES:(h // 2 + 1) * LANES]
        gates = []
        for g in range(GROUP):
            hq = h * GROUP + g
            kk = kt if hq % 2 == h % 2 else pltpu.roll(kt, HEAD_DIM, 1)
            kk = jnp.where(lo if hq % 2 == 0 else jnp.logical_not(lo), kk, 0.0)
            qt = q[:, (hq // 2) * LANES:(hq // 2 + 1) * LANES]
            gates.append(_dot_nt(kk, qt, precision=lax.Precision.HIGHEST))
        gate = jnp.concatenate(gates, axis=1)
        bias_ref[0, 0, h] = _top3_bias(gate, blk < i, 0)


def _moba_gate(q, kmean):
    B, T, _ = q.shape
    tm = TOKEN_TILE
    nb = T // tm
    return pl.pallas_call(
        _moba_gate_kernel,
        grid=(B, nb),
        in_specs=[
            pl.BlockSpec((1, tm, Q_W), lambda b, i: (b, i, 0)),
            pl.BlockSpec((1, nb, KV_W), lambda b, i: (b, 0, 0)),
        ],
        out_specs=pl.BlockSpec((1, 1, N_KV_HEADS, nb, GROUP * tm), lambda b, i: (b, i, 0, 0, 0)),
        out_shape=jax.ShapeDtypeStruct((B, nb, N_KV_HEADS, nb, GROUP * tm), f32),
        compiler_params=_cparams(2),
        name="moba_gate",
    )(q, kmean)


def _moba_prompt_kernel(q_ref, k_ref, vt_ref, bias_ref, out_ref, acc_ref, m_ref, l_ref, sa_ref, sb_ref):
    i = pl.program_id(1)
    tm = TOKEN_TILE
    nb = k_ref.shape[2]
    n_pairs = KV_W // LANES
    per_pair = N_HEADS // n_pairs
    key = lax.broadcasted_iota(jnp.int32, (tm, tm), 0)
    qc = lax.broadcasted_iota(jnp.int32, (1, tm), 1)
    causal = key <= qc

    def scores_into(j, s_ref):
        for p in range(n_pairs):
            qp = q_ref[0, 2 * p:2 * p + 2].reshape(per_pair * tm, LANES)
            s_ref[p] = _dot_nt(k_ref[0, p, j], qp)

    def chunk(s_ref, c):
        return s_ref[c // per_pair, :, (c % per_pair) * tm:(c % per_pair + 1) * tm]

    scores_into(i, sb_ref)
    scores_into(0, sa_ref)
    for c in range(N_HEADS):
        s = jnp.where(causal, chunk(sb_ref, c), NEG_INF)
        m = jnp.max(s, axis=0, keepdims=True)
        e = jnp.exp2(s - m)
        m_ref[c] = m
        l_ref[c] = jnp.sum(e, axis=0, keepdims=True)
        acc_ref[c] = _dot(vt_ref[0, c // GROUP, i], e.astype(bf16))

    def consume(j, s_ref):
        for c in range(N_HEADS):
            h, g = divmod(c, GROUP)
            s = chunk(s_ref, c) + bias_ref[0, 0, h, pl.ds(j, 1), g * tm:(g + 1) * tm]
            m = m_ref[c]
            m_new = jnp.maximum(m, jnp.max(s, axis=0, keepdims=True))
            a = jnp.exp2(m - m_new)
            e = jnp.exp2(s - m_new)
            m_ref[c] = m_new
            l_ref[c] = a * l_ref[c] + jnp.sum(e, axis=0, keepdims=True)
            acc_ref[c] = a * acc_ref[c] + _dot(vt_ref[0, h, j], e.astype(bf16))

    def body(jj, carry):
        j = 2 * jj
        scores_into(j + 1, sb_ref)
        consume(j, sa_ref)
        scores_into(jnp.minimum(j + 2, nb - 1), sa_ref)
        consume(j + 1, sb_ref)
        return carry

    lax.fori_loop(0, (i + 1) // 2, body, 0)
    for c in range(N_HEADS):
        acc_ref[c] = acc_ref[c] / l_ref[c]
    out_ref[0] = acc_ref[...].reshape(Q_W, tm).T.astype(out_ref.dtype)


def _moba_prompt(qpad, k5, vt5, bias):
    B = qpad.shape[0]
    nb = k5.shape[2]
    tm = TOKEN_TILE
    T = nb * tm
    return pl.pallas_call(
        _moba_prompt_kernel,
        grid=(B, nb),
        in_specs=[
            pl.BlockSpec((1, N_KV_HEADS, GROUP * tm, LANES), lambda b, i: (b, 0, i, 0)),
            pl.BlockSpec((1, KV_W // LANES, nb, tm, LANES), lambda b, i: (b, 0, 0, 0, 0)),
            pl.BlockSpec((1, N_KV_HEADS, nb, HEAD_DIM, tm), lambda b, i: (b, 0, 0, 0, 0)),
            pl.BlockSpec((1, 1, N_KV_HEADS, nb, GROUP * tm), lambda b, i: (b, i, 0, 0, 0)),
        ],
        out_specs=pl.BlockSpec((1, tm, Q_W), lambda b, i: (b, i, 0)),
        out_shape=jax.ShapeDtypeStruct((B, T, Q_W), bf16),
        scratch_shapes=[pltpu.VMEM((N_HEADS, HEAD_DIM, tm), f32), pltpu.VMEM((N_HEADS, 1, tm), f32),
                        pltpu.VMEM((N_HEADS, 1, tm), f32),
                        pltpu.VMEM((KV_W // LANES, tm, N_HEADS * tm * LANES // KV_W), f32),
                        pltpu.VMEM((KV_W // LANES, tm, N_HEADS * tm * LANES // KV_W), f32)],
        compiler_params=_cparams(2),
        name="moba_prompt",
    )(qpad, k5, vt5, bias)


SAMPLE_ROWS = N_HEADS * 4
NEW_PAD = 8
SEQ_GROUP = 8


def _pad_rows(a, n):
    return jnp.concatenate([a, jnp.zeros((n - a.shape[0], a.shape[1]), a.dtype)], axis=0)


def _swa_sample_kernel(q_ref, kc_ref, vc_ref, kn_ref, vn_ref, sink_ref, o_ref, nk_ref, nv_ref):
    wb = kc_ref.shape[2]
    col = lax.broadcasted_iota(jnp.int32, (SAMPLE_ROWS, wb), 1)
    s_idx = lax.broadcasted_iota(jnp.int32, (SAMPLE_ROWS, 1), 0) % 4
    valid_c = col > s_idx
    valid_n = col <= s_idx
    lane = lax.broadcasted_iota(jnp.int32, (1, wb), 1)
    sink = sink_ref[...]
    for n in range(SEQ_GROUP):
        q = q_ref[n]
        kc = kc_ref[n]
        vc = vc_ref[n]
        kn = _pad_rows(kn_ref[n], wb)
        vn = _pad_rows(vn_ref[n], wb)
        s_c = jnp.where(valid_c, _dot(q, kc.astype(bf16)), NEG_INF)
        s_n = jnp.where(valid_n, _dot_nt(q, kn.astype(bf16)), NEG_INF)
        m = jnp.maximum(jnp.maximum(jnp.max(s_c, axis=1, keepdims=True), jnp.max(s_n, axis=1, keepdims=True)), sink)
        e_c = jnp.exp(s_c - m)
        e_n = jnp.exp(s_n - m)
        den = jnp.sum(e_c, axis=1, keepdims=True) + jnp.sum(e_n, axis=1, keepdims=True) + jnp.exp(sink - m)
        o = _dot_nt(e_c.astype(bf16), vc.astype(bf16)) + _dot(e_n.astype(bf16), vn.astype(bf16))
        o_ref[n] = o / den
        nk_ref[n] = jnp.where(lane < wb - 4, pltpu.roll(kc, wb - 4, 1), pltpu.roll(kn.T, wb - 4, 1))
        nv_ref[n] = jnp.where(lane < wb - 4, pltpu.roll(vc, wb - 4, 1), pltpu.roll(vn.T, wb - 4, 1))


def _swa_sample(qbd, kc, vc, kn, vn, sink_rows):
    n_seq, _, wb = kc.shape
    sg = SEQ_GROUP
    seq = lambda r, w: pl.BlockSpec((sg, r, w), lambda i: (i, 0, 0))
    return pl.pallas_call(
        _swa_sample_kernel,
        grid=(n_seq // sg,),
        in_specs=[seq(SAMPLE_ROWS, KV_W), seq(KV_W, wb), seq(KV_W, wb), seq(NEW_PAD, KV_W), seq(NEW_PAD, KV_W),
                  pl.BlockSpec((SAMPLE_ROWS, 1), lambda i: (0, 0))],
        out_specs=[seq(SAMPLE_ROWS, KV_W), seq(KV_W, wb), seq(KV_W, wb)],
        out_shape=[jax.ShapeDtypeStruct((n_seq, SAMPLE_ROWS, KV_W), f32),
                   jax.ShapeDtypeStruct((n_seq, KV_W, wb), f32),
                   jax.ShapeDtypeStruct((n_seq, KV_W, wb), f32)],
        compiler_params=_cparams(1),
        name="swa_sample",
    )(qbd, kc, vc, kn, vn, sink_rows)


def _mem_sample_kernel(q_ref, k_ref, v_ref, o_ref):
    for n in range(SEQ_GROUP):
        s = _dot(q_ref[n], k_ref[0, n].astype(bf16))
        m = jnp.max(s, axis=1, keepdims=True)
        e = jnp.exp(s - m)
        den = jnp.sum(e, axis=1, keepdims=True)
        o_ref[n] = _dot_nt(e.astype(bf16), v_ref[0, n].astype(bf16)) / den


def _mem_sample(qbd, km, vm, layer):
    n_seq = km.shape[1]
    sg = SEQ_GROUP
    rows = qbd.shape[1]
    cache = pl.BlockSpec((1, sg, MEM_W, N_MEM), lambda i: (layer, i, 0, 0))
    return pl.pallas_call(
        _mem_sample_kernel,
        grid=(n_seq // sg,),
        in_specs=[pl.BlockSpec((sg, rows, MEM_W), lambda i: (i, 0, 0)), cache, cache],
        out_specs=pl.BlockSpec((sg, rows, MEM_W), lambda i: (i, 0, 0)),
        out_shape=jax.ShapeDtypeStruct((n_seq, rows, MEM_W), f32),
        compiler_params=_cparams(1),
        name="mem_sample",
    )(qbd, km, vm)


def _moba_sample_kernel(pt_ref, q_ref, qf_ref, kn_ref, vn_ref, expand_ref, kpool_ref, vpool_ref, o_ref,
                        kbuf, vbuf, sem):
    b = pl.program_id(0)
    n_seq = pl.num_programs(0)
    n_pages = pt_ref.shape[1]
    page = kpool_ref.shape[2]
    past = n_pages * page
    nb = past // MOBA_BLOCK
    slot = b % 2

    def page_copy(which, seq, slot_, pg):
        pool, buf = ((kpool_ref, kbuf), (vpool_ref, vbuf))[which]
        return pltpu.make_async_copy(pool.at[pt_ref[seq, pg]], buf.at[slot_, :, pg * page:(pg + 1) * page],
                                     sem.at[which, slot_])

    def fetch(seq, slot_):
        for pg in range(n_pages):
            page_copy(0, seq, slot_, pg).start()
        for pg in range(n_pages):
            page_copy(1, seq, slot_, pg).start()

    def wait(which, seq, slot_):
        for pg in range(n_pages):
            page_copy(which, seq, slot_, pg).wait()

    @pl.when(b == 0)
    def _():
        fetch(0, 0)

    @pl.when(b + 1 < n_seq)
    def _():
        fetch(b + 1, 1 - slot)

    wait(0, b, slot)
    lane = lax.broadcasted_iota(jnp.int32, (1, LANES), 1)
    kmean = jnp.zeros((KV_W, LANES), f32)
    for j in range(nb):
        part = kbuf[slot, :, j * MOBA_BLOCK:j * MOBA_BLOCK + LANES]
        for c in range(1, MOBA_BLOCK // LANES):
            part = part + kbuf[slot, :, j * MOBA_BLOCK + c * LANES:j * MOBA_BLOCK + (c + 1) * LANES]
        kmean = jnp.where(lane == j, jnp.sum(part, axis=1, keepdims=True), kmean)
    kmean = kmean * (1.0 / MOBA_BLOCK)
    gate = jnp.dot(qf_ref[0], kmean, preferred_element_type=f32, precision=lax.Precision.HIGHEST)
    bias = _top3_bias(gate, lane < nb, 1)
    chosen = jnp.where(bias == 0.0, 1.0, 0.0).astype(bf16)
    keep = _dot(chosen, expand_ref[...])
    q = q_ref[0]
    s = jnp.where(keep > 0.5, _dot(q, kbuf[slot].astype(bf16)), NEG_INF)
    col = lax.broadcasted_iota(jnp.int32, (SAMPLE_ROWS, LANES), 1)
    s_idx = lax.broadcasted_iota(jnp.int32, (SAMPLE_ROWS, 1), 0) % 4
    kn = _pad_rows(kn_ref[0], LANES).astype(bf16)
    vn = _pad_rows(vn_ref[0], LANES).astype(bf16)
    s_n = jnp.where(col <= s_idx, _dot_nt(q, kn), NEG_INF)
    m = jnp.maximum(jnp.max(s, axis=1, keepdims=True), jnp.max(s_n, axis=1, keepdims=True))
    e = jnp.exp(s - m)
    e_n = jnp.exp(s_n - m)
    den = jnp.sum(e, axis=1, keepdims=True) + jnp.sum(e_n, axis=1, keepdims=True)
    wait(1, b, slot)
    o = _dot_nt(e.astype(bf16), vbuf[slot].astype(bf16)) + _dot(e_n.astype(bf16), vn)
    o_ref[0] = o / den


def _moba_sample(page_table, qbd, qbd_f32, kn, vn, kpool, vpool):
    n_seq, n_pages = page_table.shape
    page = kpool.shape[2]
    past = n_pages * page
    nb = past // MOBA_BLOCK
    expand = (jnp.arange(past, dtype=jnp.int32)[None, :] // MOBA_BLOCK == jnp.arange(LANES, dtype=jnp.int32)[:, None]).astype(bf16)
    seq = lambda r: pl.BlockSpec((1, r, KV_W), lambda i, pt: (i, 0, 0))
    grid_spec = pltpu.PrefetchScalarGridSpec(
        num_scalar_prefetch=1,
        grid=(n_seq,),
        in_specs=[seq(SAMPLE_ROWS), seq(SAMPLE_ROWS), seq(NEW_PAD), seq(NEW_PAD),
                  pl.BlockSpec((LANES, past), lambda i, pt: (0, 0)),
                  pl.BlockSpec(memory_space=pl.ANY), pl.BlockSpec(memory_space=pl.ANY)],
        out_specs=seq(SAMPLE_ROWS),
        scratch_shapes=[pltpu.VMEM((2, KV_W, past), f32), pltpu.VMEM((2, KV_W, past), f32),
                        pltpu.SemaphoreType.DMA((2, 2))],
    )
    return pl.pallas_call(
        _moba_sample_kernel,
        grid_spec=grid_spec,
        out_shape=jax.ShapeDtypeStruct((n_seq, SAMPLE_ROWS, KV_W), f32),
        compiler_params=_cparams(1),
        name="moba_sample",
    )(page_table, qbd, qbd_f32, kn, vn, expand, kpool, vpool)


def _block_diag_rows(a, n_heads, per_kv):
    n_seq, S, _ = a.shape
    n_kv = n_heads // per_kv
    a = a.reshape(n_seq, S, n_kv, per_kv, HEAD_DIM).transpose(0, 2, 3, 1, 4)
    eye = jnp.eye(n_kv, dtype=a.dtype)
    bd = a[:, :, :, :, None, :] * eye[None, :, None, None, :, None]
    return bd.reshape(n_seq, n_heads * S, n_kv * HEAD_DIM)


def _diag_rows_out(o, n_heads, per_kv, S):
    n_seq = o.shape[0]
    n_kv = n_heads // per_kv
    o = o.reshape(n_seq, n_kv, per_kv, S, n_kv, HEAD_DIM)
    o = jnp.stack([o[:, h, :, :, h, :] for h in range(n_kv)], axis=1)
    return o.transpose(0, 3, 1, 2, 4).reshape(n_seq, S, n_heads * HEAD_DIM)


def _layer_weights(layer, w_o, ln1_g, ln1_b, w_ffn_gate, conv_w, conv_b, w_ffn_up, w_ffn_down, ln2_g, ln2_b):
    row = lambda a: a[layer][None, :]
    return dict(
        wo_self=w_o[layer, :Q_W].astype(bf16), wo_mem=w_o[layer, Q_W:].astype(bf16),
        g1=row(ln1_g), b1=row(ln1_b),
        w_gate=w_ffn_gate[layer].astype(bf16), conv_w=conv_w[layer], conv_b=row(conv_b),
        w_up=w_ffn_up[layer].astype(bf16), w_down=w_ffn_down[layer].astype(bf16),
        g2=row(ln2_g), b2=row(ln2_b),
    )


def _conv_injection(state):
    n_seq = state.shape[0]
    z = jnp.zeros((n_seq, 1, D_FF), state.dtype)
    inj1 = jnp.concatenate([state[:, 1:2], z, z, z], axis=1)
    inj2 = jnp.concatenate([state[:, 0:1], state[:, 1:2], z, z], axis=1)
    return inj1.reshape(1, n_seq * 4, D_FF), inj2.reshape(1, n_seq * 4, D_FF)


def kernel(x_prompt, x_sample, cache_win_k, cache_win_v, cache_moba_k, cache_moba_v, page_table, cache_mem_k, cache_mem_v, state_conv, mem_prompt, w_in_a, sink_a, w_in_b, w_kv_shared, w_mem_kv, w_o, ln1_g, ln1_b, w_ffn_gate, conv_w, conv_b, w_ffn_up, w_ffn_down, ln2_g, ln2_b):
    B, T, _ = x_prompt.shape
    Bd, S, _ = x_sample.shape
    n_pool, page = cache_moba_k.shape[:2]
    past = page_table.shape[1] * page
    wb = cache_win_k.shape[2]
    nb = T // TOKEN_TILE

    cos_p, sin_p = _rope_tables(jnp.arange(T, dtype=jnp.int32))
    cos_s, sin_s = _rope_tables(past + (jnp.arange(Bd * S, dtype=jnp.int32) % S))
    w_a = w_in_a[0].astype(bf16)
    w_b = jnp.concatenate([w_in_b[0][:, :Q_W], w_kv_shared, w_in_b[0][:, Q_W:]], axis=1).astype(bf16)
    lws = [_layer_weights(l, w_o, ln1_g, ln1_b, w_ffn_gate, conv_w, conv_b, w_ffn_up, w_ffn_down, ln2_g, ln2_b)
           for l in range(DEPTH)]
    xs = x_sample.reshape(1, Bd * S, D_MODEL)

    def feature_major(a):
        a = jnp.moveaxis(a, -3, -1)
        return a.reshape(a.shape[:-3] + (a.shape[-3] * HEAD_DIM, a.shape[-1]))

    def token_major(a, n_heads):
        a = a.reshape(a.shape[:-2] + (n_heads, HEAD_DIM, a.shape[-1]))
        return jnp.moveaxis(a, -1, -3)

    mem_kvt, mem_k2, mem_vt = _memkv(mem_prompt, w_mem_kv.astype(bf16))
    mem_k_out = token_major(mem_kvt[:, :, :MEM_W], MEM_HEADS)
    mem_v_out = token_major(mem_kvt[:, :, MEM_W:], MEM_HEADS)
    cache_mem_kt = feature_major(cache_mem_k)
    cache_mem_vt = feature_major(cache_mem_v)

    def sample_rows(a, n_heads, per_kv):
        return _block_diag_rows(a.reshape(Bd, S, n_heads * HEAD_DIM), n_heads, per_kv)

    def pad_new(a):
        return jnp.pad(a.reshape(Bd, S, KV_W), ((0, 0), (0, NEW_PAD - S), (0, 0)))

    def sample_mem(qm, layer):
        qbd = (sample_rows(qm, MEM_HEADS, 1) * SCALE).astype(bf16)
        o = _mem_sample(qbd, cache_mem_kt, cache_mem_vt, layer)
        return _diag_rows_out(o, MEM_HEADS, 1, S).reshape(1, Bd * S, MEM_W).astype(bf16)

    _, kat, vat, qpad, qmpad, k5, vt5, _ = _proj(x_prompt, w_a, cos_p, sin_p, True)
    sink_cols = jnp.repeat(sink_a[0].reshape(N_KV_HEADS, GROUP) * LOG2E, TOKEN_TILE, axis=1)[:, None, :]
    o_self = _swa_prompt(qpad, k5, vt5, sink_cols)
    o_mem = _mem_attn_prompt(qmpad, mem_k2[0], mem_vt[0])
    xp, tail0 = _merge_ffn(x_prompt, o_self, o_mem, lws[0])
    win_kp = token_major(kat[:, :, T - wb:], N_KV_HEADS)[None]
    win_vp = token_major(vat[:, :, T - wb:], N_KV_HEADS)[None]

    qs, ks, vs, qms = _proj(xs, w_a, cos_s, sin_s, False)
    qbd = (sample_rows(qs, N_HEADS, GROUP) * SCALE).astype(bf16)
    sink_rows = jnp.repeat(sink_a[0], S)[:, None]
    o_s, win_kst, win_vst = _swa_sample(qbd, feature_major(cache_win_k[0]), feature_major(cache_win_v[0]),
                                        pad_new(ks), pad_new(vs), sink_rows)
    os_self = _diag_rows_out(o_s, N_HEADS, GROUP, S).reshape(1, Bd * S, Q_W).astype(bf16)
    xs, u0 = _merge_ffn(xs, os_self, sample_mem(qms, 0), lws[0], _conv_injection(state_conv[0]))

    qb, kbt, vbt, qpad, qmpad, k5, vt5, kmean = _proj(xp, w_b, cos_p, sin_p, True)
    bias = _moba_gate(qb, kmean.reshape(B, nb, KV_W))
    o_self = _moba_prompt(qpad, k5, vt5, bias)
    o_mem = _mem_attn_prompt(qmpad, mem_k2[1], mem_vt[1])
    yp, tail1 = _merge_ffn(xp, o_self, o_mem, lws[1])

    qs, ks, vs, qms = _proj(xs, w_b, cos_s, sin_s, False)
    qbd_f32 = sample_rows(qs, N_HEADS, GROUP)
    o_s = _moba_sample(page_table, (qbd_f32 * SCALE).astype(bf16), qbd_f32, pad_new(ks), pad_new(vs),
                       feature_major(cache_moba_k), feature_major(cache_moba_v))
    os_self = _diag_rows_out(o_s, N_HEADS, GROUP, S).reshape(1, Bd * S, Q_W).astype(bf16)
    ys, u1 = _merge_ffn(xs, os_self, sample_mem(qms, 1), lws[1], _conv_injection(state_conv[1]))

    kv_heads = lambda a, lead: a.reshape(lead + (N_KV_HEADS, HEAD_DIM))
    conv_p = jnp.stack([tail0[:, 8 - (CONV_W - 1):], tail1[:, 8 - (CONV_W - 1):]])
    conv_s = jnp.stack([u.reshape(Bd, S, D_FF)[:, S - (CONV_W - 1):] for u in (u0, u1)])
    return (
        yp, ys.reshape(Bd, S, D_MODEL),
        win_kp, win_vp,
        token_major(kbt, N_KV_HEADS), token_major(vbt, N_KV_HEADS),
        mem_k_out, mem_v_out,
        conv_p,
        token_major(win_kst, N_KV_HEADS)[None], token_major(win_vst, N_KV_HEADS)[None],
        kv_heads(ks, (Bd, S)), kv_heads(vs, (Bd, S)),
        conv_s,
    )
```

```python
import functools
import math

import jax
import jax.numpy as jnp
from jax import lax
from jax.experimental import pallas as pl
from jax.experimental.pallas import tpu as pltpu

D_MODEL = 1024
HEAD_DIM = 64
N_HEADS = 12
N_KV_HEADS = 4
GROUP = N_HEADS // N_KV_HEADS
MEM_HEADS = 4
N_MEM = 256
Q_W = N_HEADS * HEAD_DIM
KV_W = N_KV_HEADS * HEAD_DIM
MEM_W = MEM_HEADS * HEAD_DIM
PROJ_W = Q_W + 2 * KV_W + MEM_W
WINDOW = 128
MOBA_BLOCK = 256
MOBA_TOPK = 3
D_FF = 2816
CONV_W = 3
ROPE_THETA = 10000.0
LN_EPS = 1e-5
DEPTH = 2
ALPHA = (2 * DEPTH) ** 0.25
SCALE = HEAD_DIM ** -0.5
LOG2E = math.log2(math.e)
SCALE_LOG2 = SCALE * LOG2E

LANES = 128
V_ROWS = HEAD_DIM + 16
TOKEN_TILE = 256
VMEM_LIMIT = 56 * 1024 * 1024
NEG_INF = float("-inf")

f32 = jnp.float32
bf16 = jnp.bfloat16


def _cparams(n_grid, vmem=VMEM_LIMIT, flags=None):
    return pltpu.CompilerParams(dimension_semantics=("arbitrary",) * n_grid, vmem_limit_bytes=vmem, flags=flags)


def _dot_nt(a, b, precision=None):
    return lax.dot_general(a, b, (((1,), (1,)), ((), ())), preferred_element_type=f32, precision=precision)


def _dot(a, b):
    return jnp.dot(a, b, preferred_element_type=f32)


def _layer_norm(r, g, b):
    mu = jnp.mean(r, axis=-1, keepdims=True)
    d = r - mu
    var = jnp.mean(d * d, axis=-1, keepdims=True)
    return d * lax.rsqrt(var + LN_EPS) * g + b


def _ones_row_block(width):
    row = lax.broadcasted_iota(jnp.int32, (V_ROWS - HEAD_DIM, width), 0)
    return jnp.where(row == 0, 1.0, 0.0).astype(f32)


def _half_mask(width):
    lane = lax.broadcasted_iota(jnp.int32, (1, width), 1)
    return (lane % HEAD_DIM) < (HEAD_DIM // 2)


def _proj_kernel(x_ref, w_ref, cos_ref, sin_ref, q_ref, k_ref, v_ref, *rest, layouts):
    xb = x_ref[0].astype(bf16)
    y = _dot(xb, w_ref[...])
    qk = y[:, :Q_W + KV_W]
    width = Q_W + KV_W
    reps = width // LANES
    cos = jnp.tile(cos_ref[...], (1, reps))
    sin = jnp.tile(sin_ref[...], (1, reps))
    half = HEAD_DIM // 2
    partner = jnp.where(_half_mask(width), pltpu.roll(qk, width - half, 1), pltpu.roll(qk, half, 1))
    rot = qk * cos + partner * sin
    q = rot[:, :Q_W]
    k = rot[:, Q_W:]
    v = y[:, Q_W + KV_W:Q_W + 2 * KV_W]
    qm = y[:, Q_W + 2 * KV_W:]
    q_ref[0] = q
    if not layouts:
        k_ref[0] = k
        v_ref[0] = v
        rest[0][0] = qm
        return
    kt = k.T
    vt = v.T
    k_ref[0] = kt
    v_ref[0] = vt
    qpad_ref, qmpad_ref, k5_ref, vt5_ref, kmean_ref = rest
    tm = q.shape[0]
    lane = lax.broadcasted_iota(jnp.int32, (1, LANES), 1)
    lo = lane < HEAD_DIM

    def placed(tile, src_half, dst_half):
        t = tile if src_half == dst_half else pltpu.roll(tile, HEAD_DIM, 1)
        keep = lo if dst_half == 0 else jnp.logical_not(lo)
        return jnp.where(keep, t, 0.0)

    for h in range(N_KV_HEADS):
        for g in range(GROUP):
            hq = h * GROUP + g
            tile = q[:, (hq // 2) * LANES:(hq // 2 + 1) * LANES]
            qpad_ref[0, h, g * tm:(g + 1) * tm, :] = (placed(tile, hq % 2, h % 2) * SCALE_LOG2).astype(bf16)
    for h in range(MEM_HEADS):
        tile = qm[:, (h // 2) * LANES:(h // 2 + 1) * LANES]
        qmpad_ref[0, h] = (placed(tile, h % 2, h % 2) * SCALE_LOG2).astype(bf16)
    for p in range(KV_W // LANES):
        k5_ref[0, p, 0] = k[:, p * LANES:(p + 1) * LANES].astype(bf16)
    ones_rows = _ones_row_block(tm)
    for h in range(N_KV_HEADS):
        vt5_ref[0, h, 0] = jnp.concatenate([vt[h * HEAD_DIM:(h + 1) * HEAD_DIM, :], ones_rows], axis=0).astype(bf16)
    kmean_ref[0, 0] = jnp.mean(k, axis=0, keepdims=True)


def _proj(x, w, cos, sin, layouts):
    B, T, _ = x.shape
    tm = TOKEN_TILE
    nt = T // tm
    n_tab = cos.shape[0] // tm
    tok = lambda w_: (jax.ShapeDtypeStruct((B, T, w_), f32), pl.BlockSpec((1, tm, w_), lambda b, i: (b, i, 0)))
    feat = lambda w_: (jax.ShapeDtypeStruct((B, w_, T), f32), pl.BlockSpec((1, w_, tm), lambda b, i: (b, 0, i)))
    outs = [tok(Q_W), feat(KV_W), feat(KV_W)] if layouts else [tok(Q_W), tok(KV_W), tok(KV_W), tok(MEM_W)]
    out_shape = [o[0] for o in outs]
    out_specs = [o[1] for o in outs]
    if layouts:
        out_shape += [
            jax.ShapeDtypeStruct((B, N_KV_HEADS, GROUP * T, LANES), bf16),
            jax.ShapeDtypeStruct((B, MEM_HEADS, T, LANES), bf16),
            jax.ShapeDtypeStruct((B, KV_W // LANES, nt, tm, LANES), bf16),
            jax.ShapeDtypeStruct((B, N_KV_HEADS, nt, V_ROWS, tm), bf16),
            jax.ShapeDtypeStruct((B, nt, 1, KV_W), f32),
        ]
        out_specs += [
            pl.BlockSpec((1, N_KV_HEADS, GROUP * tm, LANES), lambda b, i: (b, 0, i, 0)),
            pl.BlockSpec((1, MEM_HEADS, tm, LANES), lambda b, i: (b, 0, i, 0)),
            pl.BlockSpec((1, KV_W // LANES, 1, tm, LANES), lambda b, i: (b, 0, i, 0, 0)),
            pl.BlockSpec((1, N_KV_HEADS, 1, V_ROWS, tm), lambda b, i: (b, 0, i, 0, 0)),
            pl.BlockSpec((1, 1, 1, KV_W), lambda b, i: (b, i, 0, 0)),
        ]
    return pl.pallas_call(
        functools.partial(_proj_kernel, layouts=layouts),
        grid=(B, nt),
        in_specs=[
            pl.BlockSpec((1, tm, D_MODEL), lambda b, i: (b, i, 0)),
            pl.BlockSpec((D_MODEL, PROJ_W), lambda b, i: (0, 0)),
            pl.BlockSpec((tm, LANES), lambda b, i: (i % n_tab, 0)),
            pl.BlockSpec((tm, LANES), lambda b, i: (i % n_tab, 0)),
        ],
        out_specs=out_specs,
        out_shape=out_shape,
        compiler_params=_cparams(2),
        name="proj_layouts" if layouts else "proj",
    )(x, w, cos, sin)


def _rope_tables(pos):
    half = HEAD_DIM // 2
    inv = jnp.power(jnp.float32(ROPE_THETA), -jnp.arange(half, dtype=f32) / half)
    ang = pos.astype(f32)[:, None] * inv[None, :]
    cos = jnp.cos(ang)
    sin = jnp.sin(ang)
    reps = LANES // HEAD_DIM
    return jnp.tile(jnp.concatenate([cos, cos], axis=1), (1, reps)), jnp.tile(jnp.concatenate([-sin, sin], axis=1), (1, reps))


def _memkv_kernel(m_ref, w_ref, kv_ref, k2_ref, vt_ref):
    y = _dot(m_ref[0].astype(bf16), w_ref[0])
    yt = y.T
    kv_ref[0, 0] = yt
    for p in range(MEM_W // LANES):
        k2_ref[0, 0, p] = y[:, p * LANES:(p + 1) * LANES].astype(bf16)
    ones_rows = _ones_row_block(N_MEM)
    for h in range(MEM_HEADS):
        vh = yt[MEM_W + h * HEAD_DIM:MEM_W + (h + 1) * HEAD_DIM, :]
        vt_ref[0, 0, h] = jnp.concatenate([vh, ones_rows], axis=0).astype(bf16)


def _memkv(mem, w):
    B = mem.shape[0]
    return pl.pallas_call(
        _memkv_kernel,
        grid=(DEPTH, B),
        in_specs=[
            pl.BlockSpec((1, N_MEM, D_MODEL), lambda l, b: (b, 0, 0)),
            pl.BlockSpec((1, D_MODEL, 2 * MEM_W), lambda l, b: (l, 0, 0)),
        ],
        out_specs=[
            pl.BlockSpec((1, 1, 2 * MEM_W, N_MEM), lambda l, b: (l, b, 0, 0)),
            pl.BlockSpec((1, 1, MEM_W // LANES, N_MEM, LANES), lambda l, b: (l, b, 0, 0, 0)),
            pl.BlockSpec((1, 1, MEM_HEADS, V_ROWS, N_MEM), lambda l, b: (l, b, 0, 0, 0)),
        ],
        out_shape=[
            jax.ShapeDtypeStruct((DEPTH, B, 2 * MEM_W, N_MEM), f32),
            jax.ShapeDtypeStruct((DEPTH, B, MEM_W // LANES, N_MEM, LANES), bf16),
            jax.ShapeDtypeStruct((DEPTH, B, MEM_HEADS, V_ROWS, N_MEM), bf16),
        ],
        compiler_params=_cparams(2),
        name="memkv",
    )(mem, w)


def _gelu(c):
    return 0.5 * c * (1.0 + lax.erf(c * (1.0 / math.sqrt(2.0))))


def _merge_ffn_kernel(x_ref, os_ref, om_ref, wos_ref, wom_ref, g1_ref, b1_ref, wg_ref, cw_ref, cb_ref, wu_ref,
                      wd_ref, g2_ref, b2_ref, *rest, sample):
    if sample:
        inj1_ref, inj2_ref, y_ref, u_ref = rest
    else:
        y_ref, tail_ref, carry_ref = rest
    x = x_ref[0]
    attn = _dot(os_ref[0], wos_ref[...]) + _dot(om_ref[0], wom_ref[...])
    x1 = _layer_norm(ALPHA * x + attn, g1_ref[...], b1_ref[...])
    x1b = x1.astype(bf16)
    u = _dot(x1b, wg_ref[...])
    up = _dot(x1b, wu_ref[...])
    tm = u.shape[0]
    row = lax.broadcasted_iota(jnp.int32, (tm, 1), 0)
    r1 = pltpu.roll(u, 1, 0)
    r2 = pltpu.roll(u, 2, 0)
    if sample:
        s = row % 4
        prev1 = jnp.where(s >= 1, r1, inj1_ref[0])
        prev2 = jnp.where(s >= 2, r2, inj2_ref[0])
        u_ref[0] = u
    else:
        @pl.when(pl.program_id(1) == 0)
        def _():
            carry_ref[...] = jnp.zeros_like(carry_ref)
        c6 = carry_ref[6:7, :]
        c7 = carry_ref[7:8, :]
        prev1 = jnp.where(row == 0, c7, r1)
        prev2 = jnp.where(row == 0, c6, jnp.where(row == 1, c7, r2))
        carry_ref[...] = u[tm - 8:, :]
        tail_ref[0] = u[tm - 8:, :]
    cw = cw_ref[...]
    c = cb_ref[...] + prev2 * cw[0:1, :] + prev1 * cw[1:2, :] + u * cw[2:3, :]
    hid = (_gelu(c) * up).astype(bf16)
    y_ref[0] = _layer_norm(ALPHA * x1 + _dot(hid, wd_ref[...]), g2_ref[...], b2_ref[...])


def _merge_ffn(x, o_self, o_mem, lw, inj=None):
    B, T, _ = x.shape
    tm = TOKEN_TILE
    nt = T // tm
    sample = inj is not None
    tok = lambda w: pl.BlockSpec((1, tm, w), lambda b, i: (b, i, 0))
    const = lambda shape: pl.BlockSpec(shape, lambda b, i: (0,) * len(shape), pipeline_mode=pl.Buffered(1))
    in_specs = [
        tok(D_MODEL), tok(Q_W), tok(MEM_W),
        const((Q_W, D_MODEL)), const((MEM_W, D_MODEL)), const((1, D_MODEL)), const((1, D_MODEL)),
        const((D_MODEL, D_FF)), const((CONV_W, D_FF)), const((1, D_FF)), const((D_MODEL, D_FF)),
        const((D_FF, D_MODEL)), const((1, D_MODEL)), const((1, D_MODEL)),
    ]
    args = [x, o_self, o_mem, lw["wo_self"], lw["wo_mem"], lw["g1"], lw["b1"], lw["w_gate"], lw["conv_w"],
            lw["conv_b"], lw["w_up"], lw["w_down"], lw["g2"], lw["b2"]]
    if sample:
        in_specs += [tok(D_FF), tok(D_FF)]
        args += list(inj)
        out_specs = [tok(D_MODEL), tok(D_FF)]
        out_shape = [jax.ShapeDtypeStruct((B, T, D_MODEL), f32), jax.ShapeDtypeStruct((B, T, D_FF), f32)]
        scratch = []
    else:
        out_specs = [tok(D_MODEL), pl.BlockSpec((1, 8, D_FF), lambda b, i: (b, 0, 0))]
        out_shape = [jax.ShapeDtypeStruct((B, T, D_MODEL), f32), jax.ShapeDtypeStruct((B, 8, D_FF), f32)]
        scratch = [pltpu.VMEM((8, D_FF), f32)]
    return pl.pallas_call(
        functools.partial(_merge_ffn_kernel, sample=sample),
        grid=(B, nt),
        in_specs=in_specs,
        out_specs=out_specs,
        out_shape=out_shape,
        scratch_shapes=scratch,
        compiler_params=_cparams(2),
        name="merge_ffn_sample" if sample else "merge_ffn",
    )(*args)


def _store_heads_transposed(ot_ref, out_ref):
    out_ref[0] = ot_ref[...].T.astype(out_ref.dtype)


def _swa_prompt_kernel(q_ref, kc_ref, kp_ref, vc_ref, vp_ref, sink_ref, out_ref, ot_ref):
    i = pl.program_id(1)
    tm = TOKEN_TILE
    ncol = GROUP * tm
    key_c = lax.broadcasted_iota(jnp.int32, (tm, ncol), 0)
    key_p = lax.broadcasted_iota(jnp.int32, (WINDOW, ncol), 0)
    qc = lax.broadcasted_iota(jnp.int32, (1, ncol), 1) % tm
    valid_c = (key_c <= qc) & (key_c > qc - WINDOW)
    valid_p = (key_p > qc) & (i > 0)
    for h in range(N_KV_HEADS):
        p = h // 2
        qh = q_ref[0, h]
        s_c = jnp.where(valid_c, _dot_nt(kc_ref[0, p, 0], qh), NEG_INF)
        s_p = jnp.where(valid_p, _dot_nt(kp_ref[0, p, 0], qh), NEG_INF)
        sink = sink_ref[h]
        m = jnp.maximum(jnp.maximum(jnp.max(s_c, axis=0, keepdims=True), jnp.max(s_p, axis=0, keepdims=True)), sink)
        e_c = jnp.exp2(s_c - m)
        e_p = jnp.exp2(s_p - m)
        ot = _dot(vc_ref[0, h, 0], e_c.astype(bf16)) + _dot(vp_ref[0, h, 0], e_p.astype(bf16))
        den = ot[HEAD_DIM:HEAD_DIM + 1, :] + jnp.exp2(sink - m)
        ot = ot[:HEAD_DIM, :] / den
        for g in range(GROUP):
            hq = h * GROUP + g
            ot_ref[hq * HEAD_DIM:(hq + 1) * HEAD_DIM, :] = ot[:, g * tm:(g + 1) * tm]
    _store_heads_transposed(ot_ref, out_ref)


def _swa_prompt(qpad, k5, vt5, sink_cols):
    B = qpad.shape[0]
    nt = k5.shape[2]
    tm = TOKEN_TILE
    T = nt * tm
    prev = lambda i: jnp.maximum(i - 1, 0)
    return pl.pallas_call(
        _swa_prompt_kernel,
        grid=(B, nt),
        in_specs=[
            pl.BlockSpec((1, N_KV_HEADS, GROUP * tm, LANES), lambda b, i: (b, 0, i, 0)),
            pl.BlockSpec((1, KV_W // LANES, 1, tm, LANES), lambda b, i: (b, 0, i, 0, 0)),
            pl.BlockSpec((1, KV_W // LANES, 1, WINDOW, LANES), lambda b, i: (b, 0, prev(i), tm // WINDOW - 1, 0)),
            pl.BlockSpec((1, N_KV_HEADS, 1, V_ROWS, tm), lambda b, i: (b, 0, i, 0, 0)),
            pl.BlockSpec((1, N_KV_HEADS, 1, V_ROWS, WINDOW), lambda b, i: (b, 0, prev(i), 0, tm // WINDOW - 1)),
            pl.BlockSpec((N_KV_HEADS, 1, GROUP * tm), lambda b, i: (0, 0, 0)),
        ],
        out_specs=pl.BlockSpec((1, tm, Q_W), lambda b, i: (b, i, 0)),
        out_shape=jax.ShapeDtypeStruct((B, T, Q_W), bf16),
        scratch_shapes=[pltpu.VMEM((Q_W, tm), f32)],
        compiler_params=_cparams(2),
        name="swa_prompt",
    )(qpad, k5, k5, vt5, vt5, sink_cols)


def _mem_attn_prompt_kernel(q_ref, k_ref, vt_ref, out_ref, ot_ref):
    tm = q_ref.shape[2]
    for p in range(MEM_W // LANES):
        qp = q_ref[0, 2 * p:2 * p + 2].reshape(2 * tm, LANES)
        s = _dot_nt(k_ref[0, p], qp)
        m = jnp.max(s, axis=0, keepdims=True)
        e = jnp.exp2(s - m).astype(bf16)
        for hh in range(2):
            h = 2 * p + hh
            ot = _dot(vt_ref[0, h], e[:, hh * tm:(hh + 1) * tm])
            ot_ref[h * HEAD_DIM:(h + 1) * HEAD_DIM, :] = ot[:HEAD_DIM, :] / ot[HEAD_DIM:HEAD_DIM + 1, :]
    _store_heads_transposed(ot_ref, out_ref)


def _mem_attn_prompt(qmpad, mk2, mvt):
    B, _, T, _ = qmpad.shape
    tm = TOKEN_TILE
    return pl.pallas_call(
        _mem_attn_prompt_kernel,
        grid=(B, T // tm),
        in_specs=[
            pl.BlockSpec((1, MEM_HEADS, tm, LANES), lambda b, i: (b, 0, i, 0)),
            pl.BlockSpec((1, MEM_W // LANES, N_MEM, LANES), lambda b, i: (b, 0, 0, 0)),
            pl.BlockSpec((1, MEM_HEADS, V_ROWS, N_MEM), lambda b, i: (b, 0, 0, 0)),
        ],
        out_specs=pl.BlockSpec((1, tm, MEM_W), lambda b, i: (b, i, 0)),
        out_shape=jax.ShapeDtypeStruct((B, T, MEM_W), bf16),
        scratch_shapes=[pltpu.VMEM((MEM_W, tm), f32)],
        compiler_params=_cparams(2),
        name="mem_attn_prompt",
    )(qmpad, mk2, mvt)


def _top3_bias(gate, eligible, axis):
    idx = lax.broadcasted_iota(jnp.int32, gate.shape, axis)
    n = gate.shape[axis]
    rest = jnp.where(eligible, gate, NEG_INF)
    chosen = jnp.zeros(gate.shape, dtype=jnp.bool_)
    for _ in range(MOBA_TOPK):
        best = jnp.max(rest, axis=axis, keepdims=True)
        first = jnp.min(jnp.where(rest == best, idx, n), axis=axis, keepdims=True)
        pick = (idx == first) & (best > NEG_INF)
        chosen = chosen | pick
        rest = jnp.where(pick, NEG_INF, rest)
    return jnp.where(chosen, 0.0, NEG_INF)


def _moba_gate_kernel(q_ref, km_ref, bias_ref):
    i = pl.program_id(1)
    tm = TOKEN_TILE
    nb = km_ref.shape[1]
    q = q_ref[0]
    km = km_ref[0]
    q_hi = q.astype(bf16)
    q_lo = (q - q_hi.astype(f32)).astype(bf16)
    lane = lax.broadcasted_iota(jnp.int32, (1, LANES), 1)
    lo = lane < HEAD_DIM
    blk = lax.broadcasted_iota(jnp.int32, (nb, 1), 0)
    for h in range(N_KV_HEADS):
        kt = km[:, (h // 2) * LANES:(h // 2 + 1) * LANES]
        gates = []
        for g in range(GROUP):
            hq = h * GROUP + g
            kk = kt if hq % 2 == h % 2 else pltpu.roll(kt, HEAD_DIM, 1)
            kk = jnp.where(lo if hq % 2 == 0 else jnp.logical_not(lo), kk, 0.0)
            kk_hi = kk.astype(bf16)
            kk_lo = (kk - kk_hi.astype(f32)).astype(bf16)
            cols = slice((hq // 2) * LANES, (hq // 2 + 1) * LANES)
            gates.append(_dot_nt(kk_hi, q_hi[:, cols]) + _dot_nt(kk_hi, q_lo[:, cols])
                         + _dot_nt(kk_lo, q_hi[:, cols]))
        gate = jnp.concatenate(gates, axis=1)
        bias_ref[0, 0, h] = _top3_bias(gate, blk < i, 0)


def _moba_gate(q, kmean):
    B, T, _ = q.shape
    tm = TOKEN_TILE
    nb = T // tm
    return pl.pallas_call(
        _moba_gate_kernel,
        grid=(B, nb),
        in_specs=[
            pl.BlockSpec((1, tm, Q_W), lambda b, i: (b, i, 0)),
            pl.BlockSpec((1, nb, KV_W), lambda b, i: (b, 0, 0)),
        ],
        out_specs=pl.BlockSpec((1, 1, N_KV_HEADS, nb, GROUP * tm), lambda b, i: (b, i, 0, 0, 0)),
        out_shape=jax.ShapeDtypeStruct((B, nb, N_KV_HEADS, nb, GROUP * tm), f32),
        compiler_params=_cparams(2),
        name="moba_gate",
    )(q, kmean)


def _moba_prompt_kernel(q_ref, k_ref, vt_ref, bias_ref, out_ref, acc_ref, m_ref, sa_ref, sb_ref):
    i = pl.program_id(1)
    tm = TOKEN_TILE
    nb = k_ref.shape[2]
    n_pairs = KV_W // LANES
    per_pair = N_HEADS // n_pairs
    key = lax.broadcasted_iota(jnp.int32, (tm, tm), 0)
    qc = lax.broadcasted_iota(jnp.int32, (1, tm), 1)
    causal = key <= qc

    def scores_into(j, s_ref):
        for p in range(n_pairs):
            qp = q_ref[0, 2 * p:2 * p + 2].reshape(per_pair * tm, LANES)
            s_ref[p] = _dot_nt(k_ref[0, p, j], qp)

    def chunk(s_ref, c):
        return s_ref[c // per_pair, :, (c % per_pair) * tm:(c % per_pair + 1) * tm]

    scores_into(i, sb_ref)
    scores_into(0, sa_ref)
    for c in range(N_HEADS):
        s = jnp.where(causal, chunk(sb_ref, c), NEG_INF)
        m = jnp.max(s, axis=0, keepdims=True)
        e = jnp.exp2(s - m)
        m_ref[c] = m
        acc_ref[c] = _dot(vt_ref[0, c // GROUP, i], e.astype(bf16))

    def consume(j, s_ref):
        for c in range(N_HEADS):
            h, g = divmod(c, GROUP)
            s = chunk(s_ref, c)
            bias = bias_ref[0, 0, h, pl.ds(j, 1), g * tm:(g + 1) * tm]
            m = m_ref[c]
            m_new = jnp.maximum(m, jnp.max(s, axis=0, keepdims=True) + bias)
            a = jnp.exp2(m - m_new)
            e = jnp.exp2(s - (m_new - bias))
            m_ref[c] = m_new
            acc_ref[c] = a * acc_ref[c] + _dot(vt_ref[0, h, j], e.astype(bf16))

    def body(jj, carry):
        j = 2 * jj
        scores_into(j + 1, sb_ref)
        consume(j, sa_ref)
        scores_into(jnp.minimum(j + 2, nb - 1), sa_ref)
        consume(j + 1, sb_ref)
        return carry

    lax.fori_loop(0, (i + 1) // 2, body, 0)
    heads = [acc_ref[c, :HEAD_DIM, :] / acc_ref[c, HEAD_DIM:HEAD_DIM + 1, :] for c in range(N_HEADS)]
    out_ref[0] = jnp.concatenate(heads, axis=0).T.astype(out_ref.dtype)


def _moba_prompt(qpad, k5, vt5, bias):
    B = qpad.shape[0]
    nb = k5.shape[2]
    tm = TOKEN_TILE
    T = nb * tm
    return pl.pallas_call(
        _moba_prompt_kernel,
        grid=(B, nb),
        in_specs=[
            pl.BlockSpec((1, N_KV_HEADS, GROUP * tm, LANES), lambda b, i: (b, 0, i, 0)),
            pl.BlockSpec((1, KV_W // LANES, nb, tm, LANES), lambda b, i: (b, 0, 0, 0, 0)),
            pl.BlockSpec((1, N_KV_HEADS, nb, V_ROWS, tm), lambda b, i: (b, 0, 0, 0, 0)),
            pl.BlockSpec((1, 1, N_KV_HEADS, nb, GROUP * tm), lambda b, i: (b, i, 0, 0, 0)),
        ],
        out_specs=pl.BlockSpec((1, tm, Q_W), lambda b, i: (b, i, 0)),
        out_shape=jax.ShapeDtypeStruct((B, T, Q_W), bf16),
        scratch_shapes=[pltpu.VMEM((N_HEADS, V_ROWS, tm), f32), pltpu.VMEM((N_HEADS, 1, tm), f32),
                        pltpu.VMEM((KV_W // LANES, tm, N_HEADS * tm * LANES // KV_W), f32),
                        pltpu.VMEM((KV_W // LANES, tm, N_HEADS * tm * LANES // KV_W), f32)],
        compiler_params=_cparams(2),
        name="moba_prompt",
    )(qpad, k5, vt5, bias)


SAMPLE_ROWS = N_HEADS * 4
NEW_PAD = 8
SEQ_GROUP = 8


def _pad_rows(a, n):
    return jnp.concatenate([a, jnp.zeros((n - a.shape[0], a.shape[1]), a.dtype)], axis=0)


def _swa_sample_kernel(q_ref, kc_ref, vc_ref, kn_ref, vn_ref, sink_ref, o_ref, nk_ref, nv_ref):
    wb = kc_ref.shape[2]
    col = lax.broadcasted_iota(jnp.int32, (SAMPLE_ROWS, wb), 1)
    s_idx = lax.broadcasted_iota(jnp.int32, (SAMPLE_ROWS, 1), 0) % 4
    valid_c = col > s_idx
    valid_n = col <= s_idx
    lane = lax.broadcasted_iota(jnp.int32, (1, wb), 1)
    sink = sink_ref[...]
    for n in range(SEQ_GROUP):
        q = q_ref[n]
        kc = kc_ref[n]
        vc = vc_ref[n]
        kn = _pad_rows(kn_ref[n], wb)
        vn = _pad_rows(vn_ref[n], wb)
        s_c = jnp.where(valid_c, _dot(q, kc.astype(bf16)), NEG_INF)
        s_n = jnp.where(valid_n, _dot_nt(q, kn.astype(bf16)), NEG_INF)
        m = jnp.maximum(jnp.maximum(jnp.max(s_c, axis=1, keepdims=True), jnp.max(s_n, axis=1, keepdims=True)), sink)
        e_c = jnp.exp(s_c - m)
        e_n = jnp.exp(s_n - m)
        den = jnp.sum(e_c, axis=1, keepdims=True) + jnp.sum(e_n, axis=1, keepdims=True) + jnp.exp(sink - m)
        o = _dot_nt(e_c.astype(bf16), vc.astype(bf16)) + _dot(e_n.astype(bf16), vn.astype(bf16))
        o_ref[n] = o / den
        nk_ref[n] = jnp.where(lane < wb - 4, pltpu.roll(kc, wb - 4, 1), pltpu.roll(kn.T, wb - 4, 1))
        nv_ref[n] = jnp.where(lane < wb - 4, pltpu.roll(vc, wb - 4, 1), pltpu.roll(vn.T, wb - 4, 1))


def _swa_sample(qbd, kc, vc, kn, vn, sink_rows):
    n_seq, _, wb = kc.shape
    sg = SEQ_GROUP
    seq = lambda r, w: pl.BlockSpec((sg, r, w), lambda i: (i, 0, 0))
    return pl.pallas_call(
        _swa_sample_kernel,
        grid=(n_seq // sg,),
        in_specs=[seq(SAMPLE_ROWS, KV_W), seq(KV_W, wb), seq(KV_W, wb), seq(NEW_PAD, KV_W), seq(NEW_PAD, KV_W),
                  pl.BlockSpec((SAMPLE_ROWS, 1), lambda i: (0, 0))],
        out_specs=[seq(SAMPLE_ROWS, KV_W), seq(KV_W, wb), seq(KV_W, wb)],
        out_shape=[jax.ShapeDtypeStruct((n_seq, SAMPLE_ROWS, KV_W), f32),
                   jax.ShapeDtypeStruct((n_seq, KV_W, wb), f32),
                   jax.ShapeDtypeStruct((n_seq, KV_W, wb), f32)],
        compiler_params=_cparams(1),
        name="swa_sample",
    )(qbd, kc, vc, kn, vn, sink_rows)


def _mem_sample_kernel(q_ref, k_ref, v_ref, o_ref):
    for n in range(SEQ_GROUP):
        s = _dot(q_ref[n], k_ref[0, n].astype(bf16))
        m = jnp.max(s, axis=1, keepdims=True)
        e = jnp.exp(s - m)
        den = jnp.sum(e, axis=1, keepdims=True)
        o_ref[n] = _dot_nt(e.astype(bf16), v_ref[0, n].astype(bf16)) / den


def _mem_sample(qbd, km, vm, layer):
    n_seq = km.shape[1]
    sg = SEQ_GROUP
    rows = qbd.shape[1]
    cache = pl.BlockSpec((1, sg, MEM_W, N_MEM), lambda i: (layer, i, 0, 0))
    return pl.pallas_call(
        _mem_sample_kernel,
        grid=(n_seq // sg,),
        in_specs=[pl.BlockSpec((sg, rows, MEM_W), lambda i: (i, 0, 0)), cache, cache],
        out_specs=pl.BlockSpec((sg, rows, MEM_W), lambda i: (i, 0, 0)),
        out_shape=jax.ShapeDtypeStruct((n_seq, rows, MEM_W), f32),
        compiler_params=_cparams(1),
        name="mem_sample",
    )(qbd, km, vm)


def _moba_sample_kernel(pt_ref, q_ref, qf_ref, kn_ref, vn_ref, expand_ref, kpool_ref, vpool_ref, o_ref,
                        kbuf, vbuf, sem):
    b = pl.program_id(0)
    n_seq = pl.num_programs(0)
    n_pages = pt_ref.shape[1]
    page = kpool_ref.shape[2]
    past = n_pages * page
    nb = past // MOBA_BLOCK
    slot = b % 2

    def page_copy(which, seq, slot_, pg):
        pool, buf = ((kpool_ref, kbuf), (vpool_ref, vbuf))[which]
        return pltpu.make_async_copy(pool.at[pt_ref[seq, pg]], buf.at[slot_, :, pg * page:(pg + 1) * page],
                                     sem.at[which, slot_])

    def fetch(seq, slot_):
        for pg in range(n_pages):
            page_copy(0, seq, slot_, pg).start()
        for pg in range(n_pages):
            page_copy(1, seq, slot_, pg).start()

    def wait(which, seq, slot_):
        for pg in range(n_pages):
            page_copy(which, seq, slot_, pg).wait()

    @pl.when(b == 0)
    def _():
        fetch(0, 0)

    @pl.when(b + 1 < n_seq)
    def _():
        fetch(b + 1, 1 - slot)

    wait(0, b, slot)
    lane = lax.broadcasted_iota(jnp.int32, (1, LANES), 1)
    kmean = jnp.zeros((KV_W, LANES), f32)
    for j in range(nb):
        part = kbuf[slot, :, j * MOBA_BLOCK:j * MOBA_BLOCK + LANES]
        for c in range(1, MOBA_BLOCK // LANES):
            part = part + kbuf[slot, :, j * MOBA_BLOCK + c * LANES:j * MOBA_BLOCK + (c + 1) * LANES]
        kmean = jnp.where(lane == j, jnp.sum(part, axis=1, keepdims=True), kmean)
    kmean = kmean * (1.0 / MOBA_BLOCK)
    gate = jnp.dot(qf_ref[0], kmean, preferred_element_type=f32, precision=lax.Precision.HIGHEST)
    bias = _top3_bias(gate, lane < nb, 1)
    chosen = jnp.where(bias == 0.0, 1.0, 0.0).astype(bf16)
    keep = _dot(chosen, expand_ref[...])
    q = q_ref[0]
    s = jnp.where(keep > 0.5, _dot(q, kbuf[slot].astype(bf16)), NEG_INF)
    col = lax.broadcasted_iota(jnp.int32, (SAMPLE_ROWS, LANES), 1)
    s_idx = lax.broadcasted_iota(jnp.int32, (SAMPLE_ROWS, 1), 0) % 4
    kn = _pad_rows(kn_ref[0], LANES).astype(bf16)
    vn = _pad_rows(vn_ref[0], LANES).astype(bf16)
    s_n = jnp.where(col <= s_idx, _dot_nt(q, kn), NEG_INF)
    m = jnp.maximum(jnp.max(s, axis=1, keepdims=True), jnp.max(s_n, axis=1, keepdims=True))
    e = jnp.exp(s - m)
    e_n = jnp.exp(s_n - m)
    den = jnp.sum(e, axis=1, keepdims=True) + jnp.sum(e_n, axis=1, keepdims=True)
    wait(1, b, slot)
    o = _dot_nt(e.astype(bf16), vbuf[slot].astype(bf16)) + _dot(e_n.astype(bf16), vn)
    o_ref[0] = o / den


def _moba_sample(page_table, qbd, qbd_f32, kn, vn, kpool, vpool):
    n_seq, n_pages = page_table.shape
    page = kpool.shape[2]
    past = n_pages * page
    nb = past // MOBA_BLOCK
    expand = (jnp.arange(past, dtype=jnp.int32)[None, :] // MOBA_BLOCK == jnp.arange(LANES, dtype=jnp.int32)[:, None]).astype(bf16)
    seq = lambda r: pl.BlockSpec((1, r, KV_W), lambda i, pt: (i, 0, 0))
    grid_spec = pltpu.PrefetchScalarGridSpec(
        num_scalar_prefetch=1,
        grid=(n_seq,),
        in_specs=[seq(SAMPLE_ROWS), seq(SAMPLE_ROWS), seq(NEW_PAD), seq(NEW_PAD),
                  pl.BlockSpec((LANES, past), lambda i, pt: (0, 0)),
                  pl.BlockSpec(memory_space=pl.ANY), pl.BlockSpec(memory_space=pl.ANY)],
        out_specs=seq(SAMPLE_ROWS),
        scratch_shapes=[pltpu.VMEM((2, KV_W, past), f32), pltpu.VMEM((2, KV_W, past), f32),
                        pltpu.SemaphoreType.DMA((2, 2))],
    )
    return pl.pallas_call(
        _moba_sample_kernel,
        grid_spec=grid_spec,
        out_shape=jax.ShapeDtypeStruct((n_seq, SAMPLE_ROWS, KV_W), f32),
        compiler_params=_cparams(1),
        name="moba_sample",
    )(page_table, qbd, qbd_f32, kn, vn, expand, kpool, vpool)


def _block_diag_rows(a, n_heads, per_kv):
    n_seq, S, _ = a.shape
    n_kv = n_heads // per_kv
    a = a.reshape(n_seq, S, n_kv, per_kv, HEAD_DIM).transpose(0, 2, 3, 1, 4)
    eye = jnp.eye(n_kv, dtype=a.dtype)
    bd = a[:, :, :, :, None, :] * eye[None, :, None, None, :, None]
    return bd.reshape(n_seq, n_heads * S, n_kv * HEAD_DIM)


def _diag_rows_out(o, n_heads, per_kv, S):
    n_seq = o.shape[0]
    n_kv = n_heads // per_kv
    o = o.reshape(n_seq, n_kv, per_kv, S, n_kv, HEAD_DIM)
    o = jnp.stack([o[:, h, :, :, h, :] for h in range(n_kv)], axis=1)
    return o.transpose(0, 3, 1, 2, 4).reshape(n_seq, S, n_heads * HEAD_DIM)


def _layer_weights(layer, w_o, ln1_g, ln1_b, w_ffn_gate, conv_w, conv_b, w_ffn_up, w_ffn_down, ln2_g, ln2_b):
    row = lambda a: a[layer][None, :]
    return dict(
        wo_self=w_o[layer, :Q_W].astype(bf16), wo_mem=w_o[layer, Q_W:].astype(bf16),
        g1=row(ln1_g), b1=row(ln1_b),
        w_gate=w_ffn_gate[layer].astype(bf16), conv_w=conv_w[layer], conv_b=row(conv_b),
        w_up=w_ffn_up[layer].astype(bf16), w_down=w_ffn_down[layer].astype(bf16),
        g2=row(ln2_g), b2=row(ln2_b),
    )


def _conv_injection(state):
    n_seq = state.shape[0]
    z = jnp.zeros((n_seq, 1, D_FF), state.dtype)
    inj1 = jnp.concatenate([state[:, 1:2], z, z, z], axis=1)
    inj2 = jnp.concatenate([state[:, 0:1], state[:, 1:2], z, z], axis=1)
    return inj1.reshape(1, n_seq * 4, D_FF), inj2.reshape(1, n_seq * 4, D_FF)


def kernel(x_prompt, x_sample, cache_win_k, cache_win_v, cache_moba_k, cache_moba_v, page_table, cache_mem_k, cache_mem_v, state_conv, mem_prompt, w_in_a, sink_a, w_in_b, w_kv_shared, w_mem_kv, w_o, ln1_g, ln1_b, w_ffn_gate, conv_w, conv_b, w_ffn_up, w_ffn_down, ln2_g, ln2_b):
    B, T, _ = x_prompt.shape
    Bd, S, _ = x_sample.shape
    n_pool, page = cache_moba_k.shape[:2]
    past = page_table.shape[1] * page
    wb = cache_win_k.shape[2]
    nb = T // TOKEN_TILE

    cos_p, sin_p = _rope_tables(jnp.arange(T, dtype=jnp.int32))
    cos_s, sin_s = _rope_tables(past + (jnp.arange(Bd * S, dtype=jnp.int32) % S))
    w_a = w_in_a[0].astype(bf16)
    w_b = jnp.concatenate([w_in_b[0][:, :Q_W], w_kv_shared, w_in_b[0][:, Q_W:]], axis=1).astype(bf16)
    lws = [_layer_weights(l, w_o, ln1_g, ln1_b, w_ffn_gate, conv_w, conv_b, w_ffn_up, w_ffn_down, ln2_g, ln2_b)
           for l in range(DEPTH)]
    xs = x_sample.reshape(1, Bd * S, D_MODEL)

    def feature_major(a):
        a = jnp.moveaxis(a, -3, -1)
        return a.reshape(a.shape[:-3] + (a.shape[-3] * HEAD_DIM, a.shape[-1]))

    def token_major(a, n_heads):
        a = a.reshape(a.shape[:-2] + (n_heads, HEAD_DIM, a.shape[-1]))
        return jnp.moveaxis(a, -1, -3)

    mem_kvt, mem_k2, mem_vt = _memkv(mem_prompt, w_mem_kv.astype(bf16))
    mem_k_out = token_major(mem_kvt[:, :, :MEM_W], MEM_HEADS)
    mem_v_out = token_major(mem_kvt[:, :, MEM_W:], MEM_HEADS)
    cache_mem_kt = feature_major(cache_mem_k)
    cache_mem_vt = feature_major(cache_mem_v)

    def sample_rows(a, n_heads, per_kv):
        return _block_diag_rows(a.reshape(Bd, S, n_heads * HEAD_DIM), n_heads, per_kv)

    def pad_new(a):
        return jnp.pad(a.reshape(Bd, S, KV_W), ((0, 0), (0, NEW_PAD - S), (0, 0)))

    def sample_mem(qm, layer):
        qbd = (sample_rows(qm, MEM_HEADS, 1) * SCALE).astype(bf16)
        o = _mem_sample(qbd, cache_mem_kt, cache_mem_vt, layer)
        return _diag_rows_out(o, MEM_HEADS, 1, S).reshape(1, Bd * S, MEM_W).astype(bf16)

    _, kat, vat, qpad, qmpad, k5, vt5, _ = _proj(x_prompt, w_a, cos_p, sin_p, True)
    sink_cols = jnp.repeat(sink_a[0].reshape(N_KV_HEADS, GROUP) * LOG2E, TOKEN_TILE, axis=1)[:, None, :]
    o_self = _swa_prompt(qpad, k5, vt5, sink_cols)
    o_mem = _mem_attn_prompt(qmpad, mem_k2[0], mem_vt[0])
    xp, tail0 = _merge_ffn(x_prompt, o_self, o_mem, lws[0])
    win_kp = token_major(kat[:, :, T - wb:], N_KV_HEADS)[None]
    win_vp = token_major(vat[:, :, T - wb:], N_KV_HEADS)[None]

    qs, ks, vs, qms = _proj(xs, w_a, cos_s, sin_s, False)
    qbd = (sample_rows(qs, N_HEADS, GROUP) * SCALE).astype(bf16)
    sink_rows = jnp.repeat(sink_a[0], S)[:, None]
    o_s, win_kst, win_vst = _swa_sample(qbd, feature_major(cache_win_k[0]), feature_major(cache_win_v[0]),
                                        pad_new(ks), pad_new(vs), sink_rows)
    os_self = _diag_rows_out(o_s, N_HEADS, GROUP, S).reshape(1, Bd * S, Q_W).astype(bf16)
    xs, u0 = _merge_ffn(xs, os_self, sample_mem(qms, 0), lws[0], _conv_injection(state_conv[0]))

    qb, kbt, vbt, qpad, qmpad, k5, vt5, kmean = _proj(xp, w_b, cos_p, sin_p, True)
    bias = _moba_gate(qb, kmean.reshape(B, nb, KV_W))
    o_self = _moba_prompt(qpad, k5, vt5, bias)
    o_mem = _mem_attn_prompt(qmpad, mem_k2[1], mem_vt[1])
    yp, tail1 = _merge_ffn(xp, o_self, o_mem, lws[1])

    qs, ks, vs, qms = _proj(xs, w_b, cos_s, sin_s, False)
    qbd_f32 = sample_rows(qs, N_HEADS, GROUP)
    o_s = _moba_sample(page_table, (qbd_f32 * SCALE).astype(bf16), qbd_f32, pad_new(ks), pad_new(vs),
                       feature_major(cache_moba_k), feature_major(cache_moba_v))
    os_self = _diag_rows_out(o_s, N_HEADS, GROUP, S).reshape(1, Bd * S, Q_W).astype(bf16)
    ys, u1 = _merge_ffn(xs, os_self, sample_mem(qms, 1), lws[1], _conv_injection(state_conv[1]))

    kv_heads = lambda a, lead: a.reshape(lead + (N_KV_HEADS, HEAD_DIM))
    conv_p = jnp.stack([tail0[:, 8 - (CONV_W - 1):], tail1[:, 8 - (CONV_W - 1):]])
    conv_s = jnp.stack([u.reshape(Bd, S, D_FF)[:, S - (CONV_W - 1):] for u in (u0, u1)])
    return (
        yp, ys.reshape(Bd, S, D_MODEL),
        win_kp, win_vp,
        token_major(kbt, N_KV_HEADS), token_major(vbt, N_KV_HEADS),
        mem_k_out, mem_v_out,
        conv_p,
        token_major(win_kst, N_KV_HEADS)[None], token_major(win_vst, N_KV_HEADS)[None],
        kv_heads(ks, (Bd, S)), kv_heads(vs, (Bd, S)),
        conv_s,
    )
```

```python
import functools
import math

import jax
import jax.numpy as jnp
from jax import lax
from jax.experimental import pallas as pl
from jax.experimental.pallas import tpu as pltpu

D_MODEL = 1024
HEAD_DIM = 64
N_HEADS = 12
N_KV_HEADS = 4
GROUP = N_HEADS // N_KV_HEADS
MEM_HEADS = 4
N_MEM = 256
Q_W = N_HEADS * HEAD_DIM
KV_W = N_KV_HEADS * HEAD_DIM
MEM_W = MEM_HEADS * HEAD_DIM
PROJ_W = Q_W + 2 * KV_W + MEM_W
WINDOW = 128
MOBA_BLOCK = 256
MOBA_TOPK = 3
D_FF = 2816
CONV_W = 3
ROPE_THETA = 10000.0
LN_EPS = 1e-5
DEPTH = 2
ALPHA = (2 * DEPTH) ** 0.25
SCALE = HEAD_DIM ** -0.5
LOG2E = math.log2(math.e)
SCALE_LOG2 = SCALE * LOG2E

LANES = 128
V_ROWS = HEAD_DIM + 16
TOKEN_TILE = 256
VMEM_LIMIT = 56 * 1024 * 1024
NEG_INF = float("-inf")

f32 = jnp.float32
bf16 = jnp.bfloat16


def _cparams(n_grid, vmem=VMEM_LIMIT, flags=None):
    return pltpu.CompilerParams(dimension_semantics=("arbitrary",) * n_grid, vmem_limit_bytes=vmem, flags=flags)


def _dot_nt(a, b, precision=None):
    return lax.dot_general(a, b, (((1,), (1,)), ((), ())), preferred_element_type=f32, precision=precision)


def _dot(a, b):
    return jnp.dot(a, b, preferred_element_type=f32)


def _layer_norm(r, g, b):
    mu = jnp.mean(r, axis=-1, keepdims=True)
    d = r - mu
    var = jnp.mean(d * d, axis=-1, keepdims=True)
    return d * lax.rsqrt(var + LN_EPS) * g + b


def _ones_row_block(width):
    row = lax.broadcasted_iota(jnp.int32, (V_ROWS - HEAD_DIM, width), 0)
    return jnp.where(row == 0, 1.0, 0.0).astype(f32)


def _half_mask(width):
    lane = lax.broadcasted_iota(jnp.int32, (1, width), 1)
    return (lane % HEAD_DIM) < (HEAD_DIM // 2)


def _proj_kernel(x_ref, w_ref, cos_ref, sin_ref, q_ref, k_ref, v_ref, *rest, layouts):
    xb = x_ref[0].astype(bf16)
    y = _dot(xb, w_ref[...])
    qk = y[:, :Q_W + KV_W]
    width = Q_W + KV_W
    reps = width // LANES
    cos = jnp.tile(cos_ref[...], (1, reps))
    sin = jnp.tile(sin_ref[...], (1, reps))
    half = HEAD_DIM // 2
    partner = jnp.where(_half_mask(width), pltpu.roll(qk, width - half, 1), pltpu.roll(qk, half, 1))
    rot = qk * cos + partner * sin
    q = rot[:, :Q_W]
    k = rot[:, Q_W:]
    v = y[:, Q_W + KV_W:Q_W + 2 * KV_W]
    qm = y[:, Q_W + 2 * KV_W:]
    q_ref[0] = q
    if not layouts:
        k_ref[0] = k
        v_ref[0] = v
        rest[0][0] = qm
        return
    kt = k.T
    vt = v.T
    k_ref[0] = kt
    v_ref[0] = vt
    qpad_ref, qmpad_ref, k5_ref, vt5_ref, kmean_ref = rest
    tm = q.shape[0]
    lane = lax.broadcasted_iota(jnp.int32, (1, LANES), 1)
    lo = lane < HEAD_DIM

    def placed(tile, src_half, dst_half):
        t = tile if src_half == dst_half else pltpu.roll(tile, HEAD_DIM, 1)
        keep = lo if dst_half == 0 else jnp.logical_not(lo)
        return jnp.where(keep, t, 0.0)

    for h in range(N_KV_HEADS):
        for g in range(GROUP):
            hq = h * GROUP + g
            tile = q[:, (hq // 2) * LANES:(hq // 2 + 1) * LANES]
            qpad_ref[0, h, g * tm:(g + 1) * tm, :] = (placed(tile, hq % 2, h % 2) * SCALE_LOG2).astype(bf16)
    for h in range(MEM_HEADS):
        tile = qm[:, (h // 2) * LANES:(h // 2 + 1) * LANES]
        qmpad_ref[0, h] = (placed(tile, h % 2, h % 2) * SCALE_LOG2).astype(bf16)
    for p in range(KV_W // LANES):
        k5_ref[0, p, 0] = k[:, p * LANES:(p + 1) * LANES].astype(bf16)
    ones_rows = _ones_row_block(tm)
    for h in range(N_KV_HEADS):
        vt5_ref[0, h, 0] = jnp.concatenate([vt[h * HEAD_DIM:(h + 1) * HEAD_DIM, :], ones_rows], axis=0).astype(bf16)
    kmean_ref[0, 0] = jnp.mean(k, axis=0, keepdims=True)


def _proj(x, w, cos, sin, layouts):
    B, T, _ = x.shape
    tm = TOKEN_TILE
    nt = T // tm
    n_tab = cos.shape[0] // tm
    tok = lambda w_: (jax.ShapeDtypeStruct((B, T, w_), f32), pl.BlockSpec((1, tm, w_), lambda b, i: (b, i, 0)))
    feat = lambda w_: (jax.ShapeDtypeStruct((B, w_, T), f32), pl.BlockSpec((1, w_, tm), lambda b, i: (b, 0, i)))
    outs = [tok(Q_W), feat(KV_W), feat(KV_W)] if layouts else [tok(Q_W), tok(KV_W), tok(KV_W), tok(MEM_W)]
    out_shape = [o[0] for o in outs]
    out_specs = [o[1] for o in outs]
    if layouts:
        out_shape += [
            jax.ShapeDtypeStruct((B, N_KV_HEADS, GROUP * T, LANES), bf16),
            jax.ShapeDtypeStruct((B, MEM_HEADS, T, LANES), bf16),
            jax.ShapeDtypeStruct((B, KV_W // LANES, nt, tm, LANES), bf16),
            jax.ShapeDtypeStruct((B, N_KV_HEADS, nt, V_ROWS, tm), bf16),
            jax.ShapeDtypeStruct((B, nt, 1, KV_W), f32),
        ]
        out_specs += [
            pl.BlockSpec((1, N_KV_HEADS, GROUP * tm, LANES), lambda b, i: (b, 0, i, 0)),
            pl.BlockSpec((1, MEM_HEADS, tm, LANES), lambda b, i: (b, 0, i, 0)),
            pl.BlockSpec((1, KV_W // LANES, 1, tm, LANES), lambda b, i: (b, 0, i, 0, 0)),
            pl.BlockSpec((1, N_KV_HEADS, 1, V_ROWS, tm), lambda b, i: (b, 0, i, 0, 0)),
            pl.BlockSpec((1, 1, 1, KV_W), lambda b, i: (b, i, 0, 0)),
        ]
    return pl.pallas_call(
        functools.partial(_proj_kernel, layouts=layouts),
        grid=(B, nt),
        in_specs=[
            pl.BlockSpec((1, tm, D_MODEL), lambda b, i: (b, i, 0)),
            pl.BlockSpec((D_MODEL, PROJ_W), lambda b, i: (0, 0)),
            pl.BlockSpec((tm, LANES), lambda b, i: (i % n_tab, 0)),
            pl.BlockSpec((tm, LANES), lambda b, i: (i % n_tab, 0)),
        ],
        out_specs=out_specs,
        out_shape=out_shape,
        compiler_params=_cparams(2),
        name="proj_layouts" if layouts else "proj",
    )(x, w, cos, sin)


def _rope_tables(pos):
    half = HEAD_DIM // 2
    inv = jnp.power(jnp.float32(ROPE_THETA), -jnp.arange(half, dtype=f32) / half)
    ang = pos.astype(f32)[:, None] * inv[None, :]
    cos = jnp.cos(ang)
    sin = jnp.sin(ang)
    reps = LANES // HEAD_DIM
    return jnp.tile(jnp.concatenate([cos, cos], axis=1), (1, reps)), jnp.tile(jnp.concatenate([-sin, sin], axis=1), (1, reps))


def _memkv_kernel(m_ref, w_ref, kv_ref, k2_ref, vt_ref):
    y = _dot(m_ref[0].astype(bf16), w_ref[0])
    yt = y.T
    kv_ref[0, 0] = yt
    for p in range(MEM_W // LANES):
        k2_ref[0, 0, p] = y[:, p * LANES:(p + 1) * LANES].astype(bf16)
    ones_rows = _ones_row_block(N_MEM)
    for h in range(MEM_HEADS):
        vh = yt[MEM_W + h * HEAD_DIM:MEM_W + (h + 1) * HEAD_DIM, :]
        vt_ref[0, 0, h] = jnp.concatenate([vh, ones_rows], axis=0).astype(bf16)


def _memkv(mem, w):
    B = mem.shape[0]
    return pl.pallas_call(
        _memkv_kernel,
        grid=(DEPTH, B),
        in_specs=[
            pl.BlockSpec((1, N_MEM, D_MODEL), lambda l, b: (b, 0, 0)),
            pl.BlockSpec((1, D_MODEL, 2 * MEM_W), lambda l, b: (l, 0, 0)),
        ],
        out_specs=[
            pl.BlockSpec((1, 1, 2 * MEM_W, N_MEM), lambda l, b: (l, b, 0, 0)),
            pl.BlockSpec((1, 1, MEM_W // LANES, N_MEM, LANES), lambda l, b: (l, b, 0, 0, 0)),
            pl.BlockSpec((1, 1, MEM_HEADS, V_ROWS, N_MEM), lambda l, b: (l, b, 0, 0, 0)),
        ],
        out_shape=[
            jax.ShapeDtypeStruct((DEPTH, B, 2 * MEM_W, N_MEM), f32),
            jax.ShapeDtypeStruct((DEPTH, B, MEM_W // LANES, N_MEM, LANES), bf16),
            jax.ShapeDtypeStruct((DEPTH, B, MEM_HEADS, V_ROWS, N_MEM), bf16),
        ],
        compiler_params=_cparams(2),
        name="memkv",
    )(mem, w)


def _gelu(c):
    return 0.5 * c * (1.0 + lax.erf(c * (1.0 / math.sqrt(2.0))))


def _merge_ffn_kernel(x_ref, os_ref, om_ref, wos_ref, wom_ref, g1_ref, b1_ref, wg_ref, cw_ref, cb_ref, wu_ref,
                      wd_ref, g2_ref, b2_ref, *rest, sample):
    if sample:
        inj1_ref, inj2_ref, y_ref, u_ref, x1_s, u_s, up_s = rest
    else:
        y_ref, tail_ref, carry_ref, x1_s, u_s, up_s = rest
        @pl.when(pl.program_id(1) == 0)
        def _():
            carry_ref[...] = jnp.zeros_like(carry_ref)
    tm = TOKEN_TILE
    n_sub = x_ref.shape[1] // tm
    for t in range(n_sub):
        rows = slice(t * tm, (t + 1) * tm)
        attn = _dot(os_ref[0, rows, :], wos_ref[0]) + _dot(om_ref[0, rows, :], wom_ref[0])
        x1 = _layer_norm(ALPHA * x_ref[0, rows, :] + attn, g1_ref[0], b1_ref[0])
        x1_s[t] = x1
        x1b = x1.astype(bf16)
        u_s[t] = _dot(x1b, wg_ref[0])
        up_s[t] = _dot(x1b, wu_ref[0])
    row = lax.broadcasted_iota(jnp.int32, (tm, 1), 0)
    cw = cw_ref[0]
    for t in range(n_sub):
        rows = slice(t * tm, (t + 1) * tm)
        u = u_s[t]
        r1 = pltpu.roll(u, 1, 0)
        r2 = pltpu.roll(u, 2, 0)
        if sample:
            s = row % 4
            prev1 = jnp.where(s >= 1, r1, inj1_ref[0, rows, :])
            prev2 = jnp.where(s >= 2, r2, inj2_ref[0, rows, :])
            u_ref[0, rows, :] = u
        else:
            c6 = carry_ref[6:7, :]
            c7 = carry_ref[7:8, :]
            prev1 = jnp.where(row == 0, c7, r1)
            prev2 = jnp.where(row == 0, c6, jnp.where(row == 1, c7, r2))
            carry_ref[...] = u[tm - 8:, :]
            if t == n_sub - 1:
                tail_ref[0] = u[tm - 8:, :]
        c = cb_ref[0] + prev2 * cw[0:1, :] + prev1 * cw[1:2, :] + u * cw[2:3, :]
        hid = (_gelu(c) * up_s[t]).astype(bf16)
        y_ref[0, rows, :] = _layer_norm(ALPHA * x1_s[t] + _dot(hid, wd_ref[0]), g2_ref[0], b2_ref[0])


FFN_SUBTILES = 2


def _merge_ffn(x, o_self, o_mem, lw, layer, inj=None):
    B, T, _ = x.shape
    tm = TOKEN_TILE * FFN_SUBTILES
    nt = T // tm
    sample = inj is not None
    tok = lambda w: pl.BlockSpec((1, tm, w), lambda b, i: (b, i, 0))
    const = lambda r, c, rb=0: pl.BlockSpec((1, r, c), lambda b, i: (layer, rb, 0), pipeline_mode=pl.Buffered(1))
    in_specs = [
        tok(D_MODEL), tok(Q_W), tok(MEM_W),
        const(Q_W, D_MODEL), const(MEM_W, D_MODEL, Q_W // MEM_W), const(1, D_MODEL), const(1, D_MODEL),
        const(D_MODEL, D_FF), const(CONV_W, D_FF), const(1, D_FF), const(D_MODEL, D_FF),
        const(D_FF, D_MODEL), const(1, D_MODEL), const(1, D_MODEL),
    ]
    args = [x, o_self, o_mem, lw["w_o"], lw["w_o"], lw["g1"], lw["b1"], lw["w_gate"], lw["conv_w"],
            lw["conv_b"], lw["w_up"], lw["w_down"], lw["g2"], lw["b2"]]
    if sample:
        in_specs += [tok(D_FF), tok(D_FF)]
        args += list(inj)
        out_specs = [tok(D_MODEL), tok(D_FF)]
        out_shape = [jax.ShapeDtypeStruct((B, T, D_MODEL), f32), jax.ShapeDtypeStruct((B, T, D_FF), f32)]
        scratch = []
    else:
        out_specs = [tok(D_MODEL), pl.BlockSpec((1, 8, D_FF), lambda b, i: (b, 0, 0))]
        out_shape = [jax.ShapeDtypeStruct((B, T, D_MODEL), f32), jax.ShapeDtypeStruct((B, 8, D_FF), f32)]
        scratch = [pltpu.VMEM((8, D_FF), f32)]
    scratch += [pltpu.VMEM((FFN_SUBTILES, TOKEN_TILE, D_MODEL), f32), pltpu.VMEM((FFN_SUBTILES, TOKEN_TILE, D_FF), f32),
                pltpu.VMEM((FFN_SUBTILES, TOKEN_TILE, D_FF), f32)]
    return pl.pallas_call(
        functools.partial(_merge_ffn_kernel, sample=sample),
        grid=(B, nt),
        in_specs=in_specs,
        out_specs=out_specs,
        out_shape=out_shape,
        scratch_shapes=scratch,
        compiler_params=_cparams(2),
        name="merge_ffn_sample" if sample else "merge_ffn",
    )(*args)


def _store_heads_transposed(ot_ref, out_ref):
    out_ref[0] = ot_ref[...].T.astype(out_ref.dtype)


def _swa_prompt_kernel(q_ref, kc_ref, kp_ref, vc_ref, vp_ref, sink_ref, out_ref, ot_ref):
    i = pl.program_id(1)
    tm = TOKEN_TILE
    ncol = GROUP * tm
    key_c = lax.broadcasted_iota(jnp.int32, (tm, ncol), 0)
    key_p = lax.broadcasted_iota(jnp.int32, (WINDOW, ncol), 0)
    qc = lax.broadcasted_iota(jnp.int32, (1, ncol), 1) % tm
    valid_c = (key_c <= qc) & (key_c > qc - WINDOW)
    valid_p = (key_p > qc) & (i > 0)
    for h in range(N_KV_HEADS):
        p = h // 2
        qh = q_ref[0, h]
        s_c = jnp.where(valid_c, _dot_nt(kc_ref[0, p, 0], qh), NEG_INF)
        s_p = jnp.where(valid_p, _dot_nt(kp_ref[0, p, 0], qh), NEG_INF)
        sink = sink_ref[h]
        m = jnp.maximum(jnp.maximum(jnp.max(s_c, axis=0, keepdims=True), jnp.max(s_p, axis=0, keepdims=True)), sink)
        e_c = jnp.exp2(s_c - m)
        e_p = jnp.exp2(s_p - m)
        ot = _dot(vc_ref[0, h, 0], e_c.astype(bf16)) + _dot(vp_ref[0, h, 0], e_p.astype(bf16))
        den = ot[HEAD_DIM:HEAD_DIM + 1, :] + jnp.exp2(sink - m)
        ot = ot[:HEAD_DIM, :] / den
        for g in range(GROUP):
            hq = h * GROUP + g
            ot_ref[hq * HEAD_DIM:(hq + 1) * HEAD_DIM, :] = ot[:, g * tm:(g + 1) * tm]
    _store_heads_transposed(ot_ref, out_ref)


def _swa_prompt(qpad, k5, vt5, sink_cols):
    B = qpad.shape[0]
    nt = k5.shape[2]
    tm = TOKEN_TILE
    T = nt * tm
    prev = lambda i: jnp.maximum(i - 1, 0)
    return pl.pallas_call(
        _swa_prompt_kernel,
        grid=(B, nt),
        in_specs=[
            pl.BlockSpec((1, N_KV_HEADS, GROUP * tm, LANES), lambda b, i: (b, 0, i, 0)),
            pl.BlockSpec((1, KV_W // LANES, 1, tm, LANES), lambda b, i: (b, 0, i, 0, 0)),
            pl.BlockSpec((1, KV_W // LANES, 1, WINDOW, LANES), lambda b, i: (b, 0, prev(i), tm // WINDOW - 1, 0)),
            pl.BlockSpec((1, N_KV_HEADS, 1, V_ROWS, tm), lambda b, i: (b, 0, i, 0, 0)),
            pl.BlockSpec((1, N_KV_HEADS, 1, V_ROWS, WINDOW), lambda b, i: (b, 0, prev(i), 0, tm // WINDOW - 1)),
            pl.BlockSpec((N_KV_HEADS, 1, GROUP * tm), lambda b, i: (0, 0, 0)),
        ],
        out_specs=pl.BlockSpec((1, tm, Q_W), lambda b, i: (b, i, 0)),
        out_shape=jax.ShapeDtypeStruct((B, T, Q_W), bf16),
        scratch_shapes=[pltpu.VMEM((Q_W, tm), f32)],
        compiler_params=_cparams(2),
        name="swa_prompt",
    )(qpad, k5, k5, vt5, vt5, sink_cols)


def _mem_attn_prompt_kernel(q_ref, k_ref, vt_ref, out_ref, ot_ref):
    tm = q_ref.shape[2]
    for p in range(MEM_W // LANES):
        qp = q_ref[0, 2 * p:2 * p + 2].reshape(2 * tm, LANES)
        s = _dot_nt(k_ref[0, p], qp)
        m = jnp.max(s, axis=0, keepdims=True)
        e = jnp.exp2(s - m).astype(bf16)
        for hh in range(2):
            h = 2 * p + hh
            ot = _dot(vt_ref[0, h], e[:, hh * tm:(hh + 1) * tm])
            ot_ref[h * HEAD_DIM:(h + 1) * HEAD_DIM, :] = ot[:HEAD_DIM, :] / ot[HEAD_DIM:HEAD_DIM + 1, :]
    _store_heads_transposed(ot_ref, out_ref)


def _mem_attn_prompt(qmpad, mk2, mvt):
    B, _, T, _ = qmpad.shape
    tm = TOKEN_TILE
    return pl.pallas_call(
        _mem_attn_prompt_kernel,
        grid=(B, T // tm),
        in_specs=[
            pl.BlockSpec((1, MEM_HEADS, tm, LANES), lambda b, i: (b, 0, i, 0)),
            pl.BlockSpec((1, MEM_W // LANES, N_MEM, LANES), lambda b, i: (b, 0, 0, 0)),
            pl.BlockSpec((1, MEM_HEADS, V_ROWS, N_MEM), lambda b, i: (b, 0, 0, 0)),
        ],
        out_specs=pl.BlockSpec((1, tm, MEM_W), lambda b, i: (b, i, 0)),
        out_shape=jax.ShapeDtypeStruct((B, T, MEM_W), bf16),
        scratch_shapes=[pltpu.VMEM((MEM_W, tm), f32)],
        compiler_params=_cparams(2),
        name="mem_attn_prompt",
    )(qmpad, mk2, mvt)


def _top3_bias(gate, eligible, axis):
    idx = lax.broadcasted_iota(jnp.int32, gate.shape, axis)
    n = gate.shape[axis]
    rest = jnp.where(eligible, gate, NEG_INF)
    chosen = jnp.zeros(gate.shape, dtype=jnp.bool_)
    for _ in range(MOBA_TOPK):
        best = jnp.max(rest, axis=axis, keepdims=True)
        first = jnp.min(jnp.where(rest == best, idx, n), axis=axis, keepdims=True)
        pick = (idx == first) & (best > NEG_INF)
        chosen = chosen | pick
        rest = jnp.where(pick, NEG_INF, rest)
    return jnp.where(chosen, 0.0, NEG_INF)


def _moba_gate_kernel(q_ref, km_ref, bias_ref):
    i = pl.program_id(1)
    tm = TOKEN_TILE
    nb = km_ref.shape[1]
    q = q_ref[0]
    km = km_ref[0]
    q_hi = q.astype(bf16)
    q_lo = (q - q_hi.astype(f32)).astype(bf16)
    lane = lax.broadcasted_iota(jnp.int32, (1, LANES), 1)
    lo = lane < HEAD_DIM
    blk = lax.broadcasted_iota(jnp.int32, (nb, 1), 0)
    for h in range(N_KV_HEADS):
        kt = km[:, (h // 2) * LANES:(h // 2 + 1) * LANES]
        gates = []
        for g in range(GROUP):
            hq = h * GROUP + g
            kk = kt if hq % 2 == h % 2 else pltpu.roll(kt, HEAD_DIM, 1)
            kk = jnp.where(lo if hq % 2 == 0 else jnp.logical_not(lo), kk, 0.0)
            kk_hi = kk.astype(bf16)
            kk_lo = (kk - kk_hi.astype(f32)).astype(bf16)
            cols = slice((hq // 2) * LANES, (hq // 2 + 1) * LANES)
            gates.append(_dot_nt(kk_hi, q_hi[:, cols]) + _dot_nt(kk_hi, q_lo[:, cols])
                         + _dot_nt(kk_lo, q_hi[:, cols]))
        gate = jnp.concatenate(gates, axis=1)
        bias_ref[0, 0, h] = _top3_bias(gate, blk < i, 0)


def _moba_gate(q, kmean):
    B, T, _ = q.shape
    tm = TOKEN_TILE
    nb = T // tm
    return pl.pallas_call(
        _moba_gate_kernel,
        grid=(B, nb),
        in_specs=[
            pl.BlockSpec((1, tm, Q_W), lambda b, i: (b, i, 0)),
            pl.BlockSpec((1, nb, KV_W), lambda b, i: (b, 0, 0)),
        ],
        out_specs=pl.BlockSpec((1, 1, N_KV_HEADS, nb, GROUP * tm), lambda b, i: (b, i, 0, 0, 0)),
        out_shape=jax.ShapeDtypeStruct((B, nb, N_KV_HEADS, nb, GROUP * tm), f32),
        compiler_params=_cparams(2),
        name="moba_gate",
    )(q, kmean)


def _moba_prompt_kernel(q_ref, k_ref, vt_ref, bias_ref, out_ref, acc_ref, m_ref, sa_ref, sb_ref):
    i = pl.program_id(1)
    tm = TOKEN_TILE
    nb = k_ref.shape[2]
    n_pairs = KV_W // LANES
    per_pair = N_HEADS // n_pairs
    key = lax.broadcasted_iota(jnp.int32, (tm, tm), 0)
    qc = lax.broadcasted_iota(jnp.int32, (1, tm), 1)
    causal = key <= qc

    def scores_into(j, s_ref):
        for p in range(n_pairs):
            qp = q_ref[0, 2 * p:2 * p + 2].reshape(per_pair * tm, LANES)
            s_ref[p] = _dot_nt(k_ref[0, p, j], qp)

    def chunk(s_ref, c):
        return s_ref[c // per_pair, :, (c % per_pair) * tm:(c % per_pair + 1) * tm]

    scores_into(i, sb_ref)
    scores_into(0, sa_ref)
    for c in range(N_HEADS):
        s = jnp.where(causal, chunk(sb_ref, c), NEG_INF)
        m = jnp.max(s, axis=0, keepdims=True)
        e = jnp.exp2(s - m)
        m_ref[c] = m
        acc_ref[c] = _dot(vt_ref[0, c // GROUP, i], e.astype(bf16))

    def consume(j, s_ref):
        for c in range(N_HEADS):
            h, g = divmod(c, GROUP)
            s = chunk(s_ref, c)
            bias = bias_ref[0, 0, h, pl.ds(j, 1), g * tm:(g + 1) * tm]
            m = m_ref[c]
            m_new = jnp.maximum(m, jnp.max(s, axis=0, keepdims=True) + bias)
            a = jnp.exp2(m - m_new)
            e = jnp.exp2(s - (m_new - bias))
            m_ref[c] = m_new
            acc_ref[c] = a * acc_ref[c] + _dot(vt_ref[0, h, j], e.astype(bf16))

    def body(jj, carry):
        j = 2 * jj
        scores_into(j + 1, sb_ref)
        consume(j, sa_ref)
        scores_into(jnp.minimum(j + 2, nb - 1), sa_ref)
        consume(j + 1, sb_ref)
        return carry

    lax.fori_loop(0, (i + 1) // 2, body, 0)
    heads = [acc_ref[c, :HEAD_DIM, :] / acc_ref[c, HEAD_DIM:HEAD_DIM + 1, :] for c in range(N_HEADS)]
    out_ref[0] = jnp.concatenate(heads, axis=0).T.astype(out_ref.dtype)


def _moba_prompt(qpad, k5, vt5, bias):
    B = qpad.shape[0]
    nb = k5.shape[2]
    tm = TOKEN_TILE
    T = nb * tm
    return pl.pallas_call(
        _moba_prompt_kernel,
        grid=(B, nb),
        in_specs=[
            pl.BlockSpec((1, N_KV_HEADS, GROUP * tm, LANES), lambda b, i: (b, 0, i, 0)),
            pl.BlockSpec((1, KV_W // LANES, nb, tm, LANES), lambda b, i: (b, 0, 0, 0, 0)),
            pl.BlockSpec((1, N_KV_HEADS, nb, V_ROWS, tm), lambda b, i: (b, 0, 0, 0, 0)),
            pl.BlockSpec((1, 1, N_KV_HEADS, nb, GROUP * tm), lambda b, i: (b, i, 0, 0, 0)),
        ],
        out_specs=pl.BlockSpec((1, tm, Q_W), lambda b, i: (b, i, 0)),
        out_shape=jax.ShapeDtypeStruct((B, T, Q_W), bf16),
        scratch_shapes=[pltpu.VMEM((N_HEADS, V_ROWS, tm), f32), pltpu.VMEM((N_HEADS, 1, tm), f32),
                        pltpu.VMEM((KV_W // LANES, tm, N_HEADS * tm * LANES // KV_W), f32),
                        pltpu.VMEM((KV_W // LANES, tm, N_HEADS * tm * LANES // KV_W), f32)],
        compiler_params=_cparams(2),
        name="moba_prompt",
    )(qpad, k5, vt5, bias)


SAMPLE_ROWS = N_HEADS * 4
NEW_PAD = 8
SEQ_GROUP = 8


def _pad_rows(a, n):
    return jnp.concatenate([a, jnp.zeros((n - a.shape[0], a.shape[1]), a.dtype)], axis=0)


def _swa_sample_kernel(q_ref, kc_ref, vc_ref, kn_ref, vn_ref, sink_ref, o_ref, nk_ref, nv_ref):
    wb = kc_ref.shape[2]
    col = lax.broadcasted_iota(jnp.int32, (SAMPLE_ROWS, wb), 1)
    s_idx = lax.broadcasted_iota(jnp.int32, (SAMPLE_ROWS, 1), 0) % 4
    valid_c = col > s_idx
    valid_n = col <= s_idx
    lane = lax.broadcasted_iota(jnp.int32, (1, wb), 1)
    sink = sink_ref[...]
    for n in range(SEQ_GROUP):
        q = q_ref[n]
        kc = kc_ref[n]
        vc = vc_ref[n]
        kn = _pad_rows(kn_ref[n], wb)
        vn = _pad_rows(vn_ref[n], wb)
        s_c = jnp.where(valid_c, _dot(q, kc.astype(bf16)), NEG_INF)
        s_n = jnp.where(valid_n, _dot_nt(q, kn.astype(bf16)), NEG_INF)
        m = jnp.maximum(jnp.maximum(jnp.max(s_c, axis=1, keepdims=True), jnp.max(s_n, axis=1, keepdims=True)), sink)
        e_c = jnp.exp(s_c - m)
        e_n = jnp.exp(s_n - m)
        den = jnp.sum(e_c, axis=1, keepdims=True) + jnp.sum(e_n, axis=1, keepdims=True) + jnp.exp(sink - m)
        o = _dot_nt(e_c.astype(bf16), vc.astype(bf16)) + _dot(e_n.astype(bf16), vn.astype(bf16))
        o_ref[n] = o / den
        nk_ref[n] = jnp.where(lane < wb - 4, pltpu.roll(kc, wb - 4, 1), pltpu.roll(kn.T, wb - 4, 1))
        nv_ref[n] = jnp.where(lane < wb - 4, pltpu.roll(vc, wb - 4, 1), pltpu.roll(vn.T, wb - 4, 1))


def _swa_sample(qbd, kc, vc, kn, vn, sink_rows):
    n_seq, _, wb = kc.shape
    sg = SEQ_GROUP
    seq = lambda r, w: pl.BlockSpec((sg, r, w), lambda i: (i, 0, 0))
    return pl.pallas_call(
        _swa_sample_kernel,
        grid=(n_seq // sg,),
        in_specs=[seq(SAMPLE_ROWS, KV_W), seq(KV_W, wb), seq(KV_W, wb), seq(NEW_PAD, KV_W), seq(NEW_PAD, KV_W),
                  pl.BlockSpec((SAMPLE_ROWS, 1), lambda i: (0, 0))],
        out_specs=[seq(SAMPLE_ROWS, KV_W), seq(KV_W, wb), seq(KV_W, wb)],
        out_shape=[jax.ShapeDtypeStruct((n_seq, SAMPLE_ROWS, KV_W), f32),
                   jax.ShapeDtypeStruct((n_seq, KV_W, wb), f32),
                   jax.ShapeDtypeStruct((n_seq, KV_W, wb), f32)],
        compiler_params=_cparams(1),
        name="swa_sample",
    )(qbd, kc, vc, kn, vn, sink_rows)


def _mem_sample_kernel(q_ref, k_ref, v_ref, o_ref):
    for n in range(SEQ_GROUP):
        s = _dot(q_ref[n], k_ref[0, n].astype(bf16))
        m = jnp.max(s, axis=1, keepdims=True)
        e = jnp.exp(s - m)
        den = jnp.sum(e, axis=1, keepdims=True)
        o_ref[n] = _dot_nt(e.astype(bf16), v_ref[0, n].astype(bf16)) / den


def _mem_sample(qbd, km, vm, layer):
    n_seq = km.shape[1]
    sg = SEQ_GROUP
    rows = qbd.shape[1]
    cache = pl.BlockSpec((1, sg, MEM_W, N_MEM), lambda i: (layer, i, 0, 0))
    return pl.pallas_call(
        _mem_sample_kernel,
        grid=(n_seq // sg,),
        in_specs=[pl.BlockSpec((sg, rows, MEM_W), lambda i: (i, 0, 0)), cache, cache],
        out_specs=pl.BlockSpec((sg, rows, MEM_W), lambda i: (i, 0, 0)),
        out_shape=jax.ShapeDtypeStruct((n_seq, rows, MEM_W), f32),
        compiler_params=_cparams(1),
        name="mem_sample",
    )(qbd, km, vm)


def _moba_sample_kernel(pt_ref, q_ref, qf_ref, kn_ref, vn_ref, expand_ref, kpool_ref, vpool_ref, o_ref,
                        kbuf, vbuf, sem):
    b = pl.program_id(0)
    n_seq = pl.num_programs(0)
    n_pages = pt_ref.shape[1]
    page = kpool_ref.shape[2]
    past = n_pages * page
    nb = past // MOBA_BLOCK
    slot = b % 2

    def page_copy(which, seq, slot_, pg):
        pool, buf = ((kpool_ref, kbuf), (vpool_ref, vbuf))[which]
        return pltpu.make_async_copy(pool.at[pt_ref[seq, pg]], buf.at[slot_, :, pg * page:(pg + 1) * page],
                                     sem.at[which, slot_])

    def fetch(seq, slot_):
        for pg in range(n_pages):
            page_copy(0, seq, slot_, pg).start()
        for pg in range(n_pages):
            page_copy(1, seq, slot_, pg).start()

    def wait(which, seq, slot_):
        for pg in range(n_pages):
            page_copy(which, seq, slot_, pg).wait()

    @pl.when(b == 0)
    def _():
        fetch(0, 0)

    @pl.when(b + 1 < n_seq)
    def _():
        fetch(b + 1, 1 - slot)

    wait(0, b, slot)
    lane = lax.broadcasted_iota(jnp.int32, (1, LANES), 1)
    kmean = jnp.zeros((KV_W, LANES), f32)
    for j in range(nb):
        part = kbuf[slot, :, j * MOBA_BLOCK:j * MOBA_BLOCK + LANES]
        for c in range(1, MOBA_BLOCK // LANES):
            part = part + kbuf[slot, :, j * MOBA_BLOCK + c * LANES:j * MOBA_BLOCK + (c + 1) * LANES]
        kmean = jnp.where(lane == j, jnp.sum(part, axis=1, keepdims=True), kmean)
    kmean = kmean * (1.0 / MOBA_BLOCK)
    gate = jnp.dot(qf_ref[0], kmean, preferred_element_type=f32, precision=lax.Precision.HIGHEST)
    bias = _top3_bias(gate, lane < nb, 1)
    chosen = jnp.where(bias == 0.0, 1.0, 0.0).astype(bf16)
    keep = _dot(chosen, expand_ref[...])
    q = q_ref[0]
    s = jnp.where(keep > 0.5, _dot(q, kbuf[slot].astype(bf16)), NEG_INF)
    col = lax.broadcasted_iota(jnp.int32, (SAMPLE_ROWS, LANES), 1)
    s_idx = lax.broadcasted_iota(jnp.int32, (SAMPLE_ROWS, 1), 0) % 4
    kn = _pad_rows(kn_ref[0], LANES).astype(bf16)
    vn = _pad_rows(vn_ref[0], LANES).astype(bf16)
    s_n = jnp.where(col <= s_idx, _dot_nt(q, kn), NEG_INF)
    m = jnp.maximum(jnp.max(s, axis=1, keepdims=True), jnp.max(s_n, axis=1, keepdims=True))
    e = jnp.exp(s - m)
    e_n = jnp.exp(s_n - m)
    den = jnp.sum(e, axis=1, keepdims=True) + jnp.sum(e_n, axis=1, keepdims=True)
    wait(1, b, slot)
    o = _dot_nt(e.astype(bf16), vbuf[slot].astype(bf16)) + _dot(e_n.astype(bf16), vn)
    o_ref[0] = o / den


def _moba_sample(page_table, qbd, qbd_f32, kn, vn, kpool, vpool):
    n_seq, n_pages = page_table.shape
    page = kpool.shape[2]
    past = n_pages * page
    nb = past // MOBA_BLOCK
    expand = (jnp.arange(past, dtype=jnp.int32)[None, :] // MOBA_BLOCK == jnp.arange(LANES, dtype=jnp.int32)[:, None]).astype(bf16)
    seq = lambda r: pl.BlockSpec((1, r, KV_W), lambda i, pt: (i, 0, 0))
    grid_spec = pltpu.PrefetchScalarGridSpec(
        num_scalar_prefetch=1,
        grid=(n_seq,),
        in_specs=[seq(SAMPLE_ROWS), seq(SAMPLE_ROWS), seq(NEW_PAD), seq(NEW_PAD),
                  pl.BlockSpec((LANES, past), lambda i, pt: (0, 0)),
                  pl.BlockSpec(memory_space=pl.ANY), pl.BlockSpec(memory_space=pl.ANY)],
        out_specs=seq(SAMPLE_ROWS),
        scratch_shapes=[pltpu.VMEM((2, KV_W, past), f32), pltpu.VMEM((2, KV_W, past), f32),
                        pltpu.SemaphoreType.DMA((2, 2))],
    )
    return pl.pallas_call(
        _moba_sample_kernel,
        grid_spec=grid_spec,
        out_shape=jax.ShapeDtypeStruct((n_seq, SAMPLE_ROWS, KV_W), f32),
        compiler_params=_cparams(1),
        name="moba_sample",
    )(page_table, qbd, qbd_f32, kn, vn, expand, kpool, vpool)


def _block_diag_rows(a, n_heads, per_kv):
    n_seq, S, _ = a.shape
    n_kv = n_heads // per_kv
    a = a.reshape(n_seq, S, n_kv, per_kv, HEAD_DIM).transpose(0, 2, 3, 1, 4)
    eye = jnp.eye(n_kv, dtype=a.dtype)
    bd = a[:, :, :, :, None, :] * eye[None, :, None, None, :, None]
    return bd.reshape(n_seq, n_heads * S, n_kv * HEAD_DIM)


def _diag_rows_out(o, n_heads, per_kv, S):
    n_seq = o.shape[0]
    n_kv = n_heads // per_kv
    o = o.reshape(n_seq, n_kv, per_kv, S, n_kv, HEAD_DIM)
    o = jnp.stack([o[:, h, :, :, h, :] for h in range(n_kv)], axis=1)
    return o.transpose(0, 3, 1, 2, 4).reshape(n_seq, S, n_heads * HEAD_DIM)


def _stacked_weights(w_o, ln1_g, ln1_b, w_ffn_gate, conv_w, conv_b, w_ffn_up, w_ffn_down, ln2_g, ln2_b):
    row = lambda a: a[:, None, :]
    return dict(
        w_o=w_o.astype(bf16), g1=row(ln1_g), b1=row(ln1_b),
        w_gate=w_ffn_gate.astype(bf16), conv_w=conv_w, conv_b=row(conv_b),
        w_up=w_ffn_up.astype(bf16), w_down=w_ffn_down.astype(bf16),
        g2=row(ln2_g), b2=row(ln2_b),
    )


def _conv_injection(state):
    n_seq = state.shape[0]
    z = jnp.zeros((n_seq, 1, D_FF), state.dtype)
    inj1 = jnp.concatenate([state[:, 1:2], z, z, z], axis=1)
    inj2 = jnp.concatenate([state[:, 0:1], state[:, 1:2], z, z], axis=1)
    return inj1.reshape(1, n_seq * 4, D_FF), inj2.reshape(1, n_seq * 4, D_FF)


def kernel(x_prompt, x_sample, cache_win_k, cache_win_v, cache_moba_k, cache_moba_v, page_table, cache_mem_k, cache_mem_v, state_conv, mem_prompt, w_in_a, sink_a, w_in_b, w_kv_shared, w_mem_kv, w_o, ln1_g, ln1_b, w_ffn_gate, conv_w, conv_b, w_ffn_up, w_ffn_down, ln2_g, ln2_b):
    B, T, _ = x_prompt.shape
    Bd, S, _ = x_sample.shape
    n_pool, page = cache_moba_k.shape[:2]
    past = page_table.shape[1] * page
    wb = cache_win_k.shape[2]
    nb = T // TOKEN_TILE

    cos_p, sin_p = _rope_tables(jnp.arange(T, dtype=jnp.int32))
    cos_s, sin_s = _rope_tables(past + (jnp.arange(Bd * S, dtype=jnp.int32) % S))
    w_a = w_in_a[0].astype(bf16)
    w_b = jnp.concatenate([w_in_b[0][:, :Q_W], w_kv_shared, w_in_b[0][:, Q_W:]], axis=1).astype(bf16)
    lw = _stacked_weights(w_o, ln1_g, ln1_b, w_ffn_gate, conv_w, conv_b, w_ffn_up, w_ffn_down, ln2_g, ln2_b)
    xs = x_sample.reshape(1, Bd * S, D_MODEL)

    def feature_major(a):
        a = jnp.moveaxis(a, -3, -1)
        return a.reshape(a.shape[:-3] + (a.shape[-3] * HEAD_DIM, a.shape[-1]))

    def token_major(a, n_heads):
        a = a.reshape(a.shape[:-2] + (n_heads, HEAD_DIM, a.shape[-1]))
        return jnp.moveaxis(a, -1, -3)

    mem_kvt, mem_k2, mem_vt = _memkv(mem_prompt, w_mem_kv.astype(bf16))
    mem_k_out = token_major(mem_kvt[:, :, :MEM_W], MEM_HEADS)
    mem_v_out = token_major(mem_kvt[:, :, MEM_W:], MEM_HEADS)
    cache_mem_kt = feature_major(cache_mem_k)
    cache_mem_vt = feature_major(cache_mem_v)

    def sample_rows(a, n_heads, per_kv):
        return _block_diag_rows(a.reshape(Bd, S, n_heads * HEAD_DIM), n_heads, per_kv)

    def pad_new(a):
        return jnp.pad(a.reshape(Bd, S, KV_W), ((0, 0), (0, NEW_PAD - S), (0, 0)))

    def sample_mem(qm, layer):
        qbd = (sample_rows(qm, MEM_HEADS, 1) * SCALE).astype(bf16)
        o = _mem_sample(qbd, cache_mem_kt, cache_mem_vt, layer)
        return _diag_rows_out(o, MEM_HEADS, 1, S).reshape(1, Bd * S, MEM_W).astype(bf16)

    _, kat, vat, qpad, qmpad, k5, vt5, _ = _proj(x_prompt, w_a, cos_p, sin_p, True)
    sink_cols = jnp.repeat(sink_a[0].reshape(N_KV_HEADS, GROUP) * LOG2E, TOKEN_TILE, axis=1)[:, None, :]
    o_self = _swa_prompt(qpad, k5, vt5, sink_cols)
    o_mem = _mem_attn_prompt(qmpad, mem_k2[0], mem_vt[0])
    xp, tail0 = _merge_ffn(x_prompt, o_self, o_mem, lw, 0)
    win_kp = token_major(kat[:, :, T - wb:], N_KV_HEADS)[None]
    win_vp = token_major(vat[:, :, T - wb:], N_KV_HEADS)[None]

    qs, ks, vs, qms = _proj(xs, w_a, cos_s, sin_s, False)
    qbd = (sample_rows(qs, N_HEADS, GROUP) * SCALE).astype(bf16)
    sink_rows = jnp.repeat(sink_a[0], S)[:, None]
    o_s, win_kst, win_vst = _swa_sample(qbd, feature_major(cache_win_k[0]), feature_major(cache_win_v[0]),
                                        pad_new(ks), pad_new(vs), sink_rows)
    os_self = _diag_rows_out(o_s, N_HEADS, GROUP, S).reshape(1, Bd * S, Q_W).astype(bf16)
    xs, u0 = _merge_ffn(xs, os_self, sample_mem(qms, 0), lw, 0, _conv_injection(state_conv[0]))

    qb, kbt, vbt, qpad, qmpad, k5, vt5, kmean = _proj(xp, w_b, cos_p, sin_p, True)
    bias = _moba_gate(qb, kmean.reshape(B, nb, KV_W))
    o_self = _moba_prompt(qpad, k5, vt5, bias)
    o_mem = _mem_attn_prompt(qmpad, mem_k2[1], mem_vt[1])
    yp, tail1 = _merge_ffn(xp, o_self, o_mem, lw, 1)

    qs, ks, vs, qms = _proj(xs, w_b, cos_s, sin_s, False)
    qbd_f32 = sample_rows(qs, N_HEADS, GROUP)
    o_s = _moba_sample(page_table, (qbd_f32 * SCALE).astype(bf16), qbd_f32, pad_new(ks), pad_new(vs),
                       feature_major(cache_moba_k), feature_major(cache_moba_v))
    os_self = _diag_rows_out(o_s, N_HEADS, GROUP, S).reshape(1, Bd * S, Q_W).astype(bf16)
    ys, u1 = _merge_ffn(xs, os_self, sample_mem(qms, 1), lw, 1, _conv_injection(state_conv[1]))

    kv_heads = lambda a, lead: a.reshape(lead + (N_KV_HEADS, HEAD_DIM))
    conv_p = jnp.stack([tail0[:, 8 - (CONV_W - 1):], tail1[:, 8 - (CONV_W - 1):]])
    conv_s = jnp.stack([u.reshape(Bd, S, D_FF)[:, S - (CONV_W - 1):] for u in (u0, u1)])
    return (
        yp, ys.reshape(Bd, S, D_MODEL),
        win_kp, win_vp,
        token_major(kbt, N_KV_HEADS), token_major(vbt, N_KV_HEADS),
        mem_k_out, mem_v_out,
        conv_p,
        token_major(win_kst, N_KV_HEADS)[None], token_major(win_vst, N_KV_HEADS)[None],
        kv_heads(ks, (Bd, S)), kv_heads(vs, (Bd, S)),
        conv_s,
    )
```

```python
import functools
import math

import jax
import jax.numpy as jnp
from jax import lax
from jax.experimental import pallas as pl
from jax.experimental.pallas import tpu as pltpu

D_MODEL = 1024
HEAD_DIM = 64
N_HEADS = 12
N_KV_HEADS = 4
GROUP = N_HEADS // N_KV_HEADS
MEM_HEADS = 4
N_MEM = 256
Q_W = N_HEADS * HEAD_DIM
KV_W = N_KV_HEADS * HEAD_DIM
MEM_W = MEM_HEADS * HEAD_DIM
PROJ_W = Q_W + 2 * KV_W + MEM_W
WINDOW = 128
MOBA_BLOCK = 256
MOBA_TOPK = 3
D_FF = 2816
CONV_W = 3
ROPE_THETA = 10000.0
LN_EPS = 1e-5
DEPTH = 2
ALPHA = (2 * DEPTH) ** 0.25
SCALE = HEAD_DIM ** -0.5
LOG2E = math.log2(math.e)
SCALE_LOG2 = SCALE * LOG2E

LANES = 128
V_ROWS = HEAD_DIM + 16
TOKEN_TILE = 256
VMEM_LIMIT = 56 * 1024 * 1024
NEG_INF = float("-inf")

f32 = jnp.float32
bf16 = jnp.bfloat16


def _cparams(n_grid, vmem=VMEM_LIMIT, flags=None):
    return pltpu.CompilerParams(dimension_semantics=("arbitrary",) * n_grid, vmem_limit_bytes=vmem, flags=flags)


def _dot_nt(a, b, precision=None):
    return lax.dot_general(a, b, (((1,), (1,)), ((), ())), preferred_element_type=f32, precision=precision)


def _dot(a, b):
    return jnp.dot(a, b, preferred_element_type=f32)


def _layer_norm(r, g, b):
    mu = jnp.mean(r, axis=-1, keepdims=True)
    d = r - mu
    var = jnp.mean(d * d, axis=-1, keepdims=True)
    return d * lax.rsqrt(var + LN_EPS) * g + b


def _ones_row_block(width):
    row = lax.broadcasted_iota(jnp.int32, (V_ROWS - HEAD_DIM, width), 0)
    return jnp.where(row == 0, 1.0, 0.0).astype(f32)


def _half_mask(width):
    lane = lax.broadcasted_iota(jnp.int32, (1, width), 1)
    return (lane % HEAD_DIM) < (HEAD_DIM // 2)


PROJ_SUBTILES = 2


def _proj_kernel(x_ref, w_ref, cos_ref, sin_ref, *rest, layouts, want_q):
    outs = list(rest[:-1])
    y_s = rest[-1]
    q_ref = outs.pop(0) if want_q else None
    tm = TOKEN_TILE
    n_sub = x_ref.shape[1] // tm
    for t in range(n_sub):
        y_s[t] = _dot(x_ref[0, t * tm:(t + 1) * tm, :].astype(bf16), w_ref[...])
    width = Q_W + KV_W
    half = HEAD_DIM // 2
    first_half = _half_mask(width)
    lane = lax.broadcasted_iota(jnp.int32, (1, LANES), 1)
    lo = lane < HEAD_DIM

    def placed(tile, src_half, dst_half):
        t_ = tile if src_half == dst_half else pltpu.roll(tile, HEAD_DIM, 1)
        keep = lo if dst_half == 0 else jnp.logical_not(lo)
        return jnp.where(keep, t_, 0.0)

    for t in range(n_sub):
        rows = slice(t * tm, (t + 1) * tm)
        y = y_s[t]
        qk = y[:, :width]
        cos = jnp.tile(cos_ref[rows, :], (1, width // LANES))
        sin = jnp.tile(sin_ref[rows, :], (1, width // LANES))
        partner = jnp.where(first_half, pltpu.roll(qk, width - half, 1), pltpu.roll(qk, half, 1))
        rot = qk * cos + partner * sin
        q = rot[:, :Q_W]
        k = rot[:, Q_W:]
        v = y[:, Q_W + KV_W:Q_W + 2 * KV_W]
        qm = y[:, Q_W + 2 * KV_W:]
        if want_q:
            q_ref[0, rows, :] = q
        if not layouts:
            k_ref, v_ref, qm_ref = outs
            k_ref[0, rows, :] = k
            v_ref[0, rows, :] = v
            qm_ref[0, rows, :] = qm
            continue
        k_ref, v_ref, qpad_ref, qmpad_ref, k5_ref, vt5_ref, kmean_ref = outs
        kt = k.T
        vt = v.T
        k_ref[0, :, rows] = kt
        v_ref[0, :, rows] = vt
        for h in range(N_KV_HEADS):
            for g in range(GROUP):
                hq = h * GROUP + g
                tile = q[:, (hq // 2) * LANES:(hq // 2 + 1) * LANES]
                r0 = (t * GROUP + g) * tm
                qpad_ref[0, h, r0:r0 + tm, :] = (placed(tile, hq % 2, h % 2) * SCALE_LOG2).astype(bf16)
        for h in range(MEM_HEADS):
            tile = qm[:, (h // 2) * LANES:(h // 2 + 1) * LANES]
            qmpad_ref[0, h, rows, :] = (placed(tile, h % 2, h % 2) * SCALE_LOG2).astype(bf16)
        for p in range(KV_W // LANES):
            k5_ref[0, p, t] = k[:, p * LANES:(p + 1) * LANES].astype(bf16)
        ones_rows = _ones_row_block(tm)
        for h in range(N_KV_HEADS):
            vt5_ref[0, h, t] = jnp.concatenate([vt[h * HEAD_DIM:(h + 1) * HEAD_DIM, :], ones_rows], axis=0).astype(bf16)
        kmean_ref[0, t] = jnp.mean(k, axis=0, keepdims=True)


def _proj(x, w, cos, sin, layouts, want_q=True):
    B, T, _ = x.shape
    ns = PROJ_SUBTILES
    tm = TOKEN_TILE * ns
    nt = T // TOKEN_TILE
    n_tab = cos.shape[0] // tm
    tok = lambda w_: (jax.ShapeDtypeStruct((B, T, w_), f32), pl.BlockSpec((1, tm, w_), lambda b, i: (b, i, 0)))
    feat = lambda w_: (jax.ShapeDtypeStruct((B, w_, T), f32), pl.BlockSpec((1, w_, tm), lambda b, i: (b, 0, i)))
    outs = [tok(Q_W)] if want_q else []
    outs += [feat(KV_W), feat(KV_W)] if layouts else [tok(KV_W), tok(KV_W), tok(MEM_W)]
    out_shape = [o[0] for o in outs]
    out_specs = [o[1] for o in outs]
    if layouts:
        out_shape += [
            jax.ShapeDtypeStruct((B, N_KV_HEADS, GROUP * T, LANES), bf16),
            jax.ShapeDtypeStruct((B, MEM_HEADS, T, LANES), bf16),
            jax.ShapeDtypeStruct((B, KV_W // LANES, nt, TOKEN_TILE, LANES), bf16),
            jax.ShapeDtypeStruct((B, N_KV_HEADS, nt, V_ROWS, TOKEN_TILE), bf16),
            jax.ShapeDtypeStruct((B, nt, 1, KV_W), f32),
        ]
        out_specs += [
            pl.BlockSpec((1, N_KV_HEADS, GROUP * tm, LANES), lambda b, i: (b, 0, i, 0)),
            pl.BlockSpec((1, MEM_HEADS, tm, LANES), lambda b, i: (b, 0, i, 0)),
            pl.BlockSpec((1, KV_W // LANES, ns, TOKEN_TILE, LANES), lambda b, i: (b, 0, i, 0, 0)),
            pl.BlockSpec((1, N_KV_HEADS, ns, V_ROWS, TOKEN_TILE), lambda b, i: (b, 0, i, 0, 0)),
            pl.BlockSpec((1, ns, 1, KV_W), lambda b, i: (b, i, 0, 0)),
        ]
    return pl.pallas_call(
        functools.partial(_proj_kernel, layouts=layouts, want_q=want_q),
        grid=(B, T // tm),
        in_specs=[
            pl.BlockSpec((1, tm, D_MODEL), lambda b, i: (b, i, 0)),
            pl.BlockSpec((D_MODEL, PROJ_W), lambda b, i: (0, 0), pipeline_mode=pl.Buffered(1)),
            pl.BlockSpec((tm, LANES), lambda b, i: (i % n_tab, 0)),
            pl.BlockSpec((tm, LANES), lambda b, i: (i % n_tab, 0)),
        ],
        out_specs=out_specs,
        out_shape=out_shape,
        scratch_shapes=[pltpu.VMEM((ns, TOKEN_TILE, PROJ_W), f32)],
        compiler_params=_cparams(2),
        name="proj_layouts" if layouts else "proj",
    )(x, w, cos, sin)


def _rope_tables(pos):
    half = HEAD_DIM // 2
    inv = jnp.power(jnp.float32(ROPE_THETA), -jnp.arange(half, dtype=f32) / half)
    ang = pos.astype(f32)[:, None] * inv[None, :]
    cos = jnp.cos(ang)
    sin = jnp.sin(ang)
    reps = LANES // HEAD_DIM
    return jnp.tile(jnp.concatenate([cos, cos], axis=1), (1, reps)), jnp.tile(jnp.concatenate([-sin, sin], axis=1), (1, reps))


def _memkv_kernel(m_ref, w_ref, kv_ref, k2_ref, vt_ref):
    y = _dot(m_ref[0].astype(bf16), w_ref[0])
    yt = y.T
    kv_ref[0, 0] = yt
    for p in range(MEM_W // LANES):
        k2_ref[0, 0, p] = y[:, p * LANES:(p + 1) * LANES].astype(bf16)
    ones_rows = _ones_row_block(N_MEM)
    for h in range(MEM_HEADS):
        vh = yt[MEM_W + h * HEAD_DIM:MEM_W + (h + 1) * HEAD_DIM, :]
        vt_ref[0, 0, h] = jnp.concatenate([vh, ones_rows], axis=0).astype(bf16)


def _memkv(mem, w):
    B = mem.shape[0]
    return pl.pallas_call(
        _memkv_kernel,
        grid=(DEPTH, B),
        in_specs=[
            pl.BlockSpec((1, N_MEM, D_MODEL), lambda l, b: (b, 0, 0)),
            pl.BlockSpec((1, D_MODEL, 2 * MEM_W), lambda l, b: (l, 0, 0)),
        ],
        out_specs=[
            pl.BlockSpec((1, 1, 2 * MEM_W, N_MEM), lambda l, b: (l, b, 0, 0)),
            pl.BlockSpec((1, 1, MEM_W // LANES, N_MEM, LANES), lambda l, b: (l, b, 0, 0, 0)),
            pl.BlockSpec((1, 1, MEM_HEADS, V_ROWS, N_MEM), lambda l, b: (l, b, 0, 0, 0)),
        ],
        out_shape=[
            jax.ShapeDtypeStruct((DEPTH, B, 2 * MEM_W, N_MEM), f32),
            jax.ShapeDtypeStruct((DEPTH, B, MEM_W // LANES, N_MEM, LANES), bf16),
            jax.ShapeDtypeStruct((DEPTH, B, MEM_HEADS, V_ROWS, N_MEM), bf16),
        ],
        compiler_params=_cparams(2),
        name="memkv",
    )(mem, w)


def _gelu(c):
    return 0.5 * c * (1.0 + lax.erf(c * (1.0 / math.sqrt(2.0))))


def _merge_ffn_kernel(x_ref, os_ref, om_ref, wos_ref, wom_ref, g1_ref, b1_ref, wg_ref, cw_ref, cb_ref, wu_ref,
                      wd_ref, g2_ref, b2_ref, *rest, sample):
    if sample:
        inj1_ref, inj2_ref, y_ref, u_ref, x1_s, u_s, up_s = rest
    else:
        y_ref, tail_ref, carry_ref, x1_s, u_s, up_s = rest
        @pl.when(pl.program_id(1) == 0)
        def _():
            carry_ref[...] = jnp.zeros_like(carry_ref)
    tm = TOKEN_TILE
    n_sub = x_ref.shape[1] // tm
    for t in range(n_sub):
        rows = slice(t * tm, (t + 1) * tm)
        attn = _dot(os_ref[0, rows, :], wos_ref[0]) + _dot(om_ref[0, rows, :], wom_ref[0])
        x1 = _layer_norm(ALPHA * x_ref[0, rows, :] + attn, g1_ref[0], b1_ref[0])
        x1_s[t] = x1
        x1b = x1.astype(bf16)
        u_s[t] = _dot(x1b, wg_ref[0])
        up_s[t] = _dot(x1b, wu_ref[0])
    row = lax.broadcasted_iota(jnp.int32, (tm, 1), 0)
    cw = cw_ref[0]
    for t in range(n_sub):
        rows = slice(t * tm, (t + 1) * tm)
        u = u_s[t]
        r1 = pltpu.roll(u, 1, 0)
        r2 = pltpu.roll(u, 2, 0)
        if sample:
            s = row % 4
            prev1 = jnp.where(s >= 1, r1, inj1_ref[0, rows, :])
            prev2 = jnp.where(s >= 2, r2, inj2_ref[0, rows, :])
            u_ref[0, rows, :] = u
        else:
            c6 = carry_ref[6:7, :]
            c7 = carry_ref[7:8, :]
            prev1 = jnp.where(row == 0, c7, r1)
            prev2 = jnp.where(row == 0, c6, jnp.where(row == 1, c7, r2))
            carry_ref[...] = u[tm - 8:, :]
            if t == n_sub - 1:
                tail_ref[0] = u[tm - 8:, :]
        c = cb_ref[0] + prev2 * cw[0:1, :] + prev1 * cw[1:2, :] + u * cw[2:3, :]
        hid = (_gelu(c) * up_s[t]).astype(bf16)
        y_ref[0, rows, :] = _layer_norm(ALPHA * x1_s[t] + _dot(hid, wd_ref[0]), g2_ref[0], b2_ref[0])


FFN_SUBTILES = 2


def _merge_ffn(x, o_self, o_mem, lw, layer, inj=None):
    B, T, _ = x.shape
    tm = TOKEN_TILE * FFN_SUBTILES
    nt = T // tm
    sample = inj is not None
    tok = lambda w: pl.BlockSpec((1, tm, w), lambda b, i: (b, i, 0))
    const = lambda r, c, rb=0: pl.BlockSpec((1, r, c), lambda b, i: (layer, rb, 0), pipeline_mode=pl.Buffered(1))
    in_specs = [
        tok(D_MODEL), tok(Q_W), tok(MEM_W),
        const(Q_W, D_MODEL), const(MEM_W, D_MODEL, Q_W // MEM_W), const(1, D_MODEL), const(1, D_MODEL),
        const(D_MODEL, D_FF), const(CONV_W, D_FF), const(1, D_FF), const(D_MODEL, D_FF),
        const(D_FF, D_MODEL), const(1, D_MODEL), const(1, D_MODEL),
    ]
    args = [x, o_self, o_mem, lw["w_o"], lw["w_o"], lw["g1"], lw["b1"], lw["w_gate"], lw["conv_w"],
            lw["conv_b"], lw["w_up"], lw["w_down"], lw["g2"], lw["b2"]]
    if sample:
        in_specs += [tok(D_FF), tok(D_FF)]
        args += list(inj)
        out_specs = [tok(D_MODEL), tok(D_FF)]
        out_shape = [jax.ShapeDtypeStruct((B, T, D_MODEL), f32), jax.ShapeDtypeStruct((B, T, D_FF), f32)]
        scratch = []
    else:
        out_specs = [tok(D_MODEL), pl.BlockSpec((1, 8, D_FF), lambda b, i: (b, 0, 0))]
        out_shape = [jax.ShapeDtypeStruct((B, T, D_MODEL), f32), jax.ShapeDtypeStruct((B, 8, D_FF), f32)]
        scratch = [pltpu.VMEM((8, D_FF), f32)]
    scratch += [pltpu.VMEM((FFN_SUBTILES, TOKEN_TILE, D_MODEL), f32), pltpu.VMEM((FFN_SUBTILES, TOKEN_TILE, D_FF), f32),
                pltpu.VMEM((FFN_SUBTILES, TOKEN_TILE, D_FF), f32)]
    return pl.pallas_call(
        functools.partial(_merge_ffn_kernel, sample=sample),
        grid=(B, nt),
        in_specs=in_specs,
        out_specs=out_specs,
        out_shape=out_shape,
        scratch_shapes=scratch,
        compiler_params=_cparams(2),
        name="merge_ffn_sample" if sample else "merge_ffn",
    )(*args)


def _store_heads_transposed(ot_ref, out_ref):
    out_ref[0] = ot_ref[...].T.astype(out_ref.dtype)


def _swa_prompt_kernel(q_ref, kc_ref, kp_ref, vc_ref, vp_ref, sink_ref, out_ref, ot_ref, sc_ref, sp_ref):
    i = pl.program_id(1)
    tm = TOKEN_TILE
    n_pairs = KV_W // LANES
    per_pair = N_HEADS // n_pairs
    key_c = lax.broadcasted_iota(jnp.int32, (tm, tm), 0)
    key_p = lax.broadcasted_iota(jnp.int32, (WINDOW, tm), 0)
    qc = lax.broadcasted_iota(jnp.int32, (1, tm), 1)
    valid_c = (key_c <= qc) & (key_c > qc - WINDOW)
    valid_p = (key_p > qc) & (i > 0)
    for p in range(n_pairs):
        qp = q_ref[0, 2 * p:2 * p + 2].reshape(per_pair * tm, LANES)
        sc_ref[p] = _dot_nt(kc_ref[0, p, 0], qp)
        sp_ref[p] = _dot_nt(kp_ref[0, p, 0], qp)
    for c in range(N_HEADS):
        h, g = divmod(c, GROUP)
        cols = slice((c % per_pair) * tm, (c % per_pair + 1) * tm)
        s_c = jnp.where(valid_c, sc_ref[c // per_pair, :, cols], NEG_INF)
        s_p = jnp.where(valid_p, sp_ref[c // per_pair, :, cols], NEG_INF)
        sink = sink_ref[h, :, g * tm:(g + 1) * tm]
        m = jnp.maximum(jnp.maximum(jnp.max(s_c, axis=0, keepdims=True), jnp.max(s_p, axis=0, keepdims=True)), sink)
        e_c = jnp.exp2(s_c - m)
        e_p = jnp.exp2(s_p - m)
        ot = _dot(vc_ref[0, h, 0], e_c.astype(bf16)) + _dot(vp_ref[0, h, 0], e_p.astype(bf16))
        den = ot[HEAD_DIM:HEAD_DIM + 1, :] + jnp.exp2(sink - m)
        ot_ref[c * HEAD_DIM:(c + 1) * HEAD_DIM, :] = ot[:HEAD_DIM, :] / den
    _store_heads_transposed(ot_ref, out_ref)


def _swa_prompt(qpad, k5, vt5, sink_cols):
    B = qpad.shape[0]
    nt = k5.shape[2]
    tm = TOKEN_TILE
    T = nt * tm
    prev = lambda i: jnp.maximum(i - 1, 0)
    return pl.pallas_call(
        _swa_prompt_kernel,
        grid=(B, nt),
        in_specs=[
            pl.BlockSpec((1, N_KV_HEADS, GROUP * tm, LANES), lambda b, i: (b, 0, i, 0)),
            pl.BlockSpec((1, KV_W // LANES, 1, tm, LANES), lambda b, i: (b, 0, i, 0, 0)),
            pl.BlockSpec((1, KV_W // LANES, 1, WINDOW, LANES), lambda b, i: (b, 0, prev(i), tm // WINDOW - 1, 0)),
            pl.BlockSpec((1, N_KV_HEADS, 1, V_ROWS, tm), lambda b, i: (b, 0, i, 0, 0)),
            pl.BlockSpec((1, N_KV_HEADS, 1, V_ROWS, WINDOW), lambda b, i: (b, 0, prev(i), 0, tm // WINDOW - 1)),
            pl.BlockSpec((N_KV_HEADS, 1, GROUP * tm), lambda b, i: (0, 0, 0)),
        ],
        out_specs=pl.BlockSpec((1, tm, Q_W), lambda b, i: (b, i, 0)),
        out_shape=jax.ShapeDtypeStruct((B, T, Q_W), bf16),
        scratch_shapes=[pltpu.VMEM((Q_W, tm), f32),
                        pltpu.VMEM((KV_W // LANES, tm, N_HEADS * tm * LANES // KV_W), f32),
                        pltpu.VMEM((KV_W // LANES, WINDOW, N_HEADS * tm * LANES // KV_W), f32)],
        compiler_params=_cparams(2),
        name="swa_prompt",
    )(qpad, k5, k5, vt5, vt5, sink_cols)


def _mem_attn_prompt_kernel(q_ref, k_ref, vt_ref, out_ref, ot_ref, s_ref):
    tm = q_ref.shape[2]
    for p in range(MEM_W // LANES):
        qp = q_ref[0, 2 * p:2 * p + 2].reshape(2 * tm, LANES)
        s_ref[p] = _dot_nt(k_ref[0, p], qp)
    for h in range(MEM_HEADS):
        s = s_ref[h // 2, :, (h % 2) * tm:(h % 2 + 1) * tm]
        m = jnp.max(s, axis=0, keepdims=True)
        e = jnp.exp2(s - m).astype(bf16)
        ot = _dot(vt_ref[0, h], e)
        ot_ref[h * HEAD_DIM:(h + 1) * HEAD_DIM, :] = ot[:HEAD_DIM, :] / ot[HEAD_DIM:HEAD_DIM + 1, :]
    _store_heads_transposed(ot_ref, out_ref)


def _mem_attn_prompt(qmpad, mk2, mvt):
    B, _, T, _ = qmpad.shape
    tm = TOKEN_TILE
    return pl.pallas_call(
        _mem_attn_prompt_kernel,
        grid=(B, T // tm),
        in_specs=[
            pl.BlockSpec((1, MEM_HEADS, tm, LANES), lambda b, i: (b, 0, i, 0)),
            pl.BlockSpec((1, MEM_W // LANES, N_MEM, LANES), lambda b, i: (b, 0, 0, 0)),
            pl.BlockSpec((1, MEM_HEADS, V_ROWS, N_MEM), lambda b, i: (b, 0, 0, 0)),
        ],
        out_specs=pl.BlockSpec((1, tm, MEM_W), lambda b, i: (b, i, 0)),
        out_shape=jax.ShapeDtypeStruct((B, T, MEM_W), bf16),
        scratch_shapes=[pltpu.VMEM((MEM_W, tm), f32), pltpu.VMEM((MEM_W // LANES, N_MEM, 2 * tm), f32)],
        compiler_params=_cparams(2),
        name="mem_attn_prompt",
    )(qmpad, mk2, mvt)


def _top3_bias(gate, eligible, axis):
    idx = lax.broadcasted_iota(jnp.int32, gate.shape, axis)
    n = gate.shape[axis]
    rest = jnp.where(eligible, gate, NEG_INF)
    chosen = jnp.zeros(gate.shape, dtype=jnp.bool_)
    for _ in range(MOBA_TOPK):
        best = jnp.max(rest, axis=axis, keepdims=True)
        first = jnp.min(jnp.where(rest == best, idx, n), axis=axis, keepdims=True)
        pick = (idx == first) & (best > NEG_INF)
        chosen = chosen | pick
        rest = jnp.where(pick, NEG_INF, rest)
    return jnp.where(chosen, 0.0, NEG_INF)


def _moba_gate_kernel(q_ref, km_ref, bias_ref):
    i = pl.program_id(1)
    tm = TOKEN_TILE
    nb = km_ref.shape[1]
    q = q_ref[0]
    km = km_ref[0]
    q_hi = q.astype(bf16)
    q_lo = (q - q_hi.astype(f32)).astype(bf16)
    lane = lax.broadcasted_iota(jnp.int32, (1, LANES), 1)
    lo = lane < HEAD_DIM
    blk = lax.broadcasted_iota(jnp.int32, (nb, 1), 0)
    for h in range(N_KV_HEADS):
        kt = km[:, (h // 2) * LANES:(h // 2 + 1) * LANES]
        gates = []
        for g in range(GROUP):
            hq = h * GROUP + g
            kk = kt if hq % 2 == h % 2 else pltpu.roll(kt, HEAD_DIM, 1)
            kk = jnp.where(lo if hq % 2 == 0 else jnp.logical_not(lo), kk, 0.0)
            kk_hi = kk.astype(bf16)
            kk_lo = (kk - kk_hi.astype(f32)).astype(bf16)
            cols = slice((hq // 2) * LANES, (hq // 2 + 1) * LANES)
            gates.append(_dot_nt(kk_hi, q_hi[:, cols]) + _dot_nt(kk_hi, q_lo[:, cols])
                         + _dot_nt(kk_lo, q_hi[:, cols]))
        gate = jnp.concatenate(gates, axis=1)
        bias_ref[0, 0, h] = _top3_bias(gate, blk < i, 0)


def _moba_gate(q, kmean):
    B, T, _ = q.shape
    tm = TOKEN_TILE
    nb = T // tm
    return pl.pallas_call(
        _moba_gate_kernel,
        grid=(B, nb),
        in_specs=[
            pl.BlockSpec((1, tm, Q_W), lambda b, i: (b, i, 0)),
            pl.BlockSpec((1, nb, KV_W), lambda b, i: (b, 0, 0)),
        ],
        out_specs=pl.BlockSpec((1, 1, N_KV_HEADS, nb, GROUP * tm), lambda b, i: (b, i, 0, 0, 0)),
        out_shape=jax.ShapeDtypeStruct((B, nb, N_KV_HEADS, nb, GROUP * tm), f32),
        compiler_params=_cparams(2),
        name="moba_gate",
    )(q, kmean)


def _moba_prompt_kernel(q_ref, k_ref, vt_ref, bias_ref, out_ref, acc_ref, m_ref, sa_ref, sb_ref):
    i = pl.program_id(1)
    tm = TOKEN_TILE
    nb = k_ref.shape[2]
    n_pairs = KV_W // LANES
    per_pair = N_HEADS // n_pairs
    key = lax.broadcasted_iota(jnp.int32, (tm, tm), 0)
    qc = lax.broadcasted_iota(jnp.int32, (1, tm), 1)
    causal = key <= qc

    def scores_into(j, s_ref):
        for p in range(n_pairs):
            qp = q_ref[0, 2 * p:2 * p + 2].reshape(per_pair * tm, LANES)
            s_ref[p] = _dot_nt(k_ref[0, p, j], qp)

    def chunk(s_ref, c):
        return s_ref[c // per_pair, :, (c % per_pair) * tm:(c % per_pair + 1) * tm]

    scores_into(i, sb_ref)
    scores_into(0, sa_ref)
    for c in range(N_HEADS):
        s = jnp.where(causal, chunk(sb_ref, c), NEG_INF)
        m = jnp.max(s, axis=0, keepdims=True)
        e = jnp.exp2(s - m)
        m_ref[c] = m
        acc_ref[c] = _dot(vt_ref[0, c // GROUP, i], e.astype(bf16))

    def consume(j, s_ref):
        for c in range(N_HEADS):
            h, g = divmod(c, GROUP)
            s = chunk(s_ref, c)
            bias = bias_ref[0, 0, h, pl.ds(j, 1), g * tm:(g + 1) * tm]
            m = m_ref[c]
            m_new = jnp.maximum(m, jnp.max(s, axis=0, keepdims=True) + bias)
            a = jnp.exp2(m - m_new)
            e = jnp.exp2(s - (m_new - bias))
            m_ref[c] = m_new
            acc_ref[c] = a * acc_ref[c] + _dot(vt_ref[0, h, j], e.astype(bf16))

    def body(jj, carry):
        j = 2 * jj
        scores_into(j + 1, sb_ref)
        consume(j, sa_ref)
        scores_into(jnp.minimum(j + 2, nb - 1), sa_ref)
        consume(j + 1, sb_ref)
        return carry

    lax.fori_loop(0, (i + 1) // 2, body, 0)
    heads = [acc_ref[c, :HEAD_DIM, :] / acc_ref[c, HEAD_DIM:HEAD_DIM + 1, :] for c in range(N_HEADS)]
    out_ref[0] = jnp.concatenate(heads, axis=0).T.astype(out_ref.dtype)


def _moba_prompt(qpad, k5, vt5, bias):
    B = qpad.shape[0]
    nb = k5.shape[2]
    tm = TOKEN_TILE
    T = nb * tm
    return pl.pallas_call(
        _moba_prompt_kernel,
        grid=(B, nb),
        in_specs=[
            pl.BlockSpec((1, N_KV_HEADS, GROUP * tm, LANES), lambda b, i: (b, 0, i, 0)),
            pl.BlockSpec((1, KV_W // LANES, nb, tm, LANES), lambda b, i: (b, 0, 0, 0, 0)),
            pl.BlockSpec((1, N_KV_HEADS, nb, V_ROWS, tm), lambda b, i: (b, 0, 0, 0, 0)),
            pl.BlockSpec((1, 1, N_KV_HEADS, nb, GROUP * tm), lambda b, i: (b, i, 0, 0, 0)),
        ],
        out_specs=pl.BlockSpec((1, tm, Q_W), lambda b, i: (b, i, 0)),
        out_shape=jax.ShapeDtypeStruct((B, T, Q_W), bf16),
        scratch_shapes=[pltpu.VMEM((N_HEADS, V_ROWS, tm), f32), pltpu.VMEM((N_HEADS, 1, tm), f32),
                        pltpu.VMEM((KV_W // LANES, tm, N_HEADS * tm * LANES // KV_W), f32),
                        pltpu.VMEM((KV_W // LANES, tm, N_HEADS * tm * LANES // KV_W), f32)],
        compiler_params=_cparams(2),
        name="moba_prompt",
    )(qpad, k5, vt5, bias)


SAMPLE_ROWS = N_HEADS * 4
NEW_PAD = 8
SEQ_GROUP = 8


def _pad_rows(a, n):
    return jnp.concatenate([a, jnp.zeros((n - a.shape[0], a.shape[1]), a.dtype)], axis=0)


def _swa_sample_kernel(q_ref, kc_ref, vc_ref, kn_ref, vn_ref, sink_ref, o_ref, nk_ref, nv_ref):
    wb = kc_ref.shape[2]
    col = lax.broadcasted_iota(jnp.int32, (SAMPLE_ROWS, wb), 1)
    s_idx = lax.broadcasted_iota(jnp.int32, (SAMPLE_ROWS, 1), 0) % 4
    valid_c = col > s_idx
    valid_n = col <= s_idx
    lane = lax.broadcasted_iota(jnp.int32, (1, wb), 1)
    sink = sink_ref[...]
    for n in range(SEQ_GROUP):
        q = q_ref[n]
        kc = kc_ref[n]
        vc = vc_ref[n]
        kn = _pad_rows(kn_ref[n], wb)
        vn = _pad_rows(vn_ref[n], wb)
        s_c = jnp.where(valid_c, _dot(q, kc.astype(bf16)), NEG_INF)
        s_n = jnp.where(valid_n, _dot_nt(q, kn.astype(bf16)), NEG_INF)
        m = jnp.maximum(jnp.maximum(jnp.max(s_c, axis=1, keepdims=True), jnp.max(s_n, axis=1, keepdims=True)), sink)
        e_c = jnp.exp(s_c - m)
        e_n = jnp.exp(s_n - m)
        den = jnp.sum(e_c, axis=1, keepdims=True) + jnp.sum(e_n, axis=1, keepdims=True) + jnp.exp(sink - m)
        o = _dot_nt(e_c.astype(bf16), vc.astype(bf16)) + _dot(e_n.astype(bf16), vn.astype(bf16))
        o_ref[n] = o / den
        nk_ref[n] = jnp.where(lane < wb - 4, pltpu.roll(kc, wb - 4, 1), pltpu.roll(kn.T, wb - 4, 1))
        nv_ref[n] = jnp.where(lane < wb - 4, pltpu.roll(vc, wb - 4, 1), pltpu.roll(vn.T, wb - 4, 1))


def _swa_sample(qbd, kc, vc, kn, vn, sink_rows):
    n_seq, _, wb = kc.shape
    sg = SEQ_GROUP
    seq = lambda r, w: pl.BlockSpec((sg, r, w), lambda i: (i, 0, 0))
    return pl.pallas_call(
        _swa_sample_kernel,
        grid=(n_seq // sg,),
        in_specs=[seq(SAMPLE_ROWS, KV_W), seq(KV_W, wb), seq(KV_W, wb), seq(NEW_PAD, KV_W), seq(NEW_PAD, KV_W),
                  pl.BlockSpec((SAMPLE_ROWS, 1), lambda i: (0, 0))],
        out_specs=[seq(SAMPLE_ROWS, KV_W), seq(KV_W, wb), seq(KV_W, wb)],
        out_shape=[jax.ShapeDtypeStruct((n_seq, SAMPLE_ROWS, KV_W), f32),
                   jax.ShapeDtypeStruct((n_seq, KV_W, wb), f32),
                   jax.ShapeDtypeStruct((n_seq, KV_W, wb), f32)],
        compiler_params=_cparams(1),
        name="swa_sample",
    )(qbd, kc, vc, kn, vn, sink_rows)


def _mem_sample_kernel(q_ref, k_ref, v_ref, o_ref):
    for n in range(SEQ_GROUP):
        s = _dot(q_ref[n], k_ref[0, n].astype(bf16))
        m = jnp.max(s, axis=1, keepdims=True)
        e = jnp.exp(s - m)
        den = jnp.sum(e, axis=1, keepdims=True)
        o_ref[n] = _dot_nt(e.astype(bf16), v_ref[0, n].astype(bf16)) / den


def _mem_sample(qbd, km, vm, layer):
    n_seq = km.shape[1]
    sg = SEQ_GROUP
    rows = qbd.shape[1]
    cache = pl.BlockSpec((1, sg, MEM_W, N_MEM), lambda i: (layer, i, 0, 0))
    return pl.pallas_call(
        _mem_sample_kernel,
        grid=(n_seq // sg,),
        in_specs=[pl.BlockSpec((sg, rows, MEM_W), lambda i: (i, 0, 0)), cache, cache],
        out_specs=pl.BlockSpec((sg, rows, MEM_W), lambda i: (i, 0, 0)),
        out_shape=jax.ShapeDtypeStruct((n_seq, rows, MEM_W), f32),
        compiler_params=_cparams(1),
        name="mem_sample",
    )(qbd, km, vm)


def _moba_sample_kernel(pt_ref, q_ref, qf_ref, kn_ref, vn_ref, expand_ref, kpool_ref, vpool_ref, o_ref,
                        kbuf, vbuf, sem):
    b = pl.program_id(0)
    n_seq = pl.num_programs(0)
    n_pages = pt_ref.shape[1]
    page = kpool_ref.shape[2]
    past = n_pages * page
    nb = past // MOBA_BLOCK
    slot = b % 2

    def page_copy(which, seq, slot_, pg):
        pool, buf = ((kpool_ref, kbuf), (vpool_ref, vbuf))[which]
        return pltpu.make_async_copy(pool.at[pt_ref[seq, pg]], buf.at[slot_, :, pg * page:(pg + 1) * page],
                                     sem.at[which, slot_])

    def fetch(seq, slot_):
        for pg in range(n_pages):
            page_copy(0, seq, slot_, pg).start()
        for pg in range(n_pages):
            page_copy(1, seq, slot_, pg).start()

    def wait(which, seq, slot_):
        for pg in range(n_pages):
            page_copy(which, seq, slot_, pg).wait()

    @pl.when(b == 0)
    def _():
        fetch(0, 0)

    @pl.when(b + 1 < n_seq)
    def _():
        fetch(b + 1, 1 - slot)

    wait(0, b, slot)
    lane = lax.broadcasted_iota(jnp.int32, (1, LANES), 1)
    kmean = jnp.zeros((KV_W, LANES), f32)
    for j in range(nb):
        part = kbuf[slot, :, j * MOBA_BLOCK:j * MOBA_BLOCK + LANES]
        for c in range(1, MOBA_BLOCK // LANES):
            part = part + kbuf[slot, :, j * MOBA_BLOCK + c * LANES:j * MOBA_BLOCK + (c + 1) * LANES]
        kmean = jnp.where(lane == j, jnp.sum(part, axis=1, keepdims=True), kmean)
    kmean = kmean * (1.0 / MOBA_BLOCK)
    gate = jnp.dot(qf_ref[0], kmean, preferred_element_type=f32, precision=lax.Precision.HIGHEST)
    bias = _top3_bias(gate, lane < nb, 1)
    chosen = jnp.where(bias == 0.0, 1.0, 0.0).astype(bf16)
    keep = _dot(chosen, expand_ref[...])
    q = q_ref[0]
    s = jnp.where(keep > 0.5, _dot(q, kbuf[slot].astype(bf16)), NEG_INF)
    col = lax.broadcasted_iota(jnp.int32, (SAMPLE_ROWS, LANES), 1)
    s_idx = lax.broadcasted_iota(jnp.int32, (SAMPLE_ROWS, 1), 0) % 4
    kn = _pad_rows(kn_ref[0], LANES).astype(bf16)
    vn = _pad_rows(vn_ref[0], LANES).astype(bf16)
    s_n = jnp.where(col <= s_idx, _dot_nt(q, kn), NEG_INF)
    m = jnp.maximum(jnp.max(s, axis=1, keepdims=True), jnp.max(s_n, axis=1, keepdims=True))
    e = jnp.exp(s - m)
    e_n = jnp.exp(s_n - m)
    den = jnp.sum(e, axis=1, keepdims=True) + jnp.sum(e_n, axis=1, keepdims=True)
    wait(1, b, slot)
    o = _dot_nt(e.astype(bf16), vbuf[slot].astype(bf16)) + _dot(e_n.astype(bf16), vn)
    o_ref[0] = o / den


def _moba_sample(page_table, qbd, qbd_f32, kn, vn, kpool, vpool):
    n_seq, n_pages = page_table.shape
    page = kpool.shape[2]
    past = n_pages * page
    nb = past // MOBA_BLOCK
    expand = (jnp.arange(past, dtype=jnp.int32)[None, :] // MOBA_BLOCK == jnp.arange(LANES, dtype=jnp.int32)[:, None]).astype(bf16)
    seq = lambda r: pl.BlockSpec((1, r, KV_W), lambda i, pt: (i, 0, 0))
    grid_spec = pltpu.PrefetchScalarGridSpec(
        num_scalar_prefetch=1,
        grid=(n_seq,),
        in_specs=[seq(SAMPLE_ROWS), seq(SAMPLE_ROWS), seq(NEW_PAD), seq(NEW_PAD),
                  pl.BlockSpec((LANES, past), lambda i, pt: (0, 0)),
                  pl.BlockSpec(memory_space=pl.ANY), pl.BlockSpec(memory_space=pl.ANY)],
        out_specs=seq(SAMPLE_ROWS),
        scratch_shapes=[pltpu.VMEM((2, KV_W, past), f32), pltpu.VMEM((2, KV_W, past), f32),
                        pltpu.SemaphoreType.DMA((2, 2))],
    )
    return pl.pallas_call(
        _moba_sample_kernel,
        grid_spec=grid_spec,
        out_shape=jax.ShapeDtypeStruct((n_seq, SAMPLE_ROWS, KV_W), f32),
        compiler_params=_cparams(1),
        name="moba_sample",
    )(page_table, qbd, qbd_f32, kn, vn, expand, kpool, vpool)


def _block_diag_rows(a, n_heads, per_kv):
    n_seq, S, _ = a.shape
    n_kv = n_heads // per_kv
    a = a.reshape(n_seq, S, n_kv, per_kv, HEAD_DIM).transpose(0, 2, 3, 1, 4)
    eye = jnp.eye(n_kv, dtype=a.dtype)
    bd = a[:, :, :, :, None, :] * eye[None, :, None, None, :, None]
    return bd.reshape(n_seq, n_heads * S, n_kv * HEAD_DIM)


def _diag_rows_out(o, n_heads, per_kv, S):
    n_seq = o.shape[0]
    n_kv = n_heads // per_kv
    o = o.reshape(n_seq, n_kv, per_kv, S, n_kv, HEAD_DIM)
    o = jnp.stack([o[:, h, :, :, h, :] for h in range(n_kv)], axis=1)
    return o.transpose(0, 3, 1, 2, 4).reshape(n_seq, S, n_heads * HEAD_DIM)


def _stacked_weights(w_o, ln1_g, ln1_b, w_ffn_gate, conv_w, conv_b, w_ffn_up, w_ffn_down, ln2_g, ln2_b):
    row = lambda a: a[:, None, :]
    return dict(
        w_o=w_o.astype(bf16), g1=row(ln1_g), b1=row(ln1_b),
        w_gate=w_ffn_gate.astype(bf16), conv_w=conv_w, conv_b=row(conv_b),
        w_up=w_ffn_up.astype(bf16), w_down=w_ffn_down.astype(bf16),
        g2=row(ln2_g), b2=row(ln2_b),
    )


def _conv_injection(state):
    n_seq = state.shape[0]
    z = jnp.zeros((n_seq, 1, D_FF), state.dtype)
    inj1 = jnp.concatenate([state[:, 1:2], z, z, z], axis=1)
    inj2 = jnp.concatenate([state[:, 0:1], state[:, 1:2], z, z], axis=1)
    return inj1.reshape(1, n_seq * 4, D_FF), inj2.reshape(1, n_seq * 4, D_FF)


def kernel(x_prompt, x_sample, cache_win_k, cache_win_v, cache_moba_k, cache_moba_v, page_table, cache_mem_k, cache_mem_v, state_conv, mem_prompt, w_in_a, sink_a, w_in_b, w_kv_shared, w_mem_kv, w_o, ln1_g, ln1_b, w_ffn_gate, conv_w, conv_b, w_ffn_up, w_ffn_down, ln2_g, ln2_b):
    B, T, _ = x_prompt.shape
    Bd, S, _ = x_sample.shape
    n_pool, page = cache_moba_k.shape[:2]
    past = page_table.shape[1] * page
    wb = cache_win_k.shape[2]
    nb = T // TOKEN_TILE

    cos_p, sin_p = _rope_tables(jnp.arange(T, dtype=jnp.int32))
    cos_s, sin_s = _rope_tables(past + (jnp.arange(Bd * S, dtype=jnp.int32) % S))
    w_a = w_in_a[0].astype(bf16)
    w_b = jnp.concatenate([w_in_b[0][:, :Q_W], w_kv_shared, w_in_b[0][:, Q_W:]], axis=1).astype(bf16)
    lw = _stacked_weights(w_o, ln1_g, ln1_b, w_ffn_gate, conv_w, conv_b, w_ffn_up, w_ffn_down, ln2_g, ln2_b)
    xs = x_sample.reshape(1, Bd * S, D_MODEL)

    def feature_major(a):
        a = jnp.moveaxis(a, -3, -1)
        return a.reshape(a.shape[:-3] + (a.shape[-3] * HEAD_DIM, a.shape[-1]))

    def token_major(a, n_heads):
        a = a.reshape(a.shape[:-2] + (n_heads, HEAD_DIM, a.shape[-1]))
        return jnp.moveaxis(a, -1, -3)

    mem_kvt, mem_k2, mem_vt = _memkv(mem_prompt, w_mem_kv.astype(bf16))
    mem_k_out = token_major(mem_kvt[:, :, :MEM_W], MEM_HEADS)
    mem_v_out = token_major(mem_kvt[:, :, MEM_W:], MEM_HEADS)
    cache_mem_kt = feature_major(cache_mem_k)
    cache_mem_vt = feature_major(cache_mem_v)

    def sample_rows(a, n_heads, per_kv):
        return _block_diag_rows(a.reshape(Bd, S, n_heads * HEAD_DIM), n_heads, per_kv)

    def pad_new(a):
        return jnp.pad(a.reshape(Bd, S, KV_W), ((0, 0), (0, NEW_PAD - S), (0, 0)))

    def sample_mem(qm, layer):
        qbd = (sample_rows(qm, MEM_HEADS, 1) * SCALE).astype(bf16)
        o = _mem_sample(qbd, cache_mem_kt, cache_mem_vt, layer)
        return _diag_rows_out(o, MEM_HEADS, 1, S).reshape(1, Bd * S, MEM_W).astype(bf16)

    kat, vat, qpad, qmpad, k5, vt5, _ = _proj(x_prompt, w_a, cos_p, sin_p, True, want_q=False)
    sink_cols = jnp.repeat(sink_a[0].reshape(N_KV_HEADS, GROUP) * LOG2E, TOKEN_TILE, axis=1)[:, None, :]
    o_self = _swa_prompt(qpad, k5, vt5, sink_cols)
    o_mem = _mem_attn_prompt(qmpad, mem_k2[0], mem_vt[0])
    xp, tail0 = _merge_ffn(x_prompt, o_self, o_mem, lw, 0)
    win_kp = token_major(kat[:, :, T - wb:], N_KV_HEADS)[None]
    win_vp = token_major(vat[:, :, T - wb:], N_KV_HEADS)[None]

    qs, ks, vs, qms = _proj(xs, w_a, cos_s, sin_s, False)
    qbd = (sample_rows(qs, N_HEADS, GROUP) * SCALE).astype(bf16)
    sink_rows = jnp.repeat(sink_a[0], S)[:, None]
    o_s, win_kst, win_vst = _swa_sample(qbd, feature_major(cache_win_k[0]), feature_major(cache_win_v[0]),
                                        pad_new(ks), pad_new(vs), sink_rows)
    os_self = _diag_rows_out(o_s, N_HEADS, GROUP, S).reshape(1, Bd * S, Q_W).astype(bf16)
    xs, u0 = _merge_ffn(xs, os_self, sample_mem(qms, 0), lw, 0, _conv_injection(state_conv[0]))

    qb, kbt, vbt, qpad, qmpad, k5, vt5, kmean = _proj(xp, w_b, cos_p, sin_p, True)
    bias = _moba_gate(qb, kmean.reshape(B, nb, KV_W))
    o_self = _moba_prompt(qpad, k5, vt5, bias)
    o_mem = _mem_attn_prompt(qmpad, mem_k2[1], mem_vt[1])
    yp, tail1 = _merge_ffn(xp, o_self, o_mem, lw, 1)

    qs, ks, vs, qms = _proj(xs, w_b, cos_s, sin_s, False)
    qbd_f32 = sample_rows(qs, N_HEADS, GROUP)
    o_s = _moba_sample(page_table, (qbd_f32 * SCALE).astype(bf16), qbd_f32, pad_new(ks), pad_new(vs),
                       feature_major(cache_moba_k), feature_major(cache_moba_v))
    os_self = _diag_rows_out(o_s, N_HEADS, GROUP, S).reshape(1, Bd * S, Q_W).astype(bf16)
    ys, u1 = _merge_ffn(xs, os_self, sample_mem(qms, 1), lw, 1, _conv_injection(state_conv[1]))

    kv_heads = lambda a, lead: a.reshape(lead + (N_KV_HEADS, HEAD_DIM))
    conv_p = jnp.stack([tail0[:, 8 - (CONV_W - 1):], tail1[:, 8 - (CONV_W - 1):]])
    conv_s = jnp.stack([u.reshape(Bd, S, D_FF)[:, S - (CONV_W - 1):] for u in (u0, u1)])
    return (
        yp, ys.reshape(Bd, S, D_MODEL),
        win_kp, win_vp,
        token_major(kbt, N_KV_HEADS), token_major(vbt, N_KV_HEADS),
        mem_k_out, mem_v_out,
        conv_p,
        token_major(win_kst, N_KV_HEADS)[None], token_major(win_vst, N_KV_HEADS)[None],
        kv_heads(ks, (Bd, S)), kv_heads(vs, (Bd, S)),
        conv_s,
    )
```

```python
import functools
import math

import jax
import jax.numpy as jnp
from jax import lax
from jax.experimental import pallas as pl
from jax.experimental.pallas import tpu as pltpu

D_MODEL = 1024
HEAD_DIM = 64
N_HEADS = 12
N_KV_HEADS = 4
GROUP = N_HEADS // N_KV_HEADS
MEM_HEADS = 4
N_MEM = 256
Q_W = N_HEADS * HEAD_DIM
KV_W = N_KV_HEADS * HEAD_DIM
MEM_W = MEM_HEADS * HEAD_DIM
PROJ_W = Q_W + 2 * KV_W + MEM_W
WINDOW = 128
MOBA_BLOCK = 256
MOBA_TOPK = 3
D_FF = 2816
CONV_W = 3
ROPE_THETA = 10000.0
LN_EPS = 1e-5
DEPTH = 2
ALPHA = (2 * DEPTH) ** 0.25
SCALE = HEAD_DIM ** -0.5
LOG2E = math.log2(math.e)
SCALE_LOG2 = SCALE * LOG2E

LANES = 128
V_ROWS = HEAD_DIM + 16
TOKEN_TILE = 256
VMEM_LIMIT = 56 * 1024 * 1024
NEG_INF = float("-inf")

f32 = jnp.float32
bf16 = jnp.bfloat16


def _cparams(n_grid, vmem=VMEM_LIMIT, flags=None):
    return pltpu.CompilerParams(dimension_semantics=("arbitrary",) * n_grid, vmem_limit_bytes=vmem, flags=flags)


def _dot_nt(a, b, precision=None):
    return lax.dot_general(a, b, (((1,), (1,)), ((), ())), preferred_element_type=f32, precision=precision)


def _dot(a, b):
    return jnp.dot(a, b, preferred_element_type=f32)


def _layer_norm(r, g, b):
    mu = jnp.mean(r, axis=-1, keepdims=True)
    d = r - mu
    var = jnp.mean(d * d, axis=-1, keepdims=True)
    return d * lax.rsqrt(var + LN_EPS) * g + b


def _ones_row_block(width):
    row = lax.broadcasted_iota(jnp.int32, (V_ROWS - HEAD_DIM, width), 0)
    return jnp.where(row == 0, 1.0, 0.0).astype(f32)


def _half_mask(width):
    lane = lax.broadcasted_iota(jnp.int32, (1, width), 1)
    return (lane % HEAD_DIM) < (HEAD_DIM // 2)


PROJ_SUBTILES = 2


def _proj_kernel(x_ref, w_ref, cos_ref, sin_ref, *rest, layouts, want_q):
    outs = list(rest[:-1])
    y_s = rest[-1]
    q_ref = outs.pop(0) if want_q else None
    tm = TOKEN_TILE
    n_sub = x_ref.shape[1] // tm
    for t in range(n_sub):
        y_s[t] = _dot(x_ref[0, t * tm:(t + 1) * tm, :].astype(bf16), w_ref[...])
    width = Q_W + KV_W
    half = HEAD_DIM // 2
    first_half = _half_mask(width)
    lane = lax.broadcasted_iota(jnp.int32, (1, LANES), 1)
    lo = lane < HEAD_DIM

    def placed(tile, src_half, dst_half):
        t_ = tile if src_half == dst_half else pltpu.roll(tile, HEAD_DIM, 1)
        keep = lo if dst_half == 0 else jnp.logical_not(lo)
        return jnp.where(keep, t_, 0.0)

    for t in range(n_sub):
        rows = slice(t * tm, (t + 1) * tm)
        y = y_s[t]
        qk = y[:, :width]
        cos = jnp.tile(cos_ref[rows, :], (1, width // LANES))
        sin = jnp.tile(sin_ref[rows, :], (1, width // LANES))
        partner = jnp.where(first_half, pltpu.roll(qk, width - half, 1), pltpu.roll(qk, half, 1))
        rot = qk * cos + partner * sin
        q = rot[:, :Q_W]
        k = rot[:, Q_W:]
        v = y[:, Q_W + KV_W:Q_W + 2 * KV_W]
        qm = y[:, Q_W + 2 * KV_W:]
        if want_q:
            q_ref[0, rows, :] = q
        if not layouts:
            k_ref, v_ref, qm_ref = outs
            k_ref[0, rows, :] = k
            v_ref[0, rows, :] = v
            qm_ref[0, rows, :] = qm
            continue
        k_ref, v_ref, qpad_ref, qmpad_ref, k5_ref, vt5_ref, kmean_ref = outs
        kt = k.T
        vt = v.T
        k_ref[0, :, rows] = kt
        v_ref[0, :, rows] = vt
        for h in range(N_KV_HEADS):
            for g in range(GROUP):
                hq = h * GROUP + g
                tile = q[:, (hq // 2) * LANES:(hq // 2 + 1) * LANES]
                r0 = (t * GROUP + g) * tm
                qpad_ref[0, h, r0:r0 + tm, :] = (placed(tile, hq % 2, h % 2) * SCALE_LOG2).astype(bf16)
        for h in range(MEM_HEADS):
            tile = qm[:, (h // 2) * LANES:(h // 2 + 1) * LANES]
            qmpad_ref[0, h, rows, :] = (placed(tile, h % 2, h % 2) * SCALE_LOG2).astype(bf16)
        for p in range(KV_W // LANES):
            k5_ref[0, p, t] = k[:, p * LANES:(p + 1) * LANES].astype(bf16)
        ones_rows = _ones_row_block(tm)
        for h in range(N_KV_HEADS):
            vt5_ref[0, h, t] = jnp.concatenate([vt[h * HEAD_DIM:(h + 1) * HEAD_DIM, :], ones_rows], axis=0).astype(bf16)
        kmean_ref[0, t] = jnp.mean(k, axis=0, keepdims=True)


def _proj(x, w, cos, sin, layouts, want_q=True):
    B, T, _ = x.shape
    ns = PROJ_SUBTILES
    tm = TOKEN_TILE * ns
    nt = T // TOKEN_TILE
    n_tab = cos.shape[0] // tm
    tok = lambda w_: (jax.ShapeDtypeStruct((B, T, w_), f32), pl.BlockSpec((1, tm, w_), lambda b, i: (b, i, 0)))
    feat = lambda w_: (jax.ShapeDtypeStruct((B, w_, T), f32), pl.BlockSpec((1, w_, tm), lambda b, i: (b, 0, i)))
    outs = [tok(Q_W)] if want_q else []
    outs += [feat(KV_W), feat(KV_W)] if layouts else [tok(KV_W), tok(KV_W), tok(MEM_W)]
    out_shape = [o[0] for o in outs]
    out_specs = [o[1] for o in outs]
    if layouts:
        out_shape += [
            jax.ShapeDtypeStruct((B, N_KV_HEADS, GROUP * T, LANES), bf16),
            jax.ShapeDtypeStruct((B, MEM_HEADS, T, LANES), bf16),
            jax.ShapeDtypeStruct((B, KV_W // LANES, nt, TOKEN_TILE, LANES), bf16),
            jax.ShapeDtypeStruct((B, N_KV_HEADS, nt, V_ROWS, TOKEN_TILE), bf16),
            jax.ShapeDtypeStruct((B, nt, 1, KV_W), f32),
        ]
        out_specs += [
            pl.BlockSpec((1, N_KV_HEADS, GROUP * tm, LANES), lambda b, i: (b, 0, i, 0)),
            pl.BlockSpec((1, MEM_HEADS, tm, LANES), lambda b, i: (b, 0, i, 0)),
            pl.BlockSpec((1, KV_W // LANES, ns, TOKEN_TILE, LANES), lambda b, i: (b, 0, i, 0, 0)),
            pl.BlockSpec((1, N_KV_HEADS, ns, V_ROWS, TOKEN_TILE), lambda b, i: (b, 0, i, 0, 0)),
            pl.BlockSpec((1, ns, 1, KV_W), lambda b, i: (b, i, 0, 0)),
        ]
    return pl.pallas_call(
        functools.partial(_proj_kernel, layouts=layouts, want_q=want_q),
        grid=(B, T // tm),
        in_specs=[
            pl.BlockSpec((1, tm, D_MODEL), lambda b, i: (b, i, 0)),
            pl.BlockSpec((D_MODEL, PROJ_W), lambda b, i: (0, 0), pipeline_mode=pl.Buffered(1)),
            pl.BlockSpec((tm, LANES), lambda b, i: (i % n_tab, 0)),
            pl.BlockSpec((tm, LANES), lambda b, i: (i % n_tab, 0)),
        ],
        out_specs=out_specs,
        out_shape=out_shape,
        scratch_shapes=[pltpu.VMEM((ns, TOKEN_TILE, PROJ_W), f32)],
        compiler_params=_cparams(2),
        name="proj_layouts" if layouts else "proj",
    )(x, w, cos, sin)


def _rope_tables(pos):
    half = HEAD_DIM // 2
    inv = jnp.power(jnp.float32(ROPE_THETA), -jnp.arange(half, dtype=f32) / half)
    ang = pos.astype(f32)[:, None] * inv[None, :]
    cos = jnp.cos(ang)
    sin = jnp.sin(ang)
    reps = LANES // HEAD_DIM
    return jnp.tile(jnp.concatenate([cos, cos], axis=1), (1, reps)), jnp.tile(jnp.concatenate([-sin, sin], axis=1), (1, reps))


def _memkv_kernel(m_ref, w_ref, kv_ref, k2_ref, vt_ref):
    y = _dot(m_ref[0].astype(bf16), w_ref[0])
    yt = y.T
    kv_ref[0, 0] = yt
    for p in range(MEM_W // LANES):
        k2_ref[0, 0, p] = y[:, p * LANES:(p + 1) * LANES].astype(bf16)
    ones_rows = _ones_row_block(N_MEM)
    for h in range(MEM_HEADS):
        vh = yt[MEM_W + h * HEAD_DIM:MEM_W + (h + 1) * HEAD_DIM, :]
        vt_ref[0, 0, h] = jnp.concatenate([vh, ones_rows], axis=0).astype(bf16)


def _memkv(mem, w):
    B = mem.shape[0]
    return pl.pallas_call(
        _memkv_kernel,
        grid=(DEPTH, B),
        in_specs=[
            pl.BlockSpec((1, N_MEM, D_MODEL), lambda l, b: (b, 0, 0)),
            pl.BlockSpec((1, D_MODEL, 2 * MEM_W), lambda l, b: (l, 0, 0)),
        ],
        out_specs=[
            pl.BlockSpec((1, 1, 2 * MEM_W, N_MEM), lambda l, b: (l, b, 0, 0)),
            pl.BlockSpec((1, 1, MEM_W // LANES, N_MEM, LANES), lambda l, b: (l, b, 0, 0, 0)),
            pl.BlockSpec((1, 1, MEM_HEADS, V_ROWS, N_MEM), lambda l, b: (l, b, 0, 0, 0)),
        ],
        out_shape=[
            jax.ShapeDtypeStruct((DEPTH, B, 2 * MEM_W, N_MEM), f32),
            jax.ShapeDtypeStruct((DEPTH, B, MEM_W // LANES, N_MEM, LANES), bf16),
            jax.ShapeDtypeStruct((DEPTH, B, MEM_HEADS, V_ROWS, N_MEM), bf16),
        ],
        compiler_params=_cparams(2),
        name="memkv",
    )(mem, w)


def _gelu(c):
    return 0.5 * c * (1.0 + lax.erf(c * (1.0 / math.sqrt(2.0))))


def _merge_ffn_kernel(x_ref, os_ref, om_ref, wos_ref, wom_ref, g1_ref, b1_ref, wg_ref, cw_ref, cb_ref, wu_ref,
                      wd_ref, g2_ref, b2_ref, *rest, sample):
    if sample:
        inj_ref, y_ref, u_ref, x1_s, u_s, up_s = rest
    else:
        y_ref, tail_ref, carry_ref, x1_s, u_s, up_s = rest
        @pl.when(pl.program_id(1) == 0)
        def _():
            carry_ref[...] = jnp.zeros_like(carry_ref)
    tm = TOKEN_TILE
    n_sub = x_ref.shape[1] // tm
    for t in range(n_sub):
        rows = slice(t * tm, (t + 1) * tm)
        attn = _dot(os_ref[0, rows, :], wos_ref[0]) + _dot(om_ref[0, rows, :], wom_ref[0])
        x1 = _layer_norm(ALPHA * x_ref[0, rows, :] + attn, g1_ref[0], b1_ref[0])
        x1_s[t] = x1
        x1b = x1.astype(bf16)
        u_s[t] = _dot(x1b, wg_ref[0])
        up_s[t] = _dot(x1b, wu_ref[0])
    row = lax.broadcasted_iota(jnp.int32, (tm, 1), 0)
    cw = cw_ref[0]
    for t in range(n_sub):
        rows = slice(t * tm, (t + 1) * tm)
        u = u_s[t]
        r1 = pltpu.roll(u, 1, 0)
        r2 = pltpu.roll(u, 2, 0)
        if sample:
            s = row % 4
            inj = inj_ref[0, rows, :]
            prev1 = jnp.where(s >= 1, r1, pltpu.roll(inj, tm - 1, 0))
            prev2 = jnp.where(s >= 2, r2, inj)
            u_ref[0, rows, :] = u
        else:
            c6 = carry_ref[6:7, :]
            c7 = carry_ref[7:8, :]
            prev1 = jnp.where(row == 0, c7, r1)
            prev2 = jnp.where(row == 0, c6, jnp.where(row == 1, c7, r2))
            carry_ref[...] = u[tm - 8:, :]
            if t == n_sub - 1:
                tail_ref[0] = u[tm - 8:, :]
        c = cb_ref[0] + prev2 * cw[0:1, :] + prev1 * cw[1:2, :] + u * cw[2:3, :]
        hid = (_gelu(c) * up_s[t]).astype(bf16)
        y_ref[0, rows, :] = _layer_norm(ALPHA * x1_s[t] + _dot(hid, wd_ref[0]), g2_ref[0], b2_ref[0])


FFN_SUBTILES = 2


def _merge_ffn(x, o_self, o_mem, lw, layer, inj=None):
    B, T, _ = x.shape
    tm = TOKEN_TILE * FFN_SUBTILES
    nt = T // tm
    sample = inj is not None
    tok = lambda w: pl.BlockSpec((1, tm, w), lambda b, i: (b, i, 0))
    const = lambda r, c, rb=0: pl.BlockSpec((1, r, c), lambda b, i: (layer, rb, 0), pipeline_mode=pl.Buffered(1))
    in_specs = [
        tok(D_MODEL), tok(Q_W), tok(MEM_W),
        const(Q_W, D_MODEL), const(MEM_W, D_MODEL, Q_W // MEM_W), const(1, D_MODEL), const(1, D_MODEL),
        const(D_MODEL, D_FF), const(CONV_W, D_FF), const(1, D_FF), const(D_MODEL, D_FF),
        const(D_FF, D_MODEL), const(1, D_MODEL), const(1, D_MODEL),
    ]
    args = [x, o_self, o_mem, lw["w_o"], lw["w_o"], lw["g1"], lw["b1"], lw["w_gate"], lw["conv_w"],
            lw["conv_b"], lw["w_up"], lw["w_down"], lw["g2"], lw["b2"]]
    if sample:
        in_specs += [tok(D_FF)]
        args += [inj]
        out_specs = [tok(D_MODEL), tok(D_FF)]
        out_shape = [jax.ShapeDtypeStruct((B, T, D_MODEL), f32), jax.ShapeDtypeStruct((B, T, D_FF), f32)]
        scratch = []
    else:
        out_specs = [tok(D_MODEL), pl.BlockSpec((1, 8, D_FF), lambda b, i: (b, 0, 0))]
        out_shape = [jax.ShapeDtypeStruct((B, T, D_MODEL), f32), jax.ShapeDtypeStruct((B, 8, D_FF), f32)]
        scratch = [pltpu.VMEM((8, D_FF), f32)]
    scratch += [pltpu.VMEM((FFN_SUBTILES, TOKEN_TILE, D_MODEL), f32), pltpu.VMEM((FFN_SUBTILES, TOKEN_TILE, D_FF), f32),
                pltpu.VMEM((FFN_SUBTILES, TOKEN_TILE, D_FF), f32)]
    return pl.pallas_call(
        functools.partial(_merge_ffn_kernel, sample=sample),
        grid=(B, nt),
        in_specs=in_specs,
        out_specs=out_specs,
        out_shape=out_shape,
        scratch_shapes=scratch,
        compiler_params=_cparams(2),
        name="merge_ffn_sample" if sample else "merge_ffn",
    )(*args)


def _store_heads_transposed(ot_ref, out_ref):
    out_ref[0] = ot_ref[...].T.astype(out_ref.dtype)


def _swa_prompt_kernel(q_ref, kc_ref, kp_ref, vc_ref, vp_ref, sink_ref, out_ref, ot_ref, sc_ref, sp_ref):
    i = pl.program_id(1)
    tm = TOKEN_TILE
    n_pairs = KV_W // LANES
    per_pair = N_HEADS // n_pairs
    key_c = lax.broadcasted_iota(jnp.int32, (tm, tm), 0)
    key_p = lax.broadcasted_iota(jnp.int32, (WINDOW, tm), 0)
    qc = lax.broadcasted_iota(jnp.int32, (1, tm), 1)
    valid_c = (key_c <= qc) & (key_c > qc - WINDOW)
    valid_p = (key_p > qc) & (i > 0)
    for p in range(n_pairs):
        qp = q_ref[0, 2 * p:2 * p + 2].reshape(per_pair * tm, LANES)
        sc_ref[p] = _dot_nt(kc_ref[0, p, 0], qp)
        sp_ref[p] = _dot_nt(kp_ref[0, p, 0], qp)
    for c in range(N_HEADS):
        h, g = divmod(c, GROUP)
        cols = slice((c % per_pair) * tm, (c % per_pair + 1) * tm)
        s_c = jnp.where(valid_c, sc_ref[c // per_pair, :, cols], NEG_INF)
        s_p = jnp.where(valid_p, sp_ref[c // per_pair, :, cols], NEG_INF)
        sink = sink_ref[h, :, g * tm:(g + 1) * tm]
        m = jnp.maximum(jnp.maximum(jnp.max(s_c, axis=0, keepdims=True), jnp.max(s_p, axis=0, keepdims=True)), sink)
        e_c = jnp.exp2(s_c - m)
        e_p = jnp.exp2(s_p - m)
        ot = _dot(vc_ref[0, h, 0], e_c.astype(bf16)) + _dot(vp_ref[0, h, 0], e_p.astype(bf16))
        den = ot[HEAD_DIM:HEAD_DIM + 1, :] + jnp.exp2(sink - m)
        ot_ref[c * HEAD_DIM:(c + 1) * HEAD_DIM, :] = ot[:HEAD_DIM, :] / den
    _store_heads_transposed(ot_ref, out_ref)


def _swa_prompt(qpad, k5, vt5, sink_cols):
    B = qpad.shape[0]
    nt = k5.shape[2]
    tm = TOKEN_TILE
    T = nt * tm
    prev = lambda i: jnp.maximum(i - 1, 0)
    return pl.pallas_call(
        _swa_prompt_kernel,
        grid=(B, nt),
        in_specs=[
            pl.BlockSpec((1, N_KV_HEADS, GROUP * tm, LANES), lambda b, i: (b, 0, i, 0)),
            pl.BlockSpec((1, KV_W // LANES, 1, tm, LANES), lambda b, i: (b, 0, i, 0, 0)),
            pl.BlockSpec((1, KV_W // LANES, 1, WINDOW, LANES), lambda b, i: (b, 0, prev(i), tm // WINDOW - 1, 0)),
            pl.BlockSpec((1, N_KV_HEADS, 1, V_ROWS, tm), lambda b, i: (b, 0, i, 0, 0)),
            pl.BlockSpec((1, N_KV_HEADS, 1, V_ROWS, WINDOW), lambda b, i: (b, 0, prev(i), 0, tm // WINDOW - 1)),
            pl.BlockSpec((N_KV_HEADS, 1, GROUP * tm), lambda b, i: (0, 0, 0)),
        ],
        out_specs=pl.BlockSpec((1, tm, Q_W), lambda b, i: (b, i, 0)),
        out_shape=jax.ShapeDtypeStruct((B, T, Q_W), bf16),
        scratch_shapes=[pltpu.VMEM((Q_W, tm), f32),
                        pltpu.VMEM((KV_W // LANES, tm, N_HEADS * tm * LANES // KV_W), f32),
                        pltpu.VMEM((KV_W // LANES, WINDOW, N_HEADS * tm * LANES // KV_W), f32)],
        compiler_params=_cparams(2),
        name="swa_prompt",
    )(qpad, k5, k5, vt5, vt5, sink_cols)


def _mem_attn_prompt_kernel(q_ref, k_ref, vt_ref, out_ref, ot_ref, s_ref):
    tm = q_ref.shape[2]
    for p in range(MEM_W // LANES):
        qp = q_ref[0, 2 * p:2 * p + 2].reshape(2 * tm, LANES)
        s_ref[p] = _dot_nt(k_ref[0, p], qp)
    for h in range(MEM_HEADS):
        s = s_ref[h // 2, :, (h % 2) * tm:(h % 2 + 1) * tm]
        m = jnp.max(s, axis=0, keepdims=True)
        e = jnp.exp2(s - m).astype(bf16)
        ot = _dot(vt_ref[0, h], e)
        ot_ref[h * HEAD_DIM:(h + 1) * HEAD_DIM, :] = ot[:HEAD_DIM, :] / ot[HEAD_DIM:HEAD_DIM + 1, :]
    _store_heads_transposed(ot_ref, out_ref)


def _mem_attn_prompt(qmpad, mk2, mvt):
    B, _, T, _ = qmpad.shape
    tm = TOKEN_TILE
    return pl.pallas_call(
        _mem_attn_prompt_kernel,
        grid=(B, T // tm),
        in_specs=[
            pl.BlockSpec((1, MEM_HEADS, tm, LANES), lambda b, i: (b, 0, i, 0)),
            pl.BlockSpec((1, MEM_W // LANES, N_MEM, LANES), lambda b, i: (b, 0, 0, 0)),
            pl.BlockSpec((1, MEM_HEADS, V_ROWS, N_MEM), lambda b, i: (b, 0, 0, 0)),
        ],
        out_specs=pl.BlockSpec((1, tm, MEM_W), lambda b, i: (b, i, 0)),
        out_shape=jax.ShapeDtypeStruct((B, T, MEM_W), bf16),
        scratch_shapes=[pltpu.VMEM((MEM_W, tm), f32), pltpu.VMEM((MEM_W // LANES, N_MEM, 2 * tm), f32)],
        compiler_params=_cparams(2),
        name="mem_attn_prompt",
    )(qmpad, mk2, mvt)


def _top3_bias(gate, eligible, axis):
    idx = lax.broadcasted_iota(jnp.int32, gate.shape, axis)
    n = gate.shape[axis]
    rest = jnp.where(eligible, gate, NEG_INF)
    chosen = jnp.zeros(gate.shape, dtype=jnp.bool_)
    for _ in range(MOBA_TOPK):
        best = jnp.max(rest, axis=axis, keepdims=True)
        first = jnp.min(jnp.where(rest == best, idx, n), axis=axis, keepdims=True)
        pick = (idx == first) & (best > NEG_INF)
        chosen = chosen | pick
        rest = jnp.where(pick, NEG_INF, rest)
    return jnp.where(chosen, 0.0, NEG_INF)


def _moba_gate_kernel(q_ref, km_ref, bias_ref):
    i = pl.program_id(1)
    tm = TOKEN_TILE
    nb = km_ref.shape[1]
    q = q_ref[0]
    km = km_ref[0]
    q_hi = q.astype(bf16)
    q_lo = (q - q_hi.astype(f32)).astype(bf16)
    lane = lax.broadcasted_iota(jnp.int32, (1, LANES), 1)
    lo = lane < HEAD_DIM
    blk = lax.broadcasted_iota(jnp.int32, (nb, 1), 0)
    for h in range(N_KV_HEADS):
        kt = km[:, (h // 2) * LANES:(h // 2 + 1) * LANES]
        gates = []
        for g in range(GROUP):
            hq = h * GROUP + g
            kk = kt if hq % 2 == h % 2 else pltpu.roll(kt, HEAD_DIM, 1)
            kk = jnp.where(lo if hq % 2 == 0 else jnp.logical_not(lo), kk, 0.0)
            kk_hi = kk.astype(bf16)
            kk_lo = (kk - kk_hi.astype(f32)).astype(bf16)
            cols = slice((hq // 2) * LANES, (hq // 2 + 1) * LANES)
            gates.append(_dot_nt(kk_hi, q_hi[:, cols]) + _dot_nt(kk_hi, q_lo[:, cols])
                         + _dot_nt(kk_lo, q_hi[:, cols]))
        gate = jnp.concatenate(gates, axis=1)
        bias_ref[0, 0, h] = _top3_bias(gate, blk < i, 0)


def _moba_gate(q, kmean):
    B, T, _ = q.shape
    tm = TOKEN_TILE
    nb = T // tm
    return pl.pallas_call(
        _moba_gate_kernel,
        grid=(B, nb),
        in_specs=[
            pl.BlockSpec((1, tm, Q_W), lambda b, i: (b, i, 0)),
            pl.BlockSpec((1, nb, KV_W), lambda b, i: (b, 0, 0)),
        ],
        out_specs=pl.BlockSpec((1, 1, N_KV_HEADS, nb, GROUP * tm), lambda b, i: (b, i, 0, 0, 0)),
        out_shape=jax.ShapeDtypeStruct((B, nb, N_KV_HEADS, nb, GROUP * tm), f32),
        compiler_params=_cparams(2),
        name="moba_gate",
    )(q, kmean)


def _moba_prompt_kernel(q_ref, k_ref, vt_ref, bias_ref, out_ref, acc_ref, m_ref, sa_ref, sb_ref):
    i = pl.program_id(1)
    tm = TOKEN_TILE
    nb = k_ref.shape[2]
    n_pairs = KV_W // LANES
    per_pair = N_HEADS // n_pairs
    key = lax.broadcasted_iota(jnp.int32, (tm, tm), 0)
    qc = lax.broadcasted_iota(jnp.int32, (1, tm), 1)
    causal = key <= qc

    def scores_into(j, s_ref):
        for p in range(n_pairs):
            qp = q_ref[0, 2 * p:2 * p + 2].reshape(per_pair * tm, LANES)
            s_ref[p] = _dot_nt(k_ref[0, p, j], qp)

    def chunk(s_ref, c):
        return s_ref[c // per_pair, :, (c % per_pair) * tm:(c % per_pair + 1) * tm]

    scores_into(i, sb_ref)
    scores_into(0, sa_ref)
    for c in range(N_HEADS):
        s = jnp.where(causal, chunk(sb_ref, c), NEG_INF)
        m = jnp.max(s, axis=0, keepdims=True)
        e = jnp.exp2(s - m)
        m_ref[c] = m
        acc_ref[c] = _dot(vt_ref[0, c // GROUP, i], e.astype(bf16))

    def consume(j, s_ref):
        for c in range(N_HEADS):
            h, g = divmod(c, GROUP)
            s = chunk(s_ref, c)
            bias = bias_ref[0, 0, h, pl.ds(j, 1), g * tm:(g + 1) * tm]
            m = m_ref[c]
            m_new = jnp.maximum(m, jnp.max(s, axis=0, keepdims=True) + bias)
            a = jnp.exp2(m - m_new)
            e = jnp.exp2(s - (m_new - bias))
            m_ref[c] = m_new
            acc_ref[c] = a * acc_ref[c] + _dot(vt_ref[0, h, j], e.astype(bf16))

    def body(jj, carry):
        j = 2 * jj
        scores_into(j + 1, sb_ref)
        consume(j, sa_ref)
        scores_into(jnp.minimum(j + 2, nb - 1), sa_ref)
        consume(j + 1, sb_ref)
        return carry

    lax.fori_loop(0, (i + 1) // 2, body, 0)
    heads = [acc_ref[c, :HEAD_DIM, :] / acc_ref[c, HEAD_DIM:HEAD_DIM + 1, :] for c in range(N_HEADS)]
    out_ref[0] = jnp.concatenate(heads, axis=0).T.astype(out_ref.dtype)


def _moba_prompt(qpad, k5, vt5, bias):
    B = qpad.shape[0]
    nb = k5.shape[2]
    tm = TOKEN_TILE
    T = nb * tm
    return pl.pallas_call(
        _moba_prompt_kernel,
        grid=(B, nb),
        in_specs=[
            pl.BlockSpec((1, N_KV_HEADS, GROUP * tm, LANES), lambda b, i: (b, 0, i, 0)),
            pl.BlockSpec((1, KV_W // LANES, nb, tm, LANES), lambda b, i: (b, 0, 0, 0, 0)),
            pl.BlockSpec((1, N_KV_HEADS, nb, V_ROWS, tm), lambda b, i: (b, 0, 0, 0, 0)),
            pl.BlockSpec((1, 1, N_KV_HEADS, nb, GROUP * tm), lambda b, i: (b, i, 0, 0, 0)),
        ],
        out_specs=pl.BlockSpec((1, tm, Q_W), lambda b, i: (b, i, 0)),
        out_shape=jax.ShapeDtypeStruct((B, T, Q_W), bf16),
        scratch_shapes=[pltpu.VMEM((N_HEADS, V_ROWS, tm), f32), pltpu.VMEM((N_HEADS, 1, tm), f32),
                        pltpu.VMEM((KV_W // LANES, tm, N_HEADS * tm * LANES // KV_W), f32),
                        pltpu.VMEM((KV_W // LANES, tm, N_HEADS * tm * LANES // KV_W), f32)],
        compiler_params=_cparams(2),
        name="moba_prompt",
    )(qpad, k5, vt5, bias)


SAMPLE_ROWS = N_HEADS * 4
NEW_PAD = 8
SEQ_GROUP = 8


def _pad_rows(a, n):
    return jnp.concatenate([a, jnp.zeros((n - a.shape[0], a.shape[1]), a.dtype)], axis=0)


def _swa_sample_kernel(q_ref, kc_ref, vc_ref, kn_ref, vn_ref, sink_ref, o_ref, nk_ref, nv_ref, s_ref):
    wb = kc_ref.shape[2]
    col = lax.broadcasted_iota(jnp.int32, (SAMPLE_ROWS, wb), 1)
    s_idx = lax.broadcasted_iota(jnp.int32, (SAMPLE_ROWS, 1), 0) % 4
    valid_c = col > s_idx
    valid_n = col <= s_idx
    lane = lax.broadcasted_iota(jnp.int32, (1, wb), 1)
    sink = sink_ref[...]
    for n in range(SEQ_GROUP):
        q = q_ref[n]
        kn = _pad_rows(kn_ref[n], wb)
        s_ref[n, :, :wb] = _dot(q, kc_ref[n].astype(bf16))
        s_ref[n, :, wb:] = _dot_nt(q, kn.astype(bf16))
    for n in range(SEQ_GROUP):
        vn = _pad_rows(vn_ref[n], wb)
        s_c = jnp.where(valid_c, s_ref[n, :, :wb], NEG_INF)
        s_n = jnp.where(valid_n, s_ref[n, :, wb:], NEG_INF)
        m = jnp.maximum(jnp.maximum(jnp.max(s_c, axis=1, keepdims=True), jnp.max(s_n, axis=1, keepdims=True)), sink)
        e_c = jnp.exp(s_c - m)
        e_n = jnp.exp(s_n - m)
        den = jnp.sum(e_c, axis=1, keepdims=True) + jnp.sum(e_n, axis=1, keepdims=True) + jnp.exp(sink - m)
        o = _dot_nt(e_c.astype(bf16), vc_ref[n].astype(bf16)) + _dot(e_n.astype(bf16), vn.astype(bf16))
        o_ref[n] = o / den
    for n in range(SEQ_GROUP):
        kn = _pad_rows(kn_ref[n], wb)
        vn = _pad_rows(vn_ref[n], wb)
        nk_ref[n] = jnp.where(lane < wb - 4, pltpu.roll(kc_ref[n], wb - 4, 1), pltpu.roll(kn.T, wb - 4, 1))
        nv_ref[n] = jnp.where(lane < wb - 4, pltpu.roll(vc_ref[n], wb - 4, 1), pltpu.roll(vn.T, wb - 4, 1))


def _swa_sample(qbd, kc, vc, kn, vn, sink_rows):
    n_seq, _, wb = kc.shape
    sg = SEQ_GROUP
    seq = lambda r, w: pl.BlockSpec((sg, r, w), lambda i: (i, 0, 0))
    return pl.pallas_call(
        _swa_sample_kernel,
        grid=(n_seq // sg,),
        in_specs=[seq(SAMPLE_ROWS, KV_W), seq(KV_W, wb), seq(KV_W, wb), seq(NEW_PAD, KV_W), seq(NEW_PAD, KV_W),
                  pl.BlockSpec((SAMPLE_ROWS, 1), lambda i: (0, 0))],
        out_specs=[seq(SAMPLE_ROWS, KV_W), seq(KV_W, wb), seq(KV_W, wb)],
        out_shape=[jax.ShapeDtypeStruct((n_seq, SAMPLE_ROWS, KV_W), f32),
                   jax.ShapeDtypeStruct((n_seq, KV_W, wb), f32),
                   jax.ShapeDtypeStruct((n_seq, KV_W, wb), f32)],
        scratch_shapes=[pltpu.VMEM((sg, SAMPLE_ROWS, 2 * wb), f32)],
        compiler_params=_cparams(1),
        name="swa_sample",
    )(qbd, kc, vc, kn, vn, sink_rows)


def _mem_sample_kernel(q_ref, k_ref, v_ref, o_ref):
    for n in range(SEQ_GROUP):
        s = _dot(q_ref[n], k_ref[0, n].astype(bf16))
        m = jnp.max(s, axis=1, keepdims=True)
        e = jnp.exp(s - m)
        den = jnp.sum(e, axis=1, keepdims=True)
        o_ref[n] = _dot_nt(e.astype(bf16), v_ref[0, n].astype(bf16)) / den


def _mem_sample(qbd, km, vm, layer):
    n_seq = km.shape[1]
    sg = SEQ_GROUP
    rows = qbd.shape[1]
    cache = pl.BlockSpec((1, sg, MEM_W, N_MEM), lambda i: (layer, i, 0, 0))
    return pl.pallas_call(
        _mem_sample_kernel,
        grid=(n_seq // sg,),
        in_specs=[pl.BlockSpec((sg, rows, MEM_W), lambda i: (i, 0, 0)), cache, cache],
        out_specs=pl.BlockSpec((sg, rows, MEM_W), lambda i: (i, 0, 0)),
        out_shape=jax.ShapeDtypeStruct((n_seq, rows, MEM_W), f32),
        compiler_params=_cparams(1),
        name="mem_sample",
    )(qbd, km, vm)


def _moba_sample_kernel(pt_ref, q_ref, qf_ref, kn_ref, vn_ref, kpool_ref, vpool_ref, o_ref,
                        kbuf, vbuf, sem, s_ref, e_ref):
    b = pl.program_id(0)
    n_seq = pl.num_programs(0)
    n_pages = pt_ref.shape[1]
    page = kpool_ref.shape[2]
    past = n_pages * page
    nb = past // MOBA_BLOCK
    slot = b % 2

    def page_copy(which, seq, slot_, pg):
        pool, buf = ((kpool_ref, kbuf), (vpool_ref, vbuf))[which]
        return pltpu.make_async_copy(pool.at[pt_ref[seq, pg]], buf.at[slot_, :, pg * page:(pg + 1) * page],
                                     sem.at[which, slot_])

    def fetch(seq, slot_):
        for pg in range(n_pages):
            page_copy(0, seq, slot_, pg).start()
        for pg in range(n_pages):
            page_copy(1, seq, slot_, pg).start()

    def wait(which, seq, slot_):
        for pg in range(n_pages):
            page_copy(which, seq, slot_, pg).wait()

    @pl.when(b == 0)
    def _():
        fetch(0, 0)

    @pl.when(b + 1 < n_seq)
    def _():
        fetch(b + 1, 1 - slot)

    wait(0, b, slot)
    q = q_ref[0]
    s_ref[...] = _dot(q, kbuf[slot].astype(bf16))
    lane = lax.broadcasted_iota(jnp.int32, (1, LANES), 1)
    kmean = jnp.zeros((KV_W, LANES), f32)
    for j in range(nb):
        part = kbuf[slot, :, j * MOBA_BLOCK:j * MOBA_BLOCK + LANES]
        for c in range(1, MOBA_BLOCK // LANES):
            part = part + kbuf[slot, :, j * MOBA_BLOCK + c * LANES:j * MOBA_BLOCK + (c + 1) * LANES]
        kmean = jnp.where(lane == j, jnp.sum(part, axis=1, keepdims=True), kmean)
    kmean = kmean * (1.0 / MOBA_BLOCK)
    gate = jnp.dot(qf_ref[0], kmean, preferred_element_type=f32, precision=lax.Precision.HIGHEST)
    bias = _top3_bias(gate, lane < nb, 1)
    col = lax.broadcasted_iota(jnp.int32, (SAMPLE_ROWS, LANES), 1)
    s_idx = lax.broadcasted_iota(jnp.int32, (SAMPLE_ROWS, 1), 0) % 4
    kn = _pad_rows(kn_ref[0], LANES).astype(bf16)
    vn = _pad_rows(vn_ref[0], LANES).astype(bf16)
    s_n = jnp.where(col <= s_idx, _dot_nt(q, kn), NEG_INF)
    mx = jnp.full((SAMPLE_ROWS, MOBA_BLOCK), NEG_INF, f32)
    for j in range(nb):
        mx = jnp.maximum(mx, s_ref[:, j * MOBA_BLOCK:(j + 1) * MOBA_BLOCK] + bias[:, j:j + 1])
    m = jnp.maximum(jnp.max(mx, axis=1, keepdims=True), jnp.max(s_n, axis=1, keepdims=True))
    acc = jnp.zeros((SAMPLE_ROWS, MOBA_BLOCK), f32)
    for j in range(nb):
        e = jnp.exp(s_ref[:, j * MOBA_BLOCK:(j + 1) * MOBA_BLOCK] - (m - bias[:, j:j + 1]))
        acc = acc + e
        e_ref[:, j * MOBA_BLOCK:(j + 1) * MOBA_BLOCK] = e.astype(bf16)
    e_n = jnp.exp(s_n - m)
    den = jnp.sum(acc, axis=1, keepdims=True) + jnp.sum(e_n, axis=1, keepdims=True)
    wait(1, b, slot)
    o = _dot_nt(e_ref[...], vbuf[slot].astype(bf16)) + _dot(e_n.astype(bf16), vn)
    o_ref[0] = o / den


def _moba_sample(page_table, qbd, qbd_f32, kn, vn, kpool, vpool):
    n_seq, n_pages = page_table.shape
    page = kpool.shape[2]
    past = n_pages * page
    seq = lambda r: pl.BlockSpec((1, r, KV_W), lambda i, pt: (i, 0, 0))
    grid_spec = pltpu.PrefetchScalarGridSpec(
        num_scalar_prefetch=1,
        grid=(n_seq,),
        in_specs=[seq(SAMPLE_ROWS), seq(SAMPLE_ROWS), seq(NEW_PAD), seq(NEW_PAD),
                  pl.BlockSpec(memory_space=pl.ANY), pl.BlockSpec(memory_space=pl.ANY)],
        out_specs=seq(SAMPLE_ROWS),
        scratch_shapes=[pltpu.VMEM((2, KV_W, past), f32), pltpu.VMEM((2, KV_W, past), f32),
                        pltpu.SemaphoreType.DMA((2, 2)),
                        pltpu.VMEM((SAMPLE_ROWS, past), f32), pltpu.VMEM((SAMPLE_ROWS, past), bf16)],
    )
    return pl.pallas_call(
        _moba_sample_kernel,
        grid_spec=grid_spec,
        out_shape=jax.ShapeDtypeStruct((n_seq, SAMPLE_ROWS, KV_W), f32),
        compiler_params=_cparams(1),
        name="moba_sample",
    )(page_table, qbd, qbd_f32, kn, vn, kpool, vpool)


def _block_diag_rows(a, n_heads, per_kv):
    n_seq, S, _ = a.shape
    n_kv = n_heads // per_kv
    a = a.reshape(n_seq, S, n_kv, per_kv, HEAD_DIM).transpose(0, 2, 3, 1, 4)
    eye = jnp.eye(n_kv, dtype=a.dtype)
    bd = a[:, :, :, :, None, :] * eye[None, :, None, None, :, None]
    return bd.reshape(n_seq, n_heads * S, n_kv * HEAD_DIM)


def _diag_rows_out(o, n_heads, per_kv, S):
    n_seq = o.shape[0]
    n_kv = n_heads // per_kv
    o = o.reshape(n_seq, n_kv, per_kv, S, n_kv, HEAD_DIM)
    o = jnp.stack([o[:, h, :, :, h, :] for h in range(n_kv)], axis=1)
    return o.transpose(0, 3, 1, 2, 4).reshape(n_seq, S, n_heads * HEAD_DIM)


def _stacked_weights(w_o, ln1_g, ln1_b, w_ffn_gate, conv_w, conv_b, w_ffn_up, w_ffn_down, ln2_g, ln2_b):
    row = lambda a: a[:, None, :]
    return dict(
        w_o=w_o.astype(bf16), g1=row(ln1_g), b1=row(ln1_b),
        w_gate=w_ffn_gate.astype(bf16), conv_w=conv_w, conv_b=row(conv_b),
        w_up=w_ffn_up.astype(bf16), w_down=w_ffn_down.astype(bf16),
        g2=row(ln2_g), b2=row(ln2_b),
    )


def _conv_injection(state):
    n_seq = state.shape[0]
    return jnp.pad(state, ((0, 0), (0, 2), (0, 0))).reshape(1, n_seq * 4, D_FF)


def kernel(x_prompt, x_sample, cache_win_k, cache_win_v, cache_moba_k, cache_moba_v, page_table, cache_mem_k, cache_mem_v, state_conv, mem_prompt, w_in_a, sink_a, w_in_b, w_kv_shared, w_mem_kv, w_o, ln1_g, ln1_b, w_ffn_gate, conv_w, conv_b, w_ffn_up, w_ffn_down, ln2_g, ln2_b):
    B, T, _ = x_prompt.shape
    Bd, S, _ = x_sample.shape
    n_pool, page = cache_moba_k.shape[:2]
    past = page_table.shape[1] * page
    wb = cache_win_k.shape[2]
    nb = T // TOKEN_TILE

    cos_p, sin_p = _rope_tables(jnp.arange(T, dtype=jnp.int32))
    cos_s, sin_s = _rope_tables(past + (jnp.arange(Bd * S, dtype=jnp.int32) % S))
    w_a = w_in_a[0].astype(bf16)
    w_b = jnp.concatenate([w_in_b[0][:, :Q_W], w_kv_shared, w_in_b[0][:, Q_W:]], axis=1).astype(bf16)
    lw = _stacked_weights(w_o, ln1_g, ln1_b, w_ffn_gate, conv_w, conv_b, w_ffn_up, w_ffn_down, ln2_g, ln2_b)
    xs = x_sample.reshape(1, Bd * S, D_MODEL)

    def feature_major(a):
        a = jnp.moveaxis(a, -3, -1)
        return a.reshape(a.shape[:-3] + (a.shape[-3] * HEAD_DIM, a.shape[-1]))

    def token_major(a, n_heads):
        a = a.reshape(a.shape[:-2] + (n_heads, HEAD_DIM, a.shape[-1]))
        return jnp.moveaxis(a, -1, -3)

    mem_kvt, mem_k2, mem_vt = _memkv(mem_prompt, w_mem_kv.astype(bf16))
    mem_k_out = token_major(mem_kvt[:, :, :MEM_W], MEM_HEADS)
    mem_v_out = token_major(mem_kvt[:, :, MEM_W:], MEM_HEADS)
    cache_mem_kt = feature_major(cache_mem_k)
    cache_mem_vt = feature_major(cache_mem_v)

    def sample_rows(a, n_heads, per_kv):
        return _block_diag_rows(a.reshape(Bd, S, n_heads * HEAD_DIM), n_heads, per_kv)

    def pad_new(a):
        return jnp.pad(a.reshape(Bd, S, KV_W), ((0, 0), (0, NEW_PAD - S), (0, 0)))

    def sample_mem(qm, layer):
        qbd = (sample_rows(qm, MEM_HEADS, 1) * SCALE).astype(bf16)
        o = _mem_sample(qbd, cache_mem_kt, cache_mem_vt, layer)
        return _diag_rows_out(o, MEM_HEADS, 1, S).reshape(1, Bd * S, MEM_W).astype(bf16)

    kat, vat, qpad, qmpad, k5, vt5, _ = _proj(x_prompt, w_a, cos_p, sin_p, True, want_q=False)
    sink_cols = jnp.repeat(sink_a[0].reshape(N_KV_HEADS, GROUP) * LOG2E, TOKEN_TILE, axis=1)[:, None, :]
    o_self = _swa_prompt(qpad, k5, vt5, sink_cols)
    o_mem = _mem_attn_prompt(qmpad, mem_k2[0], mem_vt[0])
    xp, tail0 = _merge_ffn(x_prompt, o_self, o_mem, lw, 0)
    win_kp = token_major(kat[:, :, T - wb:], N_KV_HEADS)[None]
    win_vp = token_major(vat[:, :, T - wb:], N_KV_HEADS)[None]

    qs, ks, vs, qms = _proj(xs, w_a, cos_s, sin_s, False)
    qbd = (sample_rows(qs, N_HEADS, GROUP) * SCALE).astype(bf16)
    sink_rows = jnp.repeat(sink_a[0], S)[:, None]
    o_s, win_kst, win_vst = _swa_sample(qbd, feature_major(cache_win_k[0]), feature_major(cache_win_v[0]),
                                        pad_new(ks), pad_new(vs), sink_rows)
    os_self = _diag_rows_out(o_s, N_HEADS, GROUP, S).reshape(1, Bd * S, Q_W).astype(bf16)
    xs, u0 = _merge_ffn(xs, os_self, sample_mem(qms, 0), lw, 0, _conv_injection(state_conv[0]))

    qb, kbt, vbt, qpad, qmpad, k5, vt5, kmean = _proj(xp, w_b, cos_p, sin_p, True)
    bias = _moba_gate(qb, kmean.reshape(B, nb, KV_W))
    o_self = _moba_prompt(qpad, k5, vt5, bias)
    o_mem = _mem_attn_prompt(qmpad, mem_k2[1], mem_vt[1])
    yp, tail1 = _merge_ffn(xp, o_self, o_mem, lw, 1)

    qs, ks, vs, qms = _proj(xs, w_b, cos_s, sin_s, False)
    qbd_f32 = sample_rows(qs, N_HEADS, GROUP)
    o_s = _moba_sample(page_table, (qbd_f32 * SCALE).astype(bf16), qbd_f32, pad_new(ks), pad_new(vs),
                       feature_major(cache_moba_k), feature_major(cache_moba_v))
    os_self = _diag_rows_out(o_s, N_HEADS, GROUP, S).reshape(1, Bd * S, Q_W).astype(bf16)
    ys, u1 = _merge_ffn(xs, os_self, sample_mem(qms, 1), lw, 1, _conv_injection(state_conv[1]))

    kv_heads = lambda a, lead: a.reshape(lead + (N_KV_HEADS, HEAD_DIM))
    conv_p = jnp.stack([tail0[:, 8 - (CONV_W - 1):], tail1[:, 8 - (CONV_W - 1):]])
    conv_s = jnp.stack([u.reshape(Bd, S, D_FF)[:, S - (CONV_W - 1):] for u in (u0, u1)])
    return (
        yp, ys.reshape(Bd, S, D_MODEL),
        win_kp, win_vp,
        token_major(kbt, N_KV_HEADS), token_major(vbt, N_KV_HEADS),
        mem_k_out, mem_v_out,
        conv_p,
        token_major(win_kst, N_KV_HEADS)[None], token_major(win_vst, N_KV_HEADS)[None],
        kv_heads(ks, (Bd, S)), kv_heads(vs, (Bd, S)),
        conv_s,
    )
```

```python
import functools
import math

import jax
import jax.numpy as jnp
from jax import lax
from jax.experimental import pallas as pl
from jax.experimental.pallas import tpu as pltpu

D_MODEL = 1024
HEAD_DIM = 64
N_HEADS = 12
N_KV_HEADS = 4
GROUP = N_HEADS // N_KV_HEADS
MEM_HEADS = 4
N_MEM = 256
Q_W = N_HEADS * HEAD_DIM
KV_W = N_KV_HEADS * HEAD_DIM
MEM_W = MEM_HEADS * HEAD_DIM
PROJ_W = Q_W + 2 * KV_W + MEM_W
WINDOW = 128
MOBA_BLOCK = 256
MOBA_TOPK = 3
D_FF = 2816
CONV_W = 3
ROPE_THETA = 10000.0
LN_EPS = 1e-5
DEPTH = 2
ALPHA = (2 * DEPTH) ** 0.25
SCALE = HEAD_DIM ** -0.5
LOG2E = math.log2(math.e)
SCALE_LOG2 = SCALE * LOG2E

LANES = 128
V_ROWS = HEAD_DIM + 16
TOKEN_TILE = 256
VMEM_LIMIT = 56 * 1024 * 1024
NEG_INF = float("-inf")

f32 = jnp.float32
bf16 = jnp.bfloat16


def _cparams(n_grid, vmem=VMEM_LIMIT, flags=None):
    return pltpu.CompilerParams(dimension_semantics=("arbitrary",) * n_grid, vmem_limit_bytes=vmem, flags=flags)


def _dot_nt(a, b, precision=None):
    return lax.dot_general(a, b, (((1,), (1,)), ((), ())), preferred_element_type=f32, precision=precision)


def _dot(a, b):
    return jnp.dot(a, b, preferred_element_type=f32)


def _layer_norm(r, g, b):
    mu = jnp.mean(r, axis=-1, keepdims=True)
    d = r - mu
    var = jnp.mean(d * d, axis=-1, keepdims=True)
    return d * lax.rsqrt(var + LN_EPS) * g + b


def _ones_row_block(width):
    row = lax.broadcasted_iota(jnp.int32, (V_ROWS - HEAD_DIM, width), 0)
    return jnp.where(row == 0, 1.0, 0.0).astype(f32)


def _half_mask(width):
    lane = lax.broadcasted_iota(jnp.int32, (1, width), 1)
    return (lane % HEAD_DIM) < (HEAD_DIM // 2)


PROJ_SUBTILES = 2


def _proj_kernel(x_ref, w_ref, cos_ref, sin_ref, *rest, layouts, want_q):
    outs = list(rest[:-1])
    y_s = rest[-1]
    q_ref = outs.pop(0) if want_q else None
    tm = TOKEN_TILE
    n_sub = x_ref.shape[1] // tm
    for t in range(n_sub):
        y_s[t] = _dot(x_ref[0, t * tm:(t + 1) * tm, :].astype(bf16), w_ref[...])
    width = Q_W + KV_W
    half = HEAD_DIM // 2
    first_half = _half_mask(width)
    lane = lax.broadcasted_iota(jnp.int32, (1, LANES), 1)
    lo = lane < HEAD_DIM

    def placed(tile, src_half, dst_half):
        t_ = tile if src_half == dst_half else pltpu.roll(tile, HEAD_DIM, 1)
        keep = lo if dst_half == 0 else jnp.logical_not(lo)
        return jnp.where(keep, t_, 0.0)

    for t in range(n_sub):
        rows = slice(t * tm, (t + 1) * tm)
        y = y_s[t]
        qk = y[:, :width]
        cos = jnp.tile(cos_ref[rows, :], (1, width // LANES))
        sin = jnp.tile(sin_ref[rows, :], (1, width // LANES))
        partner = jnp.where(first_half, pltpu.roll(qk, width - half, 1), pltpu.roll(qk, half, 1))
        rot = qk * cos + partner * sin
        q = rot[:, :Q_W]
        k = rot[:, Q_W:]
        v = y[:, Q_W + KV_W:Q_W + 2 * KV_W]
        qm = y[:, Q_W + 2 * KV_W:]
        if want_q:
            q_ref[0, rows, :] = q
        if not layouts:
            k_ref, v_ref, qm_ref = outs
            k_ref[0, rows, :] = k
            v_ref[0, rows, :] = v
            qm_ref[0, rows, :] = qm
            continue
        k_ref, v_ref, qpad_ref, qmpad_ref, k5_ref, vt5_ref, kmean_ref = outs
        kt = k.T
        vt = v.T
        k_ref[0, :, rows] = kt
        v_ref[0, :, rows] = vt
        for h in range(N_KV_HEADS):
            for g in range(GROUP):
                hq = h * GROUP + g
                tile = q[:, (hq // 2) * LANES:(hq // 2 + 1) * LANES]
                r0 = (t * GROUP + g) * tm
                qpad_ref[0, h, r0:r0 + tm, :] = (placed(tile, hq % 2, h % 2) * SCALE_LOG2).astype(bf16)
        for h in range(MEM_HEADS):
            tile = qm[:, (h // 2) * LANES:(h // 2 + 1) * LANES]
            qmpad_ref[0, h, rows, :] = (placed(tile, h % 2, h % 2) * SCALE_LOG2).astype(bf16)
        for p in range(KV_W // LANES):
            k5_ref[0, p, t] = k[:, p * LANES:(p + 1) * LANES].astype(bf16)
        ones_rows = _ones_row_block(tm)
        for h in range(N_KV_HEADS):
            vt5_ref[0, h, t] = jnp.concatenate([vt[h * HEAD_DIM:(h + 1) * HEAD_DIM, :], ones_rows], axis=0).astype(bf16)
        kmean_ref[0, t] = jnp.mean(k, axis=0, keepdims=True)


def _proj(x, w, cos, sin, layouts, want_q=True):
    B, T, _ = x.shape
    ns = PROJ_SUBTILES
    tm = TOKEN_TILE * ns
    nt = T // TOKEN_TILE
    n_tab = cos.shape[0] // tm
    tok = lambda w_: (jax.ShapeDtypeStruct((B, T, w_), f32), pl.BlockSpec((1, tm, w_), lambda b, i: (b, i, 0)))
    feat = lambda w_: (jax.ShapeDtypeStruct((B, w_, T), f32), pl.BlockSpec((1, w_, tm), lambda b, i: (b, 0, i)))
    outs = [tok(Q_W)] if want_q else []
    outs += [feat(KV_W), feat(KV_W)] if layouts else [tok(KV_W), tok(KV_W), tok(MEM_W)]
    out_shape = [o[0] for o in outs]
    out_specs = [o[1] for o in outs]
    if layouts:
        out_shape += [
            jax.ShapeDtypeStruct((B, N_KV_HEADS, GROUP * T, LANES), bf16),
            jax.ShapeDtypeStruct((B, MEM_HEADS, T, LANES), bf16),
            jax.ShapeDtypeStruct((B, KV_W // LANES, nt, TOKEN_TILE, LANES), bf16),
            jax.ShapeDtypeStruct((B, N_KV_HEADS, nt, V_ROWS, TOKEN_TILE), bf16),
            jax.ShapeDtypeStruct((B, nt, 1, KV_W), f32),
        ]
        out_specs += [
            pl.BlockSpec((1, N_KV_HEADS, GROUP * tm, LANES), lambda b, i: (b, 0, i, 0)),
            pl.BlockSpec((1, MEM_HEADS, tm, LANES), lambda b, i: (b, 0, i, 0)),
            pl.BlockSpec((1, KV_W // LANES, ns, TOKEN_TILE, LANES), lambda b, i: (b, 0, i, 0, 0)),
            pl.BlockSpec((1, N_KV_HEADS, ns, V_ROWS, TOKEN_TILE), lambda b, i: (b, 0, i, 0, 0)),
            pl.BlockSpec((1, ns, 1, KV_W), lambda b, i: (b, i, 0, 0)),
        ]
    return pl.pallas_call(
        functools.partial(_proj_kernel, layouts=layouts, want_q=want_q),
        grid=(B, T // tm),
        in_specs=[
            pl.BlockSpec((1, tm, D_MODEL), lambda b, i: (b, i, 0)),
            pl.BlockSpec((D_MODEL, PROJ_W), lambda b, i: (0, 0), pipeline_mode=pl.Buffered(1)),
            pl.BlockSpec((tm, LANES), lambda b, i: (i % n_tab, 0)),
            pl.BlockSpec((tm, LANES), lambda b, i: (i % n_tab, 0)),
        ],
        out_specs=out_specs,
        out_shape=out_shape,
        scratch_shapes=[pltpu.VMEM((ns, TOKEN_TILE, PROJ_W), f32)],
        compiler_params=_cparams(2),
        name="proj_layouts" if layouts else "proj",
    )(x, w, cos, sin)


def _rope_tables(pos):
    half = HEAD_DIM // 2
    inv = jnp.power(jnp.float32(ROPE_THETA), -jnp.arange(half, dtype=f32) / half)
    ang = pos.astype(f32)[:, None] * inv[None, :]
    cos = jnp.cos(ang)
    sin = jnp.sin(ang)
    reps = LANES // HEAD_DIM
    return jnp.tile(jnp.concatenate([cos, cos], axis=1), (1, reps)), jnp.tile(jnp.concatenate([-sin, sin], axis=1), (1, reps))


def _memkv_kernel(m_ref, w_ref, kv_ref, k2_ref, vt_ref):
    y = _dot(m_ref[0].astype(bf16), w_ref[0])
    yt = y.T
    kv_ref[0, 0] = yt
    for p in range(MEM_W // LANES):
        k2_ref[0, 0, p] = y[:, p * LANES:(p + 1) * LANES].astype(bf16)
    ones_rows = _ones_row_block(N_MEM)
    for h in range(MEM_HEADS):
        vh = yt[MEM_W + h * HEAD_DIM:MEM_W + (h + 1) * HEAD_DIM, :]
        vt_ref[0, 0, h] = jnp.concatenate([vh, ones_rows], axis=0).astype(bf16)


def _memkv(mem, w):
    B = mem.shape[0]
    return pl.pallas_call(
        _memkv_kernel,
        grid=(DEPTH, B),
        in_specs=[
            pl.BlockSpec((1, N_MEM, D_MODEL), lambda l, b: (b, 0, 0)),
            pl.BlockSpec((1, D_MODEL, 2 * MEM_W), lambda l, b: (l, 0, 0)),
        ],
        out_specs=[
            pl.BlockSpec((1, 1, 2 * MEM_W, N_MEM), lambda l, b: (l, b, 0, 0)),
            pl.BlockSpec((1, 1, MEM_W // LANES, N_MEM, LANES), lambda l, b: (l, b, 0, 0, 0)),
            pl.BlockSpec((1, 1, MEM_HEADS, V_ROWS, N_MEM), lambda l, b: (l, b, 0, 0, 0)),
        ],
        out_shape=[
            jax.ShapeDtypeStruct((DEPTH, B, 2 * MEM_W, N_MEM), f32),
            jax.ShapeDtypeStruct((DEPTH, B, MEM_W // LANES, N_MEM, LANES), bf16),
            jax.ShapeDtypeStruct((DEPTH, B, MEM_HEADS, V_ROWS, N_MEM), bf16),
        ],
        compiler_params=_cparams(2),
        name="memkv",
    )(mem, w)


def _gelu(c):
    return 0.5 * c * (1.0 + lax.erf(c * (1.0 / math.sqrt(2.0))))


def _merge_ffn_kernel(x_ref, os_ref, om_ref, wos_ref, wom_ref, g1_ref, b1_ref, wg_ref, cw_ref, cb_ref, wu_ref,
                      wd_ref, g2_ref, b2_ref, *rest, sample):
    if sample:
        inj_ref, y_ref, u_ref, x1_s, u_s, up_s = rest
    else:
        y_ref, tail_ref, carry_ref, x1_s, u_s, up_s = rest
        @pl.when(pl.program_id(1) == 0)
        def _():
            carry_ref[...] = jnp.zeros_like(carry_ref)
    tm = TOKEN_TILE
    n_sub = x_ref.shape[1] // tm
    for t in range(n_sub):
        rows = slice(t * tm, (t + 1) * tm)
        attn = _dot(os_ref[0, rows, :], wos_ref[0]) + _dot(om_ref[0, rows, :], wom_ref[0])
        x1 = _layer_norm(ALPHA * x_ref[0, rows, :] + attn, g1_ref[0], b1_ref[0])
        x1_s[t] = x1
        x1b = x1.astype(bf16)
        u_s[t] = _dot(x1b, wg_ref[0])
        up_s[t] = _dot(x1b, wu_ref[0])
    row = lax.broadcasted_iota(jnp.int32, (tm, 1), 0)
    cw = cw_ref[0]
    for t in range(n_sub):
        rows = slice(t * tm, (t + 1) * tm)
        u = u_s[t]
        r1 = pltpu.roll(u, 1, 0)
        r2 = pltpu.roll(u, 2, 0)
        if sample:
            s = row % 4
            inj = inj_ref[0, rows, :]
            prev1 = jnp.where(s >= 1, r1, pltpu.roll(inj, tm - 1, 0))
            prev2 = jnp.where(s >= 2, r2, inj)
            u_ref[0, rows, :] = u
        else:
            c6 = carry_ref[6:7, :]
            c7 = carry_ref[7:8, :]
            prev1 = jnp.where(row == 0, c7, r1)
            prev2 = jnp.where(row == 0, c6, jnp.where(row == 1, c7, r2))
            carry_ref[...] = u[tm - 8:, :]
            if t == n_sub - 1:
                tail_ref[0] = u[tm - 8:, :]
        c = cb_ref[0] + prev2 * cw[0:1, :] + prev1 * cw[1:2, :] + u * cw[2:3, :]
        hid = (_gelu(c) * up_s[t]).astype(bf16)
        y_ref[0, rows, :] = _layer_norm(ALPHA * x1_s[t] + _dot(hid, wd_ref[0]), g2_ref[0], b2_ref[0])


FFN_SUBTILES = 2


def _merge_ffn(x, o_self, o_mem, lw, layer, inj=None):
    B, T, _ = x.shape
    tm = TOKEN_TILE * FFN_SUBTILES
    nt = T // tm
    sample = inj is not None
    tok = lambda w: pl.BlockSpec((1, tm, w), lambda b, i: (b, i, 0))
    const = lambda r, c, rb=0: pl.BlockSpec((1, r, c), lambda b, i: (layer, rb, 0), pipeline_mode=pl.Buffered(1))
    in_specs = [
        tok(D_MODEL), tok(Q_W), tok(MEM_W),
        const(Q_W, D_MODEL), const(MEM_W, D_MODEL, Q_W // MEM_W), const(1, D_MODEL), const(1, D_MODEL),
        const(D_MODEL, D_FF), const(CONV_W, D_FF), const(1, D_FF), const(D_MODEL, D_FF),
        const(D_FF, D_MODEL), const(1, D_MODEL), const(1, D_MODEL),
    ]
    args = [x, o_self, o_mem, lw["w_o"], lw["w_o"], lw["g1"], lw["b1"], lw["w_gate"], lw["conv_w"],
            lw["conv_b"], lw["w_up"], lw["w_down"], lw["g2"], lw["b2"]]
    if sample:
        in_specs += [tok(D_FF)]
        args += [inj]
        out_specs = [tok(D_MODEL), tok(D_FF)]
        out_shape = [jax.ShapeDtypeStruct((B, T, D_MODEL), f32), jax.ShapeDtypeStruct((B, T, D_FF), f32)]
        scratch = []
    else:
        out_specs = [tok(D_MODEL), pl.BlockSpec((1, 8, D_FF), lambda b, i: (b, 0, 0))]
        out_shape = [jax.ShapeDtypeStruct((B, T, D_MODEL), f32), jax.ShapeDtypeStruct((B, 8, D_FF), f32)]
        scratch = [pltpu.VMEM((8, D_FF), f32)]
    scratch += [pltpu.VMEM((FFN_SUBTILES, TOKEN_TILE, D_MODEL), f32), pltpu.VMEM((FFN_SUBTILES, TOKEN_TILE, D_FF), f32),
                pltpu.VMEM((FFN_SUBTILES, TOKEN_TILE, D_FF), f32)]
    return pl.pallas_call(
        functools.partial(_merge_ffn_kernel, sample=sample),
        grid=(B, nt),
        in_specs=in_specs,
        out_specs=out_specs,
        out_shape=out_shape,
        scratch_shapes=scratch,
        compiler_params=_cparams(2),
        name="merge_ffn_sample" if sample else "merge_ffn",
    )(*args)


def _store_heads_transposed(ot_ref, out_ref):
    out_ref[0] = ot_ref[...].T.astype(out_ref.dtype)


def _mem_scores(qm_ref, mk_ref, ms_ref):
    tm = qm_ref.shape[2]
    for p in range(MEM_W // LANES):
        ms_ref[p] = _dot_nt(mk_ref[0, p], qm_ref[0, 2 * p:2 * p + 2].reshape(2 * tm, LANES))


def _mem_consume(ms_ref, mvt_ref, omt_ref, om_ref):
    tm = om_ref.shape[1]
    for h in range(MEM_HEADS):
        s = ms_ref[h // 2, :, (h % 2) * tm:(h % 2 + 1) * tm]
        m = jnp.max(s, axis=0, keepdims=True)
        e = jnp.exp2(s - m).astype(bf16)
        ot = _dot(mvt_ref[0, h], e)
        omt_ref[h * HEAD_DIM:(h + 1) * HEAD_DIM, :] = ot[:HEAD_DIM, :] / ot[HEAD_DIM:HEAD_DIM + 1, :]
    _store_heads_transposed(omt_ref, om_ref)


def _mem_specs(tm):
    in_specs = [
        pl.BlockSpec((1, MEM_HEADS, tm, LANES), lambda b, i: (b, 0, i, 0)),
        pl.BlockSpec((1, MEM_W // LANES, N_MEM, LANES), lambda b, i: (b, 0, 0, 0)),
        pl.BlockSpec((1, MEM_HEADS, V_ROWS, N_MEM), lambda b, i: (b, 0, 0, 0)),
    ]
    out_spec = pl.BlockSpec((1, tm, MEM_W), lambda b, i: (b, i, 0))
    scratch = [pltpu.VMEM((MEM_W, tm), f32), pltpu.VMEM((MEM_W // LANES, N_MEM, 2 * tm), f32)]
    return in_specs, out_spec, scratch


def _swa_prompt_kernel(q_ref, kc_ref, kp_ref, vc_ref, vp_ref, sink_ref, qm_ref, mk_ref, mvt_ref, out_ref, om_ref,
                       ot_ref, sc_ref, sp_ref, omt_ref, ms_ref):
    i = pl.program_id(1)
    tm = TOKEN_TILE
    n_pairs = KV_W // LANES
    per_pair = N_HEADS // n_pairs
    key_c = lax.broadcasted_iota(jnp.int32, (tm, tm), 0)
    key_p = lax.broadcasted_iota(jnp.int32, (WINDOW, tm), 0)
    qc = lax.broadcasted_iota(jnp.int32, (1, tm), 1)
    valid_c = (key_c <= qc) & (key_c > qc - WINDOW)
    valid_p = (key_p > qc) & (i > 0)
    for p in range(n_pairs):
        qp = q_ref[0, 2 * p:2 * p + 2].reshape(per_pair * tm, LANES)
        sc_ref[p] = _dot_nt(kc_ref[0, p, 0], qp)
        sp_ref[p] = _dot_nt(kp_ref[0, p, 0], qp)
    _mem_scores(qm_ref, mk_ref, ms_ref)
    for c in range(N_HEADS):
        h, g = divmod(c, GROUP)
        cols = slice((c % per_pair) * tm, (c % per_pair + 1) * tm)
        s_c = jnp.where(valid_c, sc_ref[c // per_pair, :, cols], NEG_INF)
        s_p = jnp.where(valid_p, sp_ref[c // per_pair, :, cols], NEG_INF)
        sink = sink_ref[h, :, g * tm:(g + 1) * tm]
        m = jnp.maximum(jnp.maximum(jnp.max(s_c, axis=0, keepdims=True), jnp.max(s_p, axis=0, keepdims=True)), sink)
        e_c = jnp.exp2(s_c - m)
        e_p = jnp.exp2(s_p - m)
        ot = _dot(vc_ref[0, h, 0], e_c.astype(bf16)) + _dot(vp_ref[0, h, 0], e_p.astype(bf16))
        den = ot[HEAD_DIM:HEAD_DIM + 1, :] + jnp.exp2(sink - m)
        ot_ref[c * HEAD_DIM:(c + 1) * HEAD_DIM, :] = ot[:HEAD_DIM, :] / den
    _store_heads_transposed(ot_ref, out_ref)
    _mem_consume(ms_ref, mvt_ref, omt_ref, om_ref)


def _swa_prompt(qpad, k5, vt5, sink_cols, qmpad, mk2, mvt):
    B = qpad.shape[0]
    nt = k5.shape[2]
    tm = TOKEN_TILE
    T = nt * tm
    prev = lambda i: jnp.maximum(i - 1, 0)
    mem_in, mem_out, mem_scratch = _mem_specs(tm)
    return pl.pallas_call(
        _swa_prompt_kernel,
        grid=(B, nt),
        in_specs=[
            pl.BlockSpec((1, N_KV_HEADS, GROUP * tm, LANES), lambda b, i: (b, 0, i, 0)),
            pl.BlockSpec((1, KV_W // LANES, 1, tm, LANES), lambda b, i: (b, 0, i, 0, 0)),
            pl.BlockSpec((1, KV_W // LANES, 1, WINDOW, LANES), lambda b, i: (b, 0, prev(i), tm // WINDOW - 1, 0)),
            pl.BlockSpec((1, N_KV_HEADS, 1, V_ROWS, tm), lambda b, i: (b, 0, i, 0, 0)),
            pl.BlockSpec((1, N_KV_HEADS, 1, V_ROWS, WINDOW), lambda b, i: (b, 0, prev(i), 0, tm // WINDOW - 1)),
            pl.BlockSpec((N_KV_HEADS, 1, GROUP * tm), lambda b, i: (0, 0, 0)),
        ] + mem_in,
        out_specs=[pl.BlockSpec((1, tm, Q_W), lambda b, i: (b, i, 0)), mem_out],
        out_shape=[jax.ShapeDtypeStruct((B, T, Q_W), bf16), jax.ShapeDtypeStruct((B, T, MEM_W), bf16)],
        scratch_shapes=[pltpu.VMEM((Q_W, tm), f32),
                        pltpu.VMEM((KV_W // LANES, tm, N_HEADS * tm * LANES // KV_W), f32),
                        pltpu.VMEM((KV_W // LANES, WINDOW, N_HEADS * tm * LANES // KV_W), f32)] + mem_scratch,
        compiler_params=_cparams(2),
        name="swa_prompt",
    )(qpad, k5, k5, vt5, vt5, sink_cols, qmpad, mk2, mvt)


def _top3_bias(gate, eligible, axis):
    idx = lax.broadcasted_iota(jnp.int32, gate.shape, axis)
    n = gate.shape[axis]
    rest = jnp.where(eligible, gate, NEG_INF)
    chosen = jnp.zeros(gate.shape, dtype=jnp.bool_)
    for _ in range(MOBA_TOPK):
        best = jnp.max(rest, axis=axis, keepdims=True)
        first = jnp.min(jnp.where(rest == best, idx, n), axis=axis, keepdims=True)
        pick = (idx == first) & (best > NEG_INF)
        chosen = chosen | pick
        rest = jnp.where(pick, NEG_INF, rest)
    return jnp.where(chosen, 0.0, NEG_INF)


def _moba_gate_kernel(q_ref, km_ref, bias_ref):
    i = pl.program_id(1)
    tm = TOKEN_TILE
    nb = km_ref.shape[1]
    q = q_ref[0]
    km = km_ref[0]
    q_hi = q.astype(bf16)
    q_lo = (q - q_hi.astype(f32)).astype(bf16)
    lane = lax.broadcasted_iota(jnp.int32, (1, LANES), 1)
    lo = lane < HEAD_DIM
    blk = lax.broadcasted_iota(jnp.int32, (nb, 1), 0)
    for h in range(N_KV_HEADS):
        kt = km[:, (h // 2) * LANES:(h // 2 + 1) * LANES]
        gates = []
        for g in range(GROUP):
            hq = h * GROUP + g
            kk = kt if hq % 2 == h % 2 else pltpu.roll(kt, HEAD_DIM, 1)
            kk = jnp.where(lo if hq % 2 == 0 else jnp.logical_not(lo), kk, 0.0)
            kk_hi = kk.astype(bf16)
            kk_lo = (kk - kk_hi.astype(f32)).astype(bf16)
            cols = slice((hq // 2) * LANES, (hq // 2 + 1) * LANES)
            gates.append(_dot_nt(kk_hi, q_hi[:, cols]) + _dot_nt(kk_hi, q_lo[:, cols])
                         + _dot_nt(kk_lo, q_hi[:, cols]))
        gate = jnp.concatenate(gates, axis=1)
        bias_ref[0, 0, h] = _top3_bias(gate, blk < i, 0)


def _moba_gate(q, kmean):
    B, T, _ = q.shape
    tm = TOKEN_TILE
    nb = T // tm
    return pl.pallas_call(
        _moba_gate_kernel,
        grid=(B, nb),
        in_specs=[
            pl.BlockSpec((1, tm, Q_W), lambda b, i: (b, i, 0)),
            pl.BlockSpec((1, nb, KV_W), lambda b, i: (b, 0, 0)),
        ],
        out_specs=pl.BlockSpec((1, 1, N_KV_HEADS, nb, GROUP * tm), lambda b, i: (b, i, 0, 0, 0)),
        out_shape=jax.ShapeDtypeStruct((B, nb, N_KV_HEADS, nb, GROUP * tm), f32),
        compiler_params=_cparams(2),
        name="moba_gate",
    )(q, kmean)


def _moba_prompt_kernel(q_ref, k_ref, vt_ref, bias_ref, qm_ref, mk_ref, mvt_ref, out_ref, om_ref,
                        acc_ref, m_ref, sa_ref, sb_ref, omt_ref, ms_ref):
    i = pl.program_id(1)
    tm = TOKEN_TILE
    nb = k_ref.shape[2]
    n_pairs = KV_W // LANES
    per_pair = N_HEADS // n_pairs
    key = lax.broadcasted_iota(jnp.int32, (tm, tm), 0)
    qc = lax.broadcasted_iota(jnp.int32, (1, tm), 1)
    causal = key <= qc

    def scores_into(j, s_ref):
        for p in range(n_pairs):
            qp = q_ref[0, 2 * p:2 * p + 2].reshape(per_pair * tm, LANES)
            s_ref[p] = _dot_nt(k_ref[0, p, j], qp)

    def chunk(s_ref, c):
        return s_ref[c // per_pair, :, (c % per_pair) * tm:(c % per_pair + 1) * tm]

    scores_into(i, sb_ref)
    scores_into(0, sa_ref)
    _mem_scores(qm_ref, mk_ref, ms_ref)
    for c in range(N_HEADS):
        s = jnp.where(causal, chunk(sb_ref, c), NEG_INF)
        m = jnp.max(s, axis=0, keepdims=True)
        e = jnp.exp2(s - m)
        m_ref[c] = m
        acc_ref[c] = _dot(vt_ref[0, c // GROUP, i], e.astype(bf16))
    _mem_consume(ms_ref, mvt_ref, omt_ref, om_ref)

    def consume(j, s_ref):
        for c in range(N_HEADS):
            h, g = divmod(c, GROUP)
            s = chunk(s_ref, c)
            bias = bias_ref[0, 0, h, pl.ds(j, 1), g * tm:(g + 1) * tm]
            m = m_ref[c]
            m_new = jnp.maximum(m, jnp.max(s, axis=0, keepdims=True) + bias)
            a = jnp.exp2(m - m_new)
            e = jnp.exp2(s - (m_new - bias))
            m_ref[c] = m_new
            acc_ref[c] = a * acc_ref[c] + _dot(vt_ref[0, h, j], e.astype(bf16))

    def body(jj, carry):
        j = 2 * jj
        scores_into(j + 1, sb_ref)
        consume(j, sa_ref)
        scores_into(jnp.minimum(j + 2, nb - 1), sa_ref)
        consume(j + 1, sb_ref)
        return carry

    lax.fori_loop(0, (i + 1) // 2, body, 0)
    heads = [acc_ref[c, :HEAD_DIM, :] / acc_ref[c, HEAD_DIM:HEAD_DIM + 1, :] for c in range(N_HEADS)]
    out_ref[0] = jnp.concatenate(heads, axis=0).T.astype(out_ref.dtype)


def _moba_prompt(qpad, k5, vt5, bias, qmpad, mk2, mvt):
    B = qpad.shape[0]
    nb = k5.shape[2]
    tm = TOKEN_TILE
    T = nb * tm
    mem_in, mem_out, mem_scratch = _mem_specs(tm)
    return pl.pallas_call(
        _moba_prompt_kernel,
        grid=(B, nb),
        in_specs=[
            pl.BlockSpec((1, N_KV_HEADS, GROUP * tm, LANES), lambda b, i: (b, 0, i, 0)),
            pl.BlockSpec((1, KV_W // LANES, nb, tm, LANES), lambda b, i: (b, 0, 0, 0, 0)),
            pl.BlockSpec((1, N_KV_HEADS, nb, V_ROWS, tm), lambda b, i: (b, 0, 0, 0, 0)),
            pl.BlockSpec((1, 1, N_KV_HEADS, nb, GROUP * tm), lambda b, i: (b, i, 0, 0, 0)),
        ] + mem_in,
        out_specs=[pl.BlockSpec((1, tm, Q_W), lambda b, i: (b, i, 0)), mem_out],
        out_shape=[jax.ShapeDtypeStruct((B, T, Q_W), bf16), jax.ShapeDtypeStruct((B, T, MEM_W), bf16)],
        scratch_shapes=[pltpu.VMEM((N_HEADS, V_ROWS, tm), f32), pltpu.VMEM((N_HEADS, 1, tm), f32),
                        pltpu.VMEM((KV_W // LANES, tm, N_HEADS * tm * LANES // KV_W), f32),
                        pltpu.VMEM((KV_W // LANES, tm, N_HEADS * tm * LANES // KV_W), f32)] + mem_scratch,
        compiler_params=_cparams(2),
        name="moba_prompt",
    )(qpad, k5, vt5, bias, qmpad, mk2, mvt)


SAMPLE_ROWS = N_HEADS * 4
NEW_PAD = 8
SEQ_GROUP = 8


def _pad_rows(a, n):
    return jnp.concatenate([a, jnp.zeros((n - a.shape[0], a.shape[1]), a.dtype)], axis=0)


def _swa_sample_kernel(q_ref, kc_ref, vc_ref, kn_ref, vn_ref, sink_ref, o_ref, nk_ref, nv_ref, s_ref):
    wb = kc_ref.shape[2]
    col = lax.broadcasted_iota(jnp.int32, (SAMPLE_ROWS, wb), 1)
    s_idx = lax.broadcasted_iota(jnp.int32, (SAMPLE_ROWS, 1), 0) % 4
    valid_c = col > s_idx
    valid_n = col <= s_idx
    lane = lax.broadcasted_iota(jnp.int32, (1, wb), 1)
    sink = sink_ref[...]
    for n in range(SEQ_GROUP):
        q = q_ref[n]
        kn = _pad_rows(kn_ref[n], wb)
        s_ref[n, :, :wb] = _dot(q, kc_ref[n].astype(bf16))
        s_ref[n, :, wb:] = _dot_nt(q, kn.astype(bf16))
    for n in range(SEQ_GROUP):
        vn = _pad_rows(vn_ref[n], wb)
        s_c = jnp.where(valid_c, s_ref[n, :, :wb], NEG_INF)
        s_n = jnp.where(valid_n, s_ref[n, :, wb:], NEG_INF)
        m = jnp.maximum(jnp.maximum(jnp.max(s_c, axis=1, keepdims=True), jnp.max(s_n, axis=1, keepdims=True)), sink)
        e_c = jnp.exp(s_c - m)
        e_n = jnp.exp(s_n - m)
        den = jnp.sum(e_c, axis=1, keepdims=True) + jnp.sum(e_n, axis=1, keepdims=True) + jnp.exp(sink - m)
        o = _dot_nt(e_c.astype(bf16), vc_ref[n].astype(bf16)) + _dot(e_n.astype(bf16), vn.astype(bf16))
        o_ref[n] = o / den
    for n in range(SEQ_GROUP):
        kn = _pad_rows(kn_ref[n], wb)
        vn = _pad_rows(vn_ref[n], wb)
        nk_ref[n] = jnp.where(lane < wb - 4, pltpu.roll(kc_ref[n], wb - 4, 1), pltpu.roll(kn.T, wb - 4, 1))
        nv_ref[n] = jnp.where(lane < wb - 4, pltpu.roll(vc_ref[n], wb - 4, 1), pltpu.roll(vn.T, wb - 4, 1))


def _swa_sample(qbd, kc, vc, kn, vn, sink_rows):
    n_seq, _, wb = kc.shape
    sg = SEQ_GROUP
    seq = lambda r, w: pl.BlockSpec((sg, r, w), lambda i: (i, 0, 0))
    return pl.pallas_call(
        _swa_sample_kernel,
        grid=(n_seq // sg,),
        in_specs=[seq(SAMPLE_ROWS, KV_W), seq(KV_W, wb), seq(KV_W, wb), seq(NEW_PAD, KV_W), seq(NEW_PAD, KV_W),
                  pl.BlockSpec((SAMPLE_ROWS, 1), lambda i: (0, 0))],
        out_specs=[seq(SAMPLE_ROWS, KV_W), seq(KV_W, wb), seq(KV_W, wb)],
        out_shape=[jax.ShapeDtypeStruct((n_seq, SAMPLE_ROWS, KV_W), f32),
                   jax.ShapeDtypeStruct((n_seq, KV_W, wb), f32),
                   jax.ShapeDtypeStruct((n_seq, KV_W, wb), f32)],
        scratch_shapes=[pltpu.VMEM((sg, SAMPLE_ROWS, 2 * wb), f32)],
        compiler_params=_cparams(1),
        name="swa_sample",
    )(qbd, kc, vc, kn, vn, sink_rows)


def _mem_sample_kernel(q_ref, k_ref, v_ref, o_ref):
    for n in range(SEQ_GROUP):
        s = _dot(q_ref[n], k_ref[0, n].astype(bf16))
        m = jnp.max(s, axis=1, keepdims=True)
        e = jnp.exp(s - m)
        den = jnp.sum(e, axis=1, keepdims=True)
        o_ref[n] = _dot_nt(e.astype(bf16), v_ref[0, n].astype(bf16)) / den


def _mem_sample(qbd, km, vm, layer):
    n_seq = km.shape[1]
    sg = SEQ_GROUP
    rows = qbd.shape[1]
    cache = pl.BlockSpec((1, sg, MEM_W, N_MEM), lambda i: (layer, i, 0, 0))
    return pl.pallas_call(
        _mem_sample_kernel,
        grid=(n_seq // sg,),
        in_specs=[pl.BlockSpec((sg, rows, MEM_W), lambda i: (i, 0, 0)), cache, cache],
        out_specs=pl.BlockSpec((sg, rows, MEM_W), lambda i: (i, 0, 0)),
        out_shape=jax.ShapeDtypeStruct((n_seq, rows, MEM_W), f32),
        compiler_params=_cparams(1),
        name="mem_sample",
    )(qbd, km, vm)


def _moba_sample_kernel(pt_ref, q_ref, qf_ref, kn_ref, vn_ref, kpool_ref, vpool_ref, o_ref,
                        kbuf, vbuf, sem, s_ref, e_ref):
    b = pl.program_id(0)
    n_seq = pl.num_programs(0)
    n_pages = pt_ref.shape[1]
    page = kpool_ref.shape[2]
    past = n_pages * page
    nb = past // MOBA_BLOCK
    slot = b % 2

    def page_copy(which, seq, slot_, pg):
        pool, buf = ((kpool_ref, kbuf), (vpool_ref, vbuf))[which]
        return pltpu.make_async_copy(pool.at[pt_ref[seq, pg]], buf.at[slot_, :, pg * page:(pg + 1) * page],
                                     sem.at[which, slot_])

    def fetch(seq, slot_):
        for pg in range(n_pages):
            page_copy(0, seq, slot_, pg).start()
        for pg in range(n_pages):
            page_copy(1, seq, slot_, pg).start()

    def wait(which, seq, slot_):
        for pg in range(n_pages):
            page_copy(which, seq, slot_, pg).wait()

    @pl.when(b == 0)
    def _():
        fetch(0, 0)

    @pl.when(b + 1 < n_seq)
    def _():
        fetch(b + 1, 1 - slot)

    wait(0, b, slot)
    q = q_ref[0]
    s_ref[...] = _dot(q, kbuf[slot].astype(bf16))
    lane = lax.broadcasted_iota(jnp.int32, (1, LANES), 1)
    kmean = jnp.zeros((KV_W, LANES), f32)
    for j in range(nb):
        part = kbuf[slot, :, j * MOBA_BLOCK:j * MOBA_BLOCK + LANES]
        for c in range(1, MOBA_BLOCK // LANES):
            part = part + kbuf[slot, :, j * MOBA_BLOCK + c * LANES:j * MOBA_BLOCK + (c + 1) * LANES]
        kmean = jnp.where(lane == j, jnp.sum(part, axis=1, keepdims=True), kmean)
    kmean = kmean * (1.0 / MOBA_BLOCK)
    gate = jnp.dot(qf_ref[0], kmean, preferred_element_type=f32, precision=lax.Precision.HIGHEST)
    bias = _top3_bias(gate, lane < nb, 1)
    col = lax.broadcasted_iota(jnp.int32, (SAMPLE_ROWS, LANES), 1)
    s_idx = lax.broadcasted_iota(jnp.int32, (SAMPLE_ROWS, 1), 0) % 4
    kn = _pad_rows(kn_ref[0], LANES).astype(bf16)
    vn = _pad_rows(vn_ref[0], LANES).astype(bf16)
    s_n = jnp.where(col <= s_idx, _dot_nt(q, kn), NEG_INF)
    mx = jnp.full((SAMPLE_ROWS, MOBA_BLOCK), NEG_INF, f32)
    for j in range(nb):
        mx = jnp.maximum(mx, s_ref[:, j * MOBA_BLOCK:(j + 1) * MOBA_BLOCK] + bias[:, j:j + 1])
    m = jnp.maximum(jnp.max(mx, axis=1, keepdims=True), jnp.max(s_n, axis=1, keepdims=True))
    acc = jnp.zeros((SAMPLE_ROWS, MOBA_BLOCK), f32)
    for j in range(nb):
        e = jnp.exp(s_ref[:, j * MOBA_BLOCK:(j + 1) * MOBA_BLOCK] - (m - bias[:, j:j + 1]))
        acc = acc + e
        e_ref[:, j * MOBA_BLOCK:(j + 1) * MOBA_BLOCK] = e.astype(bf16)
    e_n = jnp.exp(s_n - m)
    den = jnp.sum(acc, axis=1, keepdims=True) + jnp.sum(e_n, axis=1, keepdims=True)
    wait(1, b, slot)
    o = _dot_nt(e_ref[...], vbuf[slot].astype(bf16)) + _dot(e_n.astype(bf16), vn)
    o_ref[0] = o / den


def _moba_sample(page_table, qbd, qbd_f32, kn, vn, kpool, vpool):
    n_seq, n_pages = page_table.shape
    page = kpool.shape[2]
    past = n_pages * page
    seq = lambda r: pl.BlockSpec((1, r, KV_W), lambda i, pt: (i, 0, 0))
    grid_spec = pltpu.PrefetchScalarGridSpec(
        num_scalar_prefetch=1,
        grid=(n_seq,),
        in_specs=[seq(SAMPLE_ROWS), seq(SAMPLE_ROWS), seq(NEW_PAD), seq(NEW_PAD),
                  pl.BlockSpec(memory_space=pl.ANY), pl.BlockSpec(memory_space=pl.ANY)],
        out_specs=seq(SAMPLE_ROWS),
        scratch_shapes=[pltpu.VMEM((2, KV_W, past), f32), pltpu.VMEM((2, KV_W, past), f32),
                        pltpu.SemaphoreType.DMA((2, 2)),
                        pltpu.VMEM((SAMPLE_ROWS, past), f32), pltpu.VMEM((SAMPLE_ROWS, past), bf16)],
    )
    return pl.pallas_call(
        _moba_sample_kernel,
        grid_spec=grid_spec,
        out_shape=jax.ShapeDtypeStruct((n_seq, SAMPLE_ROWS, KV_W), f32),
        compiler_params=_cparams(1),
        name="moba_sample",
    )(page_table, qbd, qbd_f32, kn, vn, kpool, vpool)


def _block_diag_rows(a, n_heads, per_kv):
    n_seq, S, _ = a.shape
    n_kv = n_heads // per_kv
    a = a.reshape(n_seq, S, n_kv, per_kv, HEAD_DIM).transpose(0, 2, 3, 1, 4)
    eye = jnp.eye(n_kv, dtype=a.dtype)
    bd = a[:, :, :, :, None, :] * eye[None, :, None, None, :, None]
    return bd.reshape(n_seq, n_heads * S, n_kv * HEAD_DIM)


def _diag_rows_out(o, n_heads, per_kv, S):
    n_seq = o.shape[0]
    n_kv = n_heads // per_kv
    o = o.reshape(n_seq, n_kv, per_kv, S, n_kv, HEAD_DIM)
    o = jnp.stack([o[:, h, :, :, h, :] for h in range(n_kv)], axis=1)
    return o.transpose(0, 3, 1, 2, 4).reshape(n_seq, S, n_heads * HEAD_DIM)


def _stacked_weights(w_o, ln1_g, ln1_b, w_ffn_gate, conv_w, conv_b, w_ffn_up, w_ffn_down, ln2_g, ln2_b):
    row = lambda a: a[:, None, :]
    return dict(
        w_o=w_o.astype(bf16), g1=row(ln1_g), b1=row(ln1_b),
        w_gate=w_ffn_gate.astype(bf16), conv_w=conv_w, conv_b=row(conv_b),
        w_up=w_ffn_up.astype(bf16), w_down=w_ffn_down.astype(bf16),
        g2=row(ln2_g), b2=row(ln2_b),
    )


def _conv_injection(state):
    n_seq = state.shape[0]
    return jnp.pad(state, ((0, 0), (0, 2), (0, 0))).reshape(1, n_seq * 4, D_FF)


def kernel(x_prompt, x_sample, cache_win_k, cache_win_v, cache_moba_k, cache_moba_v, page_table, cache_mem_k, cache_mem_v, state_conv, mem_prompt, w_in_a, sink_a, w_in_b, w_kv_shared, w_mem_kv, w_o, ln1_g, ln1_b, w_ffn_gate, conv_w, conv_b, w_ffn_up, w_ffn_down, ln2_g, ln2_b):
    B, T, _ = x_prompt.shape
    Bd, S, _ = x_sample.shape
    n_pool, page = cache_moba_k.shape[:2]
    past = page_table.shape[1] * page
    wb = cache_win_k.shape[2]
    nb = T // TOKEN_TILE

    cos_p, sin_p = _rope_tables(jnp.arange(T, dtype=jnp.int32))
    cos_s, sin_s = _rope_tables(past + (jnp.arange(Bd * S, dtype=jnp.int32) % S))
    w_a = w_in_a[0].astype(bf16)
    w_b = jnp.concatenate([w_in_b[0][:, :Q_W], w_kv_shared, w_in_b[0][:, Q_W:]], axis=1).astype(bf16)
    lw = _stacked_weights(w_o, ln1_g, ln1_b, w_ffn_gate, conv_w, conv_b, w_ffn_up, w_ffn_down, ln2_g, ln2_b)
    xs = x_sample.reshape(1, Bd * S, D_MODEL)

    def feature_major(a):
        a = jnp.moveaxis(a, -3, -1)
        return a.reshape(a.shape[:-3] + (a.shape[-3] * HEAD_DIM, a.shape[-1]))

    def token_major(a, n_heads):
        a = a.reshape(a.shape[:-2] + (n_heads, HEAD_DIM, a.shape[-1]))
        return jnp.moveaxis(a, -1, -3)

    mem_kvt, mem_k2, mem_vt = _memkv(mem_prompt, w_mem_kv.astype(bf16))
    mem_k_out = token_major(mem_kvt[:, :, :MEM_W], MEM_HEADS)
    mem_v_out = token_major(mem_kvt[:, :, MEM_W:], MEM_HEADS)
    cache_mem_kt = feature_major(cache_mem_k)
    cache_mem_vt = feature_major(cache_mem_v)

    def sample_rows(a, n_heads, per_kv):
        return _block_diag_rows(a.reshape(Bd, S, n_heads * HEAD_DIM), n_heads, per_kv)

    def pad_new(a):
        return jnp.pad(a.reshape(Bd, S, KV_W), ((0, 0), (0, NEW_PAD - S), (0, 0)))

    def sample_mem(qm, layer):
        qbd = (sample_rows(qm, MEM_HEADS, 1) * SCALE).astype(bf16)
        o = _mem_sample(qbd, cache_mem_kt, cache_mem_vt, layer)
        return _diag_rows_out(o, MEM_HEADS, 1, S).reshape(1, Bd * S, MEM_W).astype(bf16)

    kat, vat, qpad, qmpad, k5, vt5, _ = _proj(x_prompt, w_a, cos_p, sin_p, True, want_q=False)
    sink_cols = jnp.repeat(sink_a[0].reshape(N_KV_HEADS, GROUP) * LOG2E, TOKEN_TILE, axis=1)[:, None, :]
    o_self, o_mem = _swa_prompt(qpad, k5, vt5, sink_cols, qmpad, mem_k2[0], mem_vt[0])
    xp, tail0 = _merge_ffn(x_prompt, o_self, o_mem, lw, 0)
    win_kp = token_major(kat[:, :, T - wb:], N_KV_HEADS)[None]
    win_vp = token_major(vat[:, :, T - wb:], N_KV_HEADS)[None]

    qs, ks, vs, qms = _proj(xs, w_a, cos_s, sin_s, False)
    qbd = (sample_rows(qs, N_HEADS, GROUP) * SCALE).astype(bf16)
    sink_rows = jnp.repeat(sink_a[0], S)[:, None]
    o_s, win_kst, win_vst = _swa_sample(qbd, feature_major(cache_win_k[0]), feature_major(cache_win_v[0]),
                                        pad_new(ks), pad_new(vs), sink_rows)
    os_self = _diag_rows_out(o_s, N_HEADS, GROUP, S).reshape(1, Bd * S, Q_W).astype(bf16)
    xs, u0 = _merge_ffn(xs, os_self, sample_mem(qms, 0), lw, 0, _conv_injection(state_conv[0]))

    qb, kbt, vbt, qpad, qmpad, k5, vt5, kmean = _proj(xp, w_b, cos_p, sin_p, True)
    bias = _moba_gate(qb, kmean.reshape(B, nb, KV_W))
    o_self, o_mem = _moba_prompt(qpad, k5, vt5, bias, qmpad, mem_k2[1], mem_vt[1])
    yp, tail1 = _merge_ffn(xp, o_self, o_mem, lw, 1)

    qs, ks, vs, qms = _proj(xs, w_b, cos_s, sin_s, False)
    qbd_f32 = sample_rows(qs, N_HEADS, GROUP)
    o_s = _moba_sample(page_table, (qbd_f32 * SCALE).astype(bf16), qbd_f32, pad_new(ks), pad_new(vs),
                       feature_major(cache_moba_k), feature_major(cache_moba_v))
    os_self = _diag_rows_out(o_s, N_HEADS, GROUP, S).reshape(1, Bd * S, Q_W).astype(bf16)
    ys, u1 = _merge_ffn(xs, os_self, sample_mem(qms, 1), lw, 1, _conv_injection(state_conv[1]))

    kv_heads = lambda a, lead: a.reshape(lead + (N_KV_HEADS, HEAD_DIM))
    conv_p = jnp.stack([tail0[:, 8 - (CONV_W - 1):], tail1[:, 8 - (CONV_W - 1):]])
    conv_s = jnp.stack([u.reshape(Bd, S, D_FF)[:, S - (CONV_W - 1):] for u in (u0, u1)])
    return (
        yp, ys.reshape(Bd, S, D_MODEL),
        win_kp, win_vp,
        token_major(kbt, N_KV_HEADS), token_major(vbt, N_KV_HEADS),
        mem_k_out, mem_v_out,
        conv_p,
        token_major(win_kst, N_KV_HEADS)[None], token_major(win_vst, N_KV_HEADS)[None],
        kv_heads(ks, (Bd, S)), kv_heads(vs, (Bd, S)),
        conv_s,
    )
```

```python
import functools
import math

import jax
import jax.numpy as jnp
from jax import lax
from jax.experimental import pallas as pl
from jax.experimental.pallas import tpu as pltpu

D_MODEL = 1024
HEAD_DIM = 64
N_HEADS = 12
N_KV_HEADS = 4
GROUP = N_HEADS // N_KV_HEADS
MEM_HEADS = 4
N_MEM = 256
Q_W = N_HEADS * HEAD_DIM
KV_W = N_KV_HEADS * HEAD_DIM
MEM_W = MEM_HEADS * HEAD_DIM
PROJ_W = Q_W + 2 * KV_W + MEM_W
WINDOW = 128
MOBA_BLOCK = 256
MOBA_TOPK = 3
D_FF = 2816
CONV_W = 3
ROPE_THETA = 10000.0
LN_EPS = 1e-5
DEPTH = 2
ALPHA = (2 * DEPTH) ** 0.25
SCALE = HEAD_DIM ** -0.5
LOG2E = math.log2(math.e)
SCALE_LOG2 = SCALE * LOG2E

LANES = 128
V_ROWS = HEAD_DIM + 16
TOKEN_TILE = 256
VMEM_LIMIT = 56 * 1024 * 1024
NEG_INF = float("-inf")

f32 = jnp.float32
bf16 = jnp.bfloat16


def _cparams(n_grid, vmem=VMEM_LIMIT, flags=None):
    return pltpu.CompilerParams(dimension_semantics=("arbitrary",) * n_grid, vmem_limit_bytes=vmem, flags=flags)


def _dot_nt(a, b, precision=None):
    return lax.dot_general(a, b, (((1,), (1,)), ((), ())), preferred_element_type=f32, precision=precision)


def _dot(a, b):
    return jnp.dot(a, b, preferred_element_type=f32)


def _layer_norm(r, g, b):
    mu = jnp.mean(r, axis=-1, keepdims=True)
    d = r - mu
    var = jnp.mean(d * d, axis=-1, keepdims=True)
    return d * lax.rsqrt(var + LN_EPS) * g + b


def _ones_row_block(width):
    row = lax.broadcasted_iota(jnp.int32, (V_ROWS - HEAD_DIM, width), 0)
    return jnp.where(row == 0, 1.0, 0.0).astype(f32)


def _half_mask(width):
    lane = lax.broadcasted_iota(jnp.int32, (1, width), 1)
    return (lane % HEAD_DIM) < (HEAD_DIM // 2)


PROJ_SUBTILES = 2


def _proj_kernel(x_ref, w_ref, cos_ref, sin_ref, *rest, layouts, want_q, want_gate):
    n_scratch = 2 if want_gate else 1
    outs = list(rest[:-n_scratch])
    y_s = rest[-n_scratch]
    q_ref = outs.pop(0) if want_q else None
    if want_gate:
        kmean_s = rest[-1]
        bias_ref = outs.pop()

        @pl.when((pl.program_id(0) == 0) & (pl.program_id(1) == 0))
        def _():
            kmean_s[...] = jnp.zeros_like(kmean_s)
    tm = TOKEN_TILE
    n_sub = x_ref.shape[1] // tm
    for t in range(n_sub):
        y_s[t] = _dot(x_ref[0, t * tm:(t + 1) * tm, :].astype(bf16), w_ref[...])
    width = Q_W + KV_W
    half = HEAD_DIM // 2
    first_half = _half_mask(width)
    lane = lax.broadcasted_iota(jnp.int32, (1, LANES), 1)
    lo = lane < HEAD_DIM

    def placed(tile, src_half, dst_half):
        t_ = tile if src_half == dst_half else pltpu.roll(tile, HEAD_DIM, 1)
        keep = lo if dst_half == 0 else jnp.logical_not(lo)
        return jnp.where(keep, t_, 0.0)

    for t in range(n_sub):
        rows = slice(t * tm, (t + 1) * tm)
        y = y_s[t]
        qk = y[:, :width]
        cos = jnp.tile(cos_ref[rows, :], (1, width // LANES))
        sin = jnp.tile(sin_ref[rows, :], (1, width // LANES))
        partner = jnp.where(first_half, pltpu.roll(qk, width - half, 1), pltpu.roll(qk, half, 1))
        rot = qk * cos + partner * sin
        q = rot[:, :Q_W]
        k = rot[:, Q_W:]
        v = y[:, Q_W + KV_W:Q_W + 2 * KV_W]
        qm = y[:, Q_W + 2 * KV_W:]
        if want_q:
            q_ref[0, rows, :] = q
        if not layouts:
            k_ref, v_ref, qm_ref = outs
            k_ref[0, rows, :] = k
            v_ref[0, rows, :] = v
            qm_ref[0, rows, :] = qm
            continue
        k_ref, v_ref, qpad_ref, qmpad_ref, k5_ref, vt5_ref, kmean_ref = outs
        kt = k.T
        vt = v.T
        k_ref[0, :, rows] = kt
        v_ref[0, :, rows] = vt
        for h in range(N_KV_HEADS):
            for g in range(GROUP):
                hq = h * GROUP + g
                tile = q[:, (hq // 2) * LANES:(hq // 2 + 1) * LANES]
                r0 = (t * GROUP + g) * tm
                qpad_ref[0, h, r0:r0 + tm, :] = (placed(tile, hq % 2, h % 2) * SCALE_LOG2).astype(bf16)
        for h in range(MEM_HEADS):
            tile = qm[:, (h // 2) * LANES:(h // 2 + 1) * LANES]
            qmpad_ref[0, h, rows, :] = (placed(tile, h % 2, h % 2) * SCALE_LOG2).astype(bf16)
        for p in range(KV_W // LANES):
            k5_ref[0, p, t] = k[:, p * LANES:(p + 1) * LANES].astype(bf16)
        ones_rows = _ones_row_block(tm)
        for h in range(N_KV_HEADS):
            vt5_ref[0, h, t] = jnp.concatenate([vt[h * HEAD_DIM:(h + 1) * HEAD_DIM, :], ones_rows], axis=0).astype(bf16)
        k_mean = jnp.mean(k, axis=0, keepdims=True)
        kmean_ref[0, t] = k_mean
        if want_gate:
            block = pl.program_id(1) * n_sub + t
            kmean_s[pl.ds(block, 1), :] = k_mean
            for h, bias in enumerate(_moba_gate_bias(q, kmean_s[...], block)):
                bias_ref[0, t, h] = bias


def _proj(x, w, cos, sin, layouts, want_q=True, want_gate=False):
    B, T, _ = x.shape
    ns = PROJ_SUBTILES
    tm = TOKEN_TILE * ns
    nt = T // TOKEN_TILE
    n_tab = cos.shape[0] // tm
    tok = lambda w_: (jax.ShapeDtypeStruct((B, T, w_), f32), pl.BlockSpec((1, tm, w_), lambda b, i: (b, i, 0)))
    feat = lambda w_: (jax.ShapeDtypeStruct((B, w_, T), f32), pl.BlockSpec((1, w_, tm), lambda b, i: (b, 0, i)))
    outs = [tok(Q_W)] if want_q else []
    outs += [feat(KV_W), feat(KV_W)] if layouts else [tok(KV_W), tok(KV_W), tok(MEM_W)]
    out_shape = [o[0] for o in outs]
    out_specs = [o[1] for o in outs]
    if layouts:
        out_shape += [
            jax.ShapeDtypeStruct((B, N_KV_HEADS, GROUP * T, LANES), bf16),
            jax.ShapeDtypeStruct((B, MEM_HEADS, T, LANES), bf16),
            jax.ShapeDtypeStruct((B, KV_W // LANES, nt, TOKEN_TILE, LANES), bf16),
            jax.ShapeDtypeStruct((B, N_KV_HEADS, nt, V_ROWS, TOKEN_TILE), bf16),
            jax.ShapeDtypeStruct((B, nt, 1, KV_W), f32),
        ]
        out_specs += [
            pl.BlockSpec((1, N_KV_HEADS, GROUP * tm, LANES), lambda b, i: (b, 0, i, 0)),
            pl.BlockSpec((1, MEM_HEADS, tm, LANES), lambda b, i: (b, 0, i, 0)),
            pl.BlockSpec((1, KV_W // LANES, ns, TOKEN_TILE, LANES), lambda b, i: (b, 0, i, 0, 0)),
            pl.BlockSpec((1, N_KV_HEADS, ns, V_ROWS, TOKEN_TILE), lambda b, i: (b, 0, i, 0, 0)),
            pl.BlockSpec((1, ns, 1, KV_W), lambda b, i: (b, i, 0, 0)),
        ]
    scratch = [pltpu.VMEM((ns, TOKEN_TILE, PROJ_W), f32)]
    if want_gate:
        out_shape.append(jax.ShapeDtypeStruct((B, nt, N_KV_HEADS, nt, GROUP * TOKEN_TILE), f32))
        out_specs.append(pl.BlockSpec((1, ns, N_KV_HEADS, nt, GROUP * TOKEN_TILE), lambda b, i: (b, i, 0, 0, 0)))
        scratch.append(pltpu.VMEM((nt, KV_W), f32))
    return pl.pallas_call(
        functools.partial(_proj_kernel, layouts=layouts, want_q=want_q, want_gate=want_gate),
        grid=(B, T // tm),
        in_specs=[
            pl.BlockSpec((1, tm, D_MODEL), lambda b, i: (b, i, 0)),
            pl.BlockSpec((D_MODEL, PROJ_W), lambda b, i: (0, 0), pipeline_mode=pl.Buffered(1)),
            pl.BlockSpec((tm, LANES), lambda b, i: (i % n_tab, 0)),
            pl.BlockSpec((tm, LANES), lambda b, i: (i % n_tab, 0)),
        ],
        out_specs=out_specs,
        out_shape=out_shape,
        scratch_shapes=scratch,
        compiler_params=_cparams(2),
        name="proj_layouts" if layouts else "proj",
    )(x, w, cos, sin)


def _rope_tables(pos):
    half = HEAD_DIM // 2
    inv = jnp.power(jnp.float32(ROPE_THETA), -jnp.arange(half, dtype=f32) / half)
    ang = pos.astype(f32)[:, None] * inv[None, :]
    cos = jnp.cos(ang)
    sin = jnp.sin(ang)
    reps = LANES // HEAD_DIM
    return jnp.tile(jnp.concatenate([cos, cos], axis=1), (1, reps)), jnp.tile(jnp.concatenate([-sin, sin], axis=1), (1, reps))


def _memkv_kernel(m_ref, w_ref, kv_ref, k2_ref, vt_ref):
    y = _dot(m_ref[0].astype(bf16), w_ref[0])
    yt = y.T
    kv_ref[0, 0] = yt
    for p in range(MEM_W // LANES):
        k2_ref[0, 0, p] = y[:, p * LANES:(p + 1) * LANES].astype(bf16)
    ones_rows = _ones_row_block(N_MEM)
    for h in range(MEM_HEADS):
        vh = yt[MEM_W + h * HEAD_DIM:MEM_W + (h + 1) * HEAD_DIM, :]
        vt_ref[0, 0, h] = jnp.concatenate([vh, ones_rows], axis=0).astype(bf16)


def _memkv(mem, w):
    B = mem.shape[0]
    return pl.pallas_call(
        _memkv_kernel,
        grid=(DEPTH, B),
        in_specs=[
            pl.BlockSpec((1, N_MEM, D_MODEL), lambda l, b: (b, 0, 0)),
            pl.BlockSpec((1, D_MODEL, 2 * MEM_W), lambda l, b: (l, 0, 0)),
        ],
        out_specs=[
            pl.BlockSpec((1, 1, 2 * MEM_W, N_MEM), lambda l, b: (l, b, 0, 0)),
            pl.BlockSpec((1, 1, MEM_W // LANES, N_MEM, LANES), lambda l, b: (l, b, 0, 0, 0)),
            pl.BlockSpec((1, 1, MEM_HEADS, V_ROWS, N_MEM), lambda l, b: (l, b, 0, 0, 0)),
        ],
        out_shape=[
            jax.ShapeDtypeStruct((DEPTH, B, 2 * MEM_W, N_MEM), f32),
            jax.ShapeDtypeStruct((DEPTH, B, MEM_W // LANES, N_MEM, LANES), bf16),
            jax.ShapeDtypeStruct((DEPTH, B, MEM_HEADS, V_ROWS, N_MEM), bf16),
        ],
        compiler_params=_cparams(2),
        name="memkv",
    )(mem, w)


def _gelu(c):
    return 0.5 * c * (1.0 + lax.erf(c * (1.0 / math.sqrt(2.0))))


def _merge_ffn_kernel(x_ref, os_ref, om_ref, wos_ref, wom_ref, g1_ref, b1_ref, wg_ref, cw_ref, cb_ref, wu_ref,
                      wd_ref, g2_ref, b2_ref, *rest, sample):
    if sample:
        inj_ref, y_ref, u_ref, x1_s, u_s, up_s = rest
    else:
        y_ref, tail_ref, carry_ref, x1_s, u_s, up_s = rest
        @pl.when(pl.program_id(1) == 0)
        def _():
            carry_ref[...] = jnp.zeros_like(carry_ref)
    tm = TOKEN_TILE
    n_sub = x_ref.shape[1] // tm
    for t in range(n_sub):
        rows = slice(t * tm, (t + 1) * tm)
        attn = _dot(os_ref[0, rows, :], wos_ref[0]) + _dot(om_ref[0, rows, :], wom_ref[0])
        x1 = _layer_norm(ALPHA * x_ref[0, rows, :] + attn, g1_ref[0], b1_ref[0])
        x1_s[t] = x1
        x1b = x1.astype(bf16)
        u_s[t] = _dot(x1b, wg_ref[0])
        up_s[t] = _dot(x1b, wu_ref[0])
    row = lax.broadcasted_iota(jnp.int32, (tm, 1), 0)
    cw = cw_ref[0]
    for t in range(n_sub):
        rows = slice(t * tm, (t + 1) * tm)
        u = u_s[t]
        r1 = pltpu.roll(u, 1, 0)
        r2 = pltpu.roll(u, 2, 0)
        if sample:
            s = row % 4
            inj = inj_ref[0, rows, :]
            prev1 = jnp.where(s >= 1, r1, pltpu.roll(inj, tm - 1, 0))
            prev2 = jnp.where(s >= 2, r2, inj)
            u_ref[0, rows, :] = u
        else:
            c6 = carry_ref[6:7, :]
            c7 = carry_ref[7:8, :]
            prev1 = jnp.where(row == 0, c7, r1)
            prev2 = jnp.where(row == 0, c6, jnp.where(row == 1, c7, r2))
            carry_ref[...] = u[tm - 8:, :]
            if t == n_sub - 1:
                tail_ref[0] = u[tm - 8:, :]
        c = cb_ref[0] + prev2 * cw[0:1, :] + prev1 * cw[1:2, :] + u * cw[2:3, :]
        hid = (_gelu(c) * up_s[t]).astype(bf16)
        y_ref[0, rows, :] = _layer_norm(ALPHA * x1_s[t] + _dot(hid, wd_ref[0]), g2_ref[0], b2_ref[0])


FFN_SUBTILES = 2


def _merge_ffn(x, o_self, o_mem, lw, layer, inj=None):
    B, T, _ = x.shape
    tm = TOKEN_TILE * FFN_SUBTILES
    nt = T // tm
    sample = inj is not None
    tok = lambda w: pl.BlockSpec((1, tm, w), lambda b, i: (b, i, 0))
    const = lambda r, c, rb=0: pl.BlockSpec((1, r, c), lambda b, i: (layer, rb, 0), pipeline_mode=pl.Buffered(1))
    in_specs = [
        tok(D_MODEL), tok(Q_W), tok(MEM_W),
        const(Q_W, D_MODEL), const(MEM_W, D_MODEL, Q_W // MEM_W), const(1, D_MODEL), const(1, D_MODEL),
        const(D_MODEL, D_FF), const(CONV_W, D_FF), const(1, D_FF), const(D_MODEL, D_FF),
        const(D_FF, D_MODEL), const(1, D_MODEL), const(1, D_MODEL),
    ]
    args = [x, o_self, o_mem, lw["w_o"], lw["w_o"], lw["g1"], lw["b1"], lw["w_gate"], lw["conv_w"],
            lw["conv_b"], lw["w_up"], lw["w_down"], lw["g2"], lw["b2"]]
    if sample:
        in_specs += [tok(D_FF)]
        args += [inj]
        out_specs = [tok(D_MODEL), tok(D_FF)]
        out_shape = [jax.ShapeDtypeStruct((B, T, D_MODEL), f32), jax.ShapeDtypeStruct((B, T, D_FF), f32)]
        scratch = []
    else:
        out_specs = [tok(D_MODEL), pl.BlockSpec((1, 8, D_FF), lambda b, i: (b, 0, 0))]
        out_shape = [jax.ShapeDtypeStruct((B, T, D_MODEL), f32), jax.ShapeDtypeStruct((B, 8, D_FF), f32)]
        scratch = [pltpu.VMEM((8, D_FF), f32)]
    scratch += [pltpu.VMEM((FFN_SUBTILES, TOKEN_TILE, D_MODEL), f32), pltpu.VMEM((FFN_SUBTILES, TOKEN_TILE, D_FF), f32),
                pltpu.VMEM((FFN_SUBTILES, TOKEN_TILE, D_FF), f32)]
    return pl.pallas_call(
        functools.partial(_merge_ffn_kernel, sample=sample),
        grid=(B, nt),
        in_specs=in_specs,
        out_specs=out_specs,
        out_shape=out_shape,
        scratch_shapes=scratch,
        compiler_params=_cparams(2),
        name="merge_ffn_sample" if sample else "merge_ffn",
    )(*args)


def _store_heads_transposed(ot_ref, out_ref):
    out_ref[0] = ot_ref[...].T.astype(out_ref.dtype)


def _mem_scores(qm_ref, mk_ref, ms_ref):
    tm = qm_ref.shape[2]
    for p in range(MEM_W // LANES):
        ms_ref[p] = _dot_nt(mk_ref[0, p], qm_ref[0, 2 * p:2 * p + 2].reshape(2 * tm, LANES))


def _mem_consume(ms_ref, mvt_ref, omt_ref, om_ref):
    tm = om_ref.shape[1]
    for h in range(MEM_HEADS):
        s = ms_ref[h // 2, :, (h % 2) * tm:(h % 2 + 1) * tm]
        m = jnp.max(s, axis=0, keepdims=True)
        e = jnp.exp2(s - m).astype(bf16)
        ot = _dot(mvt_ref[0, h], e)
        omt_ref[h * HEAD_DIM:(h + 1) * HEAD_DIM, :] = ot[:HEAD_DIM, :] / ot[HEAD_DIM:HEAD_DIM + 1, :]
    _store_heads_transposed(omt_ref, om_ref)


def _mem_specs(tm):
    in_specs = [
        pl.BlockSpec((1, MEM_HEADS, tm, LANES), lambda b, i: (b, 0, i, 0)),
        pl.BlockSpec((1, MEM_W // LANES, N_MEM, LANES), lambda b, i: (b, 0, 0, 0)),
        pl.BlockSpec((1, MEM_HEADS, V_ROWS, N_MEM), lambda b, i: (b, 0, 0, 0)),
    ]
    out_spec = pl.BlockSpec((1, tm, MEM_W), lambda b, i: (b, i, 0))
    scratch = [pltpu.VMEM((MEM_W, tm), f32), pltpu.VMEM((MEM_W // LANES, N_MEM, 2 * tm), f32)]
    return in_specs, out_spec, scratch


def _swa_prompt_kernel(q_ref, kc_ref, kp_ref, vc_ref, vp_ref, sink_ref, qm_ref, mk_ref, mvt_ref, out_ref, om_ref,
                       ot_ref, sc_ref, sp_ref, omt_ref, ms_ref):
    i = pl.program_id(1)
    tm = TOKEN_TILE
    n_pairs = KV_W // LANES
    per_pair = N_HEADS // n_pairs
    key_c = lax.broadcasted_iota(jnp.int32, (tm, tm), 0)
    key_p = lax.broadcasted_iota(jnp.int32, (WINDOW, tm), 0)
    qc = lax.broadcasted_iota(jnp.int32, (1, tm), 1)
    valid_c = (key_c <= qc) & (key_c > qc - WINDOW)
    valid_p = (key_p > qc) & (i > 0)
    for p in range(n_pairs):
        qp = q_ref[0, 2 * p:2 * p + 2].reshape(per_pair * tm, LANES)
        sc_ref[p] = _dot_nt(kc_ref[0, p, 0], qp)
        sp_ref[p] = _dot_nt(kp_ref[0, p, 0], qp)
    _mem_scores(qm_ref, mk_ref, ms_ref)
    for c in range(N_HEADS):
        h, g = divmod(c, GROUP)
        cols = slice((c % per_pair) * tm, (c % per_pair + 1) * tm)
        s_c = jnp.where(valid_c, sc_ref[c // per_pair, :, cols], NEG_INF)
        s_p = jnp.where(valid_p, sp_ref[c // per_pair, :, cols], NEG_INF)
        sink = sink_ref[h, :, g * tm:(g + 1) * tm]
        m = jnp.maximum(jnp.maximum(jnp.max(s_c, axis=0, keepdims=True), jnp.max(s_p, axis=0, keepdims=True)), sink)
        e_c = jnp.exp2(s_c - m)
        e_p = jnp.exp2(s_p - m)
        ot = _dot(vc_ref[0, h, 0], e_c.astype(bf16)) + _dot(vp_ref[0, h, 0], e_p.astype(bf16))
        den = ot[HEAD_DIM:HEAD_DIM + 1, :] + jnp.exp2(sink - m)
        ot_ref[c * HEAD_DIM:(c + 1) * HEAD_DIM, :] = ot[:HEAD_DIM, :] / den
    _store_heads_transposed(ot_ref, out_ref)
    _mem_consume(ms_ref, mvt_ref, omt_ref, om_ref)


def _swa_prompt(qpad, k5, vt5, sink_cols, qmpad, mk2, mvt):
    B = qpad.shape[0]
    nt = k5.shape[2]
    tm = TOKEN_TILE
    T = nt * tm
    prev = lambda i: jnp.maximum(i - 1, 0)
    mem_in, mem_out, mem_scratch = _mem_specs(tm)
    return pl.pallas_call(
        _swa_prompt_kernel,
        grid=(B, nt),
        in_specs=[
            pl.BlockSpec((1, N_KV_HEADS, GROUP * tm, LANES), lambda b, i: (b, 0, i, 0)),
            pl.BlockSpec((1, KV_W // LANES, 1, tm, LANES), lambda b, i: (b, 0, i, 0, 0)),
            pl.BlockSpec((1, KV_W // LANES, 1, WINDOW, LANES), lambda b, i: (b, 0, prev(i), tm // WINDOW - 1, 0)),
            pl.BlockSpec((1, N_KV_HEADS, 1, V_ROWS, tm), lambda b, i: (b, 0, i, 0, 0)),
            pl.BlockSpec((1, N_KV_HEADS, 1, V_ROWS, WINDOW), lambda b, i: (b, 0, prev(i), 0, tm // WINDOW - 1)),
            pl.BlockSpec((N_KV_HEADS, 1, GROUP * tm), lambda b, i: (0, 0, 0)),
        ] + mem_in,
        out_specs=[pl.BlockSpec((1, tm, Q_W), lambda b, i: (b, i, 0)), mem_out],
        out_shape=[jax.ShapeDtypeStruct((B, T, Q_W), bf16), jax.ShapeDtypeStruct((B, T, MEM_W), bf16)],
        scratch_shapes=[pltpu.VMEM((Q_W, tm), f32),
                        pltpu.VMEM((KV_W // LANES, tm, N_HEADS * tm * LANES // KV_W), f32),
                        pltpu.VMEM((KV_W // LANES, WINDOW, N_HEADS * tm * LANES // KV_W), f32)] + mem_scratch,
        compiler_params=_cparams(2),
        name="swa_prompt",
    )(qpad, k5, k5, vt5, vt5, sink_cols, qmpad, mk2, mvt)


def _top3_bias(gate, eligible, axis):
    idx = lax.broadcasted_iota(jnp.int32, gate.shape, axis)
    n = gate.shape[axis]
    rest = jnp.where(eligible, gate, NEG_INF)
    chosen = jnp.zeros(gate.shape, dtype=jnp.bool_)
    for _ in range(MOBA_TOPK):
        best = jnp.max(rest, axis=axis, keepdims=True)
        first = jnp.min(jnp.where(rest == best, idx, n), axis=axis, keepdims=True)
        pick = (idx == first) & (best > NEG_INF)
        chosen = chosen | pick
        rest = jnp.where(pick, NEG_INF, rest)
    return jnp.where(chosen, 0.0, NEG_INF)


def _moba_gate_bias(q, km, block):
    nb = km.shape[0]
    q_hi = q.astype(bf16)
    q_lo = (q - q_hi.astype(f32)).astype(bf16)
    lane = lax.broadcasted_iota(jnp.int32, (1, LANES), 1)
    lo = lane < HEAD_DIM
    blk = lax.broadcasted_iota(jnp.int32, (nb, 1), 0)
    out = []
    for h in range(N_KV_HEADS):
        kt = km[:, (h // 2) * LANES:(h // 2 + 1) * LANES]
        gates = []
        for g in range(GROUP):
            hq = h * GROUP + g
            kk = kt if hq % 2 == h % 2 else pltpu.roll(kt, HEAD_DIM, 1)
            kk = jnp.where(lo if hq % 2 == 0 else jnp.logical_not(lo), kk, 0.0)
            kk_hi = kk.astype(bf16)
            kk_lo = (kk - kk_hi.astype(f32)).astype(bf16)
            cols = slice((hq // 2) * LANES, (hq // 2 + 1) * LANES)
            gates.append(_dot_nt(kk_hi, q_hi[:, cols]) + _dot_nt(kk_hi, q_lo[:, cols])
                         + _dot_nt(kk_lo, q_hi[:, cols]))
        gate = jnp.concatenate(gates, axis=1)
        out.append(_top3_bias(gate, blk < block, 0))
    return out


def _moba_prompt_kernel(q_ref, k_ref, vt_ref, bias_ref, qm_ref, mk_ref, mvt_ref, out_ref, om_ref,
                        acc_ref, m_ref, sa_ref, sb_ref, omt_ref, ms_ref):
    i = pl.program_id(1)
    tm = TOKEN_TILE
    nb = k_ref.shape[2]
    n_pairs = KV_W // LANES
    per_pair = N_HEADS // n_pairs
    key = lax.broadcasted_iota(jnp.int32, (tm, tm), 0)
    qc = lax.broadcasted_iota(jnp.int32, (1, tm), 1)
    causal = key <= qc

    def pair_scores_into(j, s_ref, p):
        qp = q_ref[0, 2 * p:2 * p + 2].reshape(per_pair * tm, LANES)
        s_ref[p] = _dot_nt(k_ref[0, p, j], qp)

    def scores_into(j, s_ref):
        for p in range(n_pairs):
            pair_scores_into(j, s_ref, p)

    def chunk(s_ref, c):
        return s_ref[c // per_pair, :, (c % per_pair) * tm:(c % per_pair + 1) * tm]

    scores_into(i, sb_ref)
    scores_into(0, sa_ref)
    _mem_scores(qm_ref, mk_ref, ms_ref)
    for c in range(N_HEADS):
        s = jnp.where(causal, chunk(sb_ref, c), NEG_INF)
        m = jnp.max(s, axis=0, keepdims=True)
        e = jnp.exp2(s - m)
        m_ref[c] = m
        acc_ref[c] = _dot(vt_ref[0, c // GROUP, i], e.astype(bf16))
    _mem_consume(ms_ref, mvt_ref, omt_ref, om_ref)

    def consume(j, s_ref, heads=range(N_HEADS)):
        for c in heads:
            h, g = divmod(c, GROUP)
            s = chunk(s_ref, c)
            bias = bias_ref[0, 0, h, pl.ds(j, 1), g * tm:(g + 1) * tm]
            m = m_ref[c]
            m_new = jnp.maximum(m, jnp.max(s, axis=0, keepdims=True) + bias)
            a = jnp.exp2(m - m_new)
            e = jnp.exp2(s - (m_new - bias))
            m_ref[c] = m_new
            acc_ref[c] = a * acc_ref[c] + _dot(vt_ref[0, h, j], e.astype(bf16))

    def body(jj, carry):
        j = 2 * jj
        j2 = jnp.minimum(j + 2, nb - 1)
        for p in range(n_pairs):
            pair_scores_into(j + 1, sb_ref, p)
            consume(j, sa_ref, range(p * per_pair, (p + 1) * per_pair))
        for p in range(n_pairs):
            pair_scores_into(j2, sa_ref, p)
            consume(j + 1, sb_ref, range(p * per_pair, (p + 1) * per_pair))
        return carry

    lax.fori_loop(0, (i + 1) // 2, body, 0)
    heads = [acc_ref[c, :HEAD_DIM, :] / acc_ref[c, HEAD_DIM:HEAD_DIM + 1, :] for c in range(N_HEADS)]
    out_ref[0] = jnp.concatenate(heads, axis=0).T.astype(out_ref.dtype)


def _moba_prompt(qpad, k5, vt5, bias, qmpad, mk2, mvt):
    B = qpad.shape[0]
    nb = k5.shape[2]
    tm = TOKEN_TILE
    T = nb * tm
    mem_in, mem_out, mem_scratch = _mem_specs(tm)
    return pl.pallas_call(
        _moba_prompt_kernel,
        grid=(B, nb),
        in_specs=[
            pl.BlockSpec((1, N_KV_HEADS, GROUP * tm, LANES), lambda b, i: (b, 0, i, 0)),
            pl.BlockSpec((1, KV_W // LANES, nb, tm, LANES), lambda b, i: (b, 0, 0, 0, 0)),
            pl.BlockSpec((1, N_KV_HEADS, nb, V_ROWS, tm), lambda b, i: (b, 0, 0, 0, 0)),
            pl.BlockSpec((1, 1, N_KV_HEADS, nb, GROUP * tm), lambda b, i: (b, i, 0, 0, 0)),
        ] + mem_in,
        out_specs=[pl.BlockSpec((1, tm, Q_W), lambda b, i: (b, i, 0)), mem_out],
        out_shape=[jax.ShapeDtypeStruct((B, T, Q_W), bf16), jax.ShapeDtypeStruct((B, T, MEM_W), bf16)],
        scratch_shapes=[pltpu.VMEM((N_HEADS, V_ROWS, tm), f32), pltpu.VMEM((N_HEADS, 1, tm), f32),
                        pltpu.VMEM((KV_W // LANES, tm, N_HEADS * tm * LANES // KV_W), f32),
                        pltpu.VMEM((KV_W // LANES, tm, N_HEADS * tm * LANES // KV_W), f32)] + mem_scratch,
        compiler_params=_cparams(2),
        name="moba_prompt",
    )(qpad, k5, vt5, bias, qmpad, mk2, mvt)


SAMPLE_ROWS = N_HEADS * 4
NEW_PAD = 8
SEQ_GROUP = 8


def _pad_rows(a, n):
    return jnp.concatenate([a, jnp.zeros((n - a.shape[0], a.shape[1]), a.dtype)], axis=0)


def _swa_sample_kernel(q_ref, kc_ref, vc_ref, kn_ref, vn_ref, sink_ref, o_ref, nk_ref, nv_ref, s_ref):
    wb = kc_ref.shape[2]
    col = lax.broadcasted_iota(jnp.int32, (SAMPLE_ROWS, wb), 1)
    s_idx = lax.broadcasted_iota(jnp.int32, (SAMPLE_ROWS, 1), 0) % 4
    valid_c = col > s_idx
    valid_n = col <= s_idx
    lane = lax.broadcasted_iota(jnp.int32, (1, wb), 1)
    sink = sink_ref[...]
    for n in range(SEQ_GROUP):
        q = q_ref[n]
        kn = _pad_rows(kn_ref[n], wb)
        s_ref[n, :, :wb] = _dot(q, kc_ref[n].astype(bf16))
        s_ref[n, :, wb:] = _dot_nt(q, kn.astype(bf16))
    for n in range(SEQ_GROUP):
        vn = _pad_rows(vn_ref[n], wb)
        s_c = jnp.where(valid_c, s_ref[n, :, :wb], NEG_INF)
        s_n = jnp.where(valid_n, s_ref[n, :, wb:], NEG_INF)
        m = jnp.maximum(jnp.maximum(jnp.max(s_c, axis=1, keepdims=True), jnp.max(s_n, axis=1, keepdims=True)), sink)
        e_c = jnp.exp(s_c - m)
        e_n = jnp.exp(s_n - m)
        den = jnp.sum(e_c, axis=1, keepdims=True) + jnp.sum(e_n, axis=1, keepdims=True) + jnp.exp(sink - m)
        o = _dot_nt(e_c.astype(bf16), vc_ref[n].astype(bf16)) + _dot(e_n.astype(bf16), vn.astype(bf16))
        o_ref[n] = o / den
    for n in range(SEQ_GROUP):
        kn = _pad_rows(kn_ref[n], wb)
        vn = _pad_rows(vn_ref[n], wb)
        nk_ref[n] = jnp.where(lane < wb - 4, pltpu.roll(kc_ref[n], wb - 4, 1), pltpu.roll(kn.T, wb - 4, 1))
        nv_ref[n] = jnp.where(lane < wb - 4, pltpu.roll(vc_ref[n], wb - 4, 1), pltpu.roll(vn.T, wb - 4, 1))


def _swa_sample(qbd, kc, vc, kn, vn, sink_rows):
    n_seq, _, wb = kc.shape
    sg = SEQ_GROUP
    seq = lambda r, w: pl.BlockSpec((sg, r, w), lambda i: (i, 0, 0))
    return pl.pallas_call(
        _swa_sample_kernel,
        grid=(n_seq // sg,),
        in_specs=[seq(SAMPLE_ROWS, KV_W), seq(KV_W, wb), seq(KV_W, wb), seq(NEW_PAD, KV_W), seq(NEW_PAD, KV_W),
                  pl.BlockSpec((SAMPLE_ROWS, 1), lambda i: (0, 0))],
        out_specs=[seq(SAMPLE_ROWS, KV_W), seq(KV_W, wb), seq(KV_W, wb)],
        out_shape=[jax.ShapeDtypeStruct((n_seq, SAMPLE_ROWS, KV_W), f32),
                   jax.ShapeDtypeStruct((n_seq, KV_W, wb), f32),
                   jax.ShapeDtypeStruct((n_seq, KV_W, wb), f32)],
        scratch_shapes=[pltpu.VMEM((sg, SAMPLE_ROWS, 2 * wb), f32)],
        compiler_params=_cparams(1),
        name="swa_sample",
    )(qbd, kc, vc, kn, vn, sink_rows)


def _mem_sample_kernel(q_ref, k_ref, v_ref, o_ref):
    for n in range(SEQ_GROUP):
        s = _dot(q_ref[n], k_ref[0, n].astype(bf16))
        m = jnp.max(s, axis=1, keepdims=True)
        e = jnp.exp(s - m)
        den = jnp.sum(e, axis=1, keepdims=True)
        o_ref[n] = _dot_nt(e.astype(bf16), v_ref[0, n].astype(bf16)) / den


def _mem_sample(qbd, km, vm, layer):
    n_seq = km.shape[1]
    sg = SEQ_GROUP
    rows = qbd.shape[1]
    cache = pl.BlockSpec((1, sg, MEM_W, N_MEM), lambda i: (layer, i, 0, 0))
    return pl.pallas_call(
        _mem_sample_kernel,
        grid=(n_seq // sg,),
        in_specs=[pl.BlockSpec((sg, rows, MEM_W), lambda i: (i, 0, 0)), cache, cache],
        out_specs=pl.BlockSpec((sg, rows, MEM_W), lambda i: (i, 0, 0)),
        out_shape=jax.ShapeDtypeStruct((n_seq, rows, MEM_W), f32),
        compiler_params=_cparams(1),
        name="mem_sample",
    )(qbd, km, vm)


def _moba_sample_kernel(pt_ref, q_ref, qf_ref, kn_ref, vn_ref, kpool_ref, vpool_ref, o_ref,
                        kbuf, vbuf, sem, s_ref, e_ref):
    b = pl.program_id(0)
    n_seq = pl.num_programs(0)
    n_pages = pt_ref.shape[1]
    page = kpool_ref.shape[2]
    past = n_pages * page
    nb = past // MOBA_BLOCK
    slot = b % 2

    def page_copy(which, seq, slot_, pg):
        pool, buf = ((kpool_ref, kbuf), (vpool_ref, vbuf))[which]
        return pltpu.make_async_copy(pool.at[pt_ref[seq, pg]], buf.at[slot_, :, pg * page:(pg + 1) * page],
                                     sem.at[which, slot_])

    def fetch(seq, slot_):
        for pg in range(n_pages):
            page_copy(0, seq, slot_, pg).start()
        for pg in range(n_pages):
            page_copy(1, seq, slot_, pg).start()

    def wait(which, seq, slot_):
        for pg in range(n_pages):
            page_copy(which, seq, slot_, pg).wait()

    @pl.when(b == 0)
    def _():
        fetch(0, 0)

    @pl.when(b + 1 < n_seq)
    def _():
        fetch(b + 1, 1 - slot)

    wait(0, b, slot)
    q = q_ref[0]
    s_ref[...] = _dot(q, kbuf[slot].astype(bf16))
    lane = lax.broadcasted_iota(jnp.int32, (1, LANES), 1)
    kmean = jnp.zeros((KV_W, LANES), f32)
    for j in range(nb):
        part = kbuf[slot, :, j * MOBA_BLOCK:j * MOBA_BLOCK + LANES]
        for c in range(1, MOBA_BLOCK // LANES):
            part = part + kbuf[slot, :, j * MOBA_BLOCK + c * LANES:j * MOBA_BLOCK + (c + 1) * LANES]
        kmean = jnp.where(lane == j, jnp.sum(part, axis=1, keepdims=True), kmean)
    kmean = kmean * (1.0 / MOBA_BLOCK)
    gate = jnp.dot(qf_ref[0], kmean, preferred_element_type=f32, precision=lax.Precision.HIGHEST)
    bias = _top3_bias(gate, lane < nb, 1)
    col = lax.broadcasted_iota(jnp.int32, (SAMPLE_ROWS, LANES), 1)
    s_idx = lax.broadcasted_iota(jnp.int32, (SAMPLE_ROWS, 1), 0) % 4
    kn = _pad_rows(kn_ref[0], LANES).astype(bf16)
    vn = _pad_rows(vn_ref[0], LANES).astype(bf16)
    s_n = jnp.where(col <= s_idx, _dot_nt(q, kn), NEG_INF)
    mx = jnp.full((SAMPLE_ROWS, MOBA_BLOCK), NEG_INF, f32)
    for j in range(nb):
        mx = jnp.maximum(mx, s_ref[:, j * MOBA_BLOCK:(j + 1) * MOBA_BLOCK] + bias[:, j:j + 1])
    m = jnp.maximum(jnp.max(mx, axis=1, keepdims=True), jnp.max(s_n, axis=1, keepdims=True))
    acc = jnp.zeros((SAMPLE_ROWS, MOBA_BLOCK), f32)
    for j in range(nb):
        e = jnp.exp(s_ref[:, j * MOBA_BLOCK:(j + 1) * MOBA_BLOCK] - (m - bias[:, j:j + 1]))
        acc = acc + e
        e_ref[:, j * MOBA_BLOCK:(j + 1) * MOBA_BLOCK] = e.astype(bf16)
    e_n = jnp.exp(s_n - m)
    den = jnp.sum(acc, axis=1, keepdims=True) + jnp.sum(e_n, axis=1, keepdims=True)
    wait(1, b, slot)
    o = _dot_nt(e_ref[...], vbuf[slot].astype(bf16)) + _dot(e_n.astype(bf16), vn)
    o_ref[0] = o / den


def _moba_sample(page_table, qbd, qbd_f32, kn, vn, kpool, vpool):
    n_seq, n_pages = page_table.shape
    page = kpool.shape[2]
    past = n_pages * page
    seq = lambda r: pl.BlockSpec((1, r, KV_W), lambda i, pt: (i, 0, 0))
    grid_spec = pltpu.PrefetchScalarGridSpec(
        num_scalar_prefetch=1,
        grid=(n_seq,),
        in_specs=[seq(SAMPLE_ROWS), seq(SAMPLE_ROWS), seq(NEW_PAD), seq(NEW_PAD),
                  pl.BlockSpec(memory_space=pl.ANY), pl.BlockSpec(memory_space=pl.ANY)],
        out_specs=seq(SAMPLE_ROWS),
        scratch_shapes=[pltpu.VMEM((2, KV_W, past), f32), pltpu.VMEM((2, KV_W, past), f32),
                        pltpu.SemaphoreType.DMA((2, 2)),
                        pltpu.VMEM((SAMPLE_ROWS, past), f32), pltpu.VMEM((SAMPLE_ROWS, past), bf16)],
    )
    return pl.pallas_call(
        _moba_sample_kernel,
        grid_spec=grid_spec,
        out_shape=jax.ShapeDtypeStruct((n_seq, SAMPLE_ROWS, KV_W), f32),
        compiler_params=_cparams(1),
        name="moba_sample",
    )(page_table, qbd, qbd_f32, kn, vn, kpool, vpool)


def _block_diag_rows(a, n_heads, per_kv):
    n_seq, S, _ = a.shape
    n_kv = n_heads // per_kv
    a = a.reshape(n_seq, S, n_kv, per_kv, HEAD_DIM).transpose(0, 2, 3, 1, 4)
    eye = jnp.eye(n_kv, dtype=a.dtype)
    bd = a[:, :, :, :, None, :] * eye[None, :, None, None, :, None]
    return bd.reshape(n_seq, n_heads * S, n_kv * HEAD_DIM)


def _diag_rows_out(o, n_heads, per_kv, S):
    n_seq = o.shape[0]
    n_kv = n_heads // per_kv
    o = o.reshape(n_seq, n_kv, per_kv, S, n_kv, HEAD_DIM)
    o = jnp.stack([o[:, h, :, :, h, :] for h in range(n_kv)], axis=1)
    return o.transpose(0, 3, 1, 2, 4).reshape(n_seq, S, n_heads * HEAD_DIM)


def _stacked_weights(w_o, ln1_g, ln1_b, w_ffn_gate, conv_w, conv_b, w_ffn_up, w_ffn_down, ln2_g, ln2_b):
    row = lambda a: a[:, None, :]
    return dict(
        w_o=w_o.astype(bf16), g1=row(ln1_g), b1=row(ln1_b),
        w_gate=w_ffn_gate.astype(bf16), conv_w=conv_w, conv_b=row(conv_b),
        w_up=w_ffn_up.astype(bf16), w_down=w_ffn_down.astype(bf16),
        g2=row(ln2_g), b2=row(ln2_b),
    )


def _conv_injection(state):
    n_seq = state.shape[0]
    return jnp.pad(state, ((0, 0), (0, 2), (0, 0))).reshape(1, n_seq * 4, D_FF)


def kernel(x_prompt, x_sample, cache_win_k, cache_win_v, cache_moba_k, cache_moba_v, page_table, cache_mem_k, cache_mem_v, state_conv, mem_prompt, w_in_a, sink_a, w_in_b, w_kv_shared, w_mem_kv, w_o, ln1_g, ln1_b, w_ffn_gate, conv_w, conv_b, w_ffn_up, w_ffn_down, ln2_g, ln2_b):
    B, T, _ = x_prompt.shape
    Bd, S, _ = x_sample.shape
    n_pool, page = cache_moba_k.shape[:2]
    past = page_table.shape[1] * page
    wb = cache_win_k.shape[2]
    nb = T // TOKEN_TILE

    cos_p, sin_p = _rope_tables(jnp.arange(T, dtype=jnp.int32))
    cos_s, sin_s = _rope_tables(past + (jnp.arange(Bd * S, dtype=jnp.int32) % S))
    w_a = w_in_a[0].astype(bf16)
    w_b = jnp.concatenate([w_in_b[0][:, :Q_W], w_kv_shared, w_in_b[0][:, Q_W:]], axis=1).astype(bf16)
    lw = _stacked_weights(w_o, ln1_g, ln1_b, w_ffn_gate, conv_w, conv_b, w_ffn_up, w_ffn_down, ln2_g, ln2_b)
    xs = x_sample.reshape(1, Bd * S, D_MODEL)

    def feature_major(a):
        a = jnp.moveaxis(a, -3, -1)
        return a.reshape(a.shape[:-3] + (a.shape[-3] * HEAD_DIM, a.shape[-1]))

    def token_major(a, n_heads):
        a = a.reshape(a.shape[:-2] + (n_heads, HEAD_DIM, a.shape[-1]))
        return jnp.moveaxis(a, -1, -3)

    mem_kvt, mem_k2, mem_vt = _memkv(mem_prompt, w_mem_kv.astype(bf16))
    mem_k_out = token_major(mem_kvt[:, :, :MEM_W], MEM_HEADS)
    mem_v_out = token_major(mem_kvt[:, :, MEM_W:], MEM_HEADS)
    cache_mem_kt = feature_major(cache_mem_k)
    cache_mem_vt = feature_major(cache_mem_v)

    def sample_rows(a, n_heads, per_kv):
        return _block_diag_rows(a.reshape(Bd, S, n_heads * HEAD_DIM), n_heads, per_kv)

    def pad_new(a):
        return jnp.pad(a.reshape(Bd, S, KV_W), ((0, 0), (0, NEW_PAD - S), (0, 0)))

    def sample_mem(qm, layer):
        qbd = (sample_rows(qm, MEM_HEADS, 1) * SCALE).astype(bf16)
        o = _mem_sample(qbd, cache_mem_kt, cache_mem_vt, layer)
        return _diag_rows_out(o, MEM_HEADS, 1, S).reshape(1, Bd * S, MEM_W).astype(bf16)

    kat, vat, qpad, qmpad, k5, vt5, _ = _proj(x_prompt, w_a, cos_p, sin_p, True, want_q=False)
    sink_cols = jnp.repeat(sink_a[0].reshape(N_KV_HEADS, GROUP) * LOG2E, TOKEN_TILE, axis=1)[:, None, :]
    o_self, o_mem = _swa_prompt(qpad, k5, vt5, sink_cols, qmpad, mem_k2[0], mem_vt[0])
    xp, tail0 = _merge_ffn(x_prompt, o_self, o_mem, lw, 0)
    win_kp = token_major(kat[:, :, T - wb:], N_KV_HEADS)[None]
    win_vp = token_major(vat[:, :, T - wb:], N_KV_HEADS)[None]

    qs, ks, vs, qms = _proj(xs, w_a, cos_s, sin_s, False)
    qbd = (sample_rows(qs, N_HEADS, GROUP) * SCALE).astype(bf16)
    sink_rows = jnp.repeat(sink_a[0], S)[:, None]
    o_s, win_kst, win_vst = _swa_sample(qbd, feature_major(cache_win_k[0]), feature_major(cache_win_v[0]),
                                        pad_new(ks), pad_new(vs), sink_rows)
    os_self = _diag_rows_out(o_s, N_HEADS, GROUP, S).reshape(1, Bd * S, Q_W).astype(bf16)
    xs, u0 = _merge_ffn(xs, os_self, sample_mem(qms, 0), lw, 0, _conv_injection(state_conv[0]))

    kbt, vbt, qpad, qmpad, k5, vt5, _, bias = _proj(xp, w_b, cos_p, sin_p, True, want_q=False, want_gate=True)
    o_self, o_mem = _moba_prompt(qpad, k5, vt5, bias, qmpad, mem_k2[1], mem_vt[1])
    yp, tail1 = _merge_ffn(xp, o_self, o_mem, lw, 1)

    qs, ks, vs, qms = _proj(xs, w_b, cos_s, sin_s, False)
    qbd_f32 = sample_rows(qs, N_HEADS, GROUP)
    o_s = _moba_sample(page_table, (qbd_f32 * SCALE).astype(bf16), qbd_f32, pad_new(ks), pad_new(vs),
                       feature_major(cache_moba_k), feature_major(cache_moba_v))
    os_self = _diag_rows_out(o_s, N_HEADS, GROUP, S).reshape(1, Bd * S, Q_W).astype(bf16)
    ys, u1 = _merge_ffn(xs, os_self, sample_mem(qms, 1), lw, 1, _conv_injection(state_conv[1]))

    kv_heads = lambda a, lead: a.reshape(lead + (N_KV_HEADS, HEAD_DIM))
    conv_p = jnp.stack([tail0[:, 8 - (CONV_W - 1):], tail1[:, 8 - (CONV_W - 1):]])
    conv_s = jnp.stack([u.reshape(Bd, S, D_FF)[:, S - (CONV_W - 1):] for u in (u0, u1)])
    return (
        yp, ys.reshape(Bd, S, D_MODEL),
        win_kp, win_vp,
        token_major(kbt, N_KV_HEADS), token_major(vbt, N_KV_HEADS),
        mem_k_out, mem_v_out,
        conv_p,
        token_major(win_kst, N_KV_HEADS)[None], token_major(win_vst, N_KV_HEADS)[None],
        kv_heads(ks, (Bd, S)), kv_heads(vs, (Bd, S)),
        conv_s,
    )
```

```python
import functools
import math

import jax
import jax.numpy as jnp
from jax import lax
from jax.experimental import pallas as pl
from jax.experimental.pallas import tpu as pltpu

D_MODEL = 1024
HEAD_DIM = 64
N_HEADS = 12
N_KV_HEADS = 4
GROUP = N_HEADS // N_KV_HEADS
MEM_HEADS = 4
N_MEM = 256
Q_W = N_HEADS * HEAD_DIM
KV_W = N_KV_HEADS * HEAD_DIM
MEM_W = MEM_HEADS * HEAD_DIM
PROJ_W = Q_W + 2 * KV_W + MEM_W
WINDOW = 128
MOBA_BLOCK = 256
MOBA_TOPK = 3
D_FF = 2816
CONV_W = 3
ROPE_THETA = 10000.0
LN_EPS = 1e-5
DEPTH = 2
ALPHA = (2 * DEPTH) ** 0.25
SCALE = HEAD_DIM ** -0.5
LOG2E = math.log2(math.e)
SCALE_LOG2 = SCALE * LOG2E

LANES = 128
V_ROWS = HEAD_DIM + 16
TOKEN_TILE = 256
VMEM_LIMIT = 56 * 1024 * 1024
NEG_INF = float("-inf")

f32 = jnp.float32
bf16 = jnp.bfloat16


def _cparams(n_grid, vmem=VMEM_LIMIT, flags=None):
    return pltpu.CompilerParams(dimension_semantics=("arbitrary",) * n_grid, vmem_limit_bytes=vmem, flags=flags)


def _dot_nt(a, b, precision=None):
    return lax.dot_general(a, b, (((1,), (1,)), ((), ())), preferred_element_type=f32, precision=precision)


def _dot(a, b):
    return jnp.dot(a, b, preferred_element_type=f32)


def _layer_norm(r, g, b):
    mu = jnp.mean(r, axis=-1, keepdims=True)
    d = r - mu
    var = jnp.mean(d * d, axis=-1, keepdims=True)
    return d * lax.rsqrt(var + LN_EPS) * g + b


def _ones_row_block(width):
    row = lax.broadcasted_iota(jnp.int32, (V_ROWS - HEAD_DIM, width), 0)
    return jnp.where(row == 0, 1.0, 0.0).astype(f32)


def _half_mask(width):
    lane = lax.broadcasted_iota(jnp.int32, (1, width), 1)
    return (lane % HEAD_DIM) < (HEAD_DIM // 2)


PROJ_SUBTILES = 2


def _proj_kernel(x_ref, w_ref, cos_ref, sin_ref, *rest, layouts, want_q, want_gate):
    n_scratch = 2 if want_gate else 1
    outs = list(rest[:-n_scratch])
    y_s = rest[-n_scratch]
    q_ref = outs.pop(0) if want_q else None
    if want_gate:
        kmean_s = rest[-1]
        bias_ref = outs.pop()

        @pl.when((pl.program_id(0) == 0) & (pl.program_id(1) == 0))
        def _():
            kmean_s[...] = jnp.zeros_like(kmean_s)
    tm = TOKEN_TILE
    n_sub = x_ref.shape[1] // tm
    for t in range(n_sub):
        y_s[t] = _dot(x_ref[0, t * tm:(t + 1) * tm, :].astype(bf16), w_ref[...])
    width = Q_W + KV_W
    half = HEAD_DIM // 2
    first_half = _half_mask(width)
    lane = lax.broadcasted_iota(jnp.int32, (1, LANES), 1)
    lo = lane < HEAD_DIM

    def placed(tile, src_half, dst_half):
        t_ = tile if src_half == dst_half else pltpu.roll(tile, HEAD_DIM, 1)
        keep = lo if dst_half == 0 else jnp.logical_not(lo)
        return jnp.where(keep, t_, 0.0)

    for t in range(n_sub):
        rows = slice(t * tm, (t + 1) * tm)
        y = y_s[t]
        qk = y[:, :width]
        cos = jnp.tile(cos_ref[rows, :], (1, width // LANES))
        sin = jnp.tile(sin_ref[rows, :], (1, width // LANES))
        partner = jnp.where(first_half, pltpu.roll(qk, width - half, 1), pltpu.roll(qk, half, 1))
        rot = qk * cos + partner * sin
        q = rot[:, :Q_W]
        k = rot[:, Q_W:]
        v = y[:, Q_W + KV_W:Q_W + 2 * KV_W]
        qm = y[:, Q_W + 2 * KV_W:]
        if want_q:
            q_ref[0, rows, :] = q
        if not layouts:
            k_ref, v_ref, qm_ref = outs
            k_ref[0, rows, :] = k
            v_ref[0, rows, :] = v
            qm_ref[0, rows, :] = qm
            continue
        k_ref, v_ref, qpad_ref, qmpad_ref, k5_ref, vt5_ref, kmean_ref = outs
        kt = k.T
        vt = v.T
        k_ref[0, :, rows] = kt
        v_ref[0, :, rows] = vt
        for h in range(N_KV_HEADS):
            for g in range(GROUP):
                hq = h * GROUP + g
                tile = q[:, (hq // 2) * LANES:(hq // 2 + 1) * LANES]
                r0 = (t * GROUP + g) * tm
                qpad_ref[0, h, r0:r0 + tm, :] = (placed(tile, hq % 2, h % 2) * SCALE_LOG2).astype(bf16)
        for h in range(MEM_HEADS):
            tile = qm[:, (h // 2) * LANES:(h // 2 + 1) * LANES]
            qmpad_ref[0, h, rows, :] = (placed(tile, h % 2, h % 2) * SCALE_LOG2).astype(bf16)
        for p in range(KV_W // LANES):
            k5_ref[0, p, t] = k[:, p * LANES:(p + 1) * LANES].astype(bf16)
        ones_rows = _ones_row_block(tm)
        for h in range(N_KV_HEADS):
            vt5_ref[0, h, t] = jnp.concatenate([vt[h * HEAD_DIM:(h + 1) * HEAD_DIM, :], ones_rows], axis=0).astype(bf16)
        k_mean = jnp.mean(k, axis=0, keepdims=True)
        kmean_ref[0, t] = k_mean
        if want_gate:
            block = pl.program_id(1) * n_sub + t
            kmean_s[pl.ds(block, 1), :] = k_mean
            for h, bias in enumerate(_moba_gate_bias(q, kmean_s[...], block)):
                bias_ref[0, t, h] = bias


def _proj(x, w, cos, sin, layouts, want_q=True, want_gate=False):
    B, T, _ = x.shape
    ns = PROJ_SUBTILES
    tm = TOKEN_TILE * ns
    nt = T // TOKEN_TILE
    n_tab = cos.shape[0] // tm
    tok = lambda w_: (jax.ShapeDtypeStruct((B, T, w_), f32), pl.BlockSpec((1, tm, w_), lambda b, i: (b, i, 0)))
    feat = lambda w_: (jax.ShapeDtypeStruct((B, w_, T), f32), pl.BlockSpec((1, w_, tm), lambda b, i: (b, 0, i)))
    outs = [tok(Q_W)] if want_q else []
    outs += [feat(KV_W), feat(KV_W)] if layouts else [tok(KV_W), tok(KV_W), tok(MEM_W)]
    out_shape = [o[0] for o in outs]
    out_specs = [o[1] for o in outs]
    if layouts:
        out_shape += [
            jax.ShapeDtypeStruct((B, N_KV_HEADS, GROUP * T, LANES), bf16),
            jax.ShapeDtypeStruct((B, MEM_HEADS, T, LANES), bf16),
            jax.ShapeDtypeStruct((B, KV_W // LANES, nt, TOKEN_TILE, LANES), bf16),
            jax.ShapeDtypeStruct((B, N_KV_HEADS, nt, V_ROWS, TOKEN_TILE), bf16),
            jax.ShapeDtypeStruct((B, nt, 1, KV_W), f32),
        ]
        out_specs += [
            pl.BlockSpec((1, N_KV_HEADS, GROUP * tm, LANES), lambda b, i: (b, 0, i, 0)),
            pl.BlockSpec((1, MEM_HEADS, tm, LANES), lambda b, i: (b, 0, i, 0)),
            pl.BlockSpec((1, KV_W // LANES, ns, TOKEN_TILE, LANES), lambda b, i: (b, 0, i, 0, 0)),
            pl.BlockSpec((1, N_KV_HEADS, ns, V_ROWS, TOKEN_TILE), lambda b, i: (b, 0, i, 0, 0)),
            pl.BlockSpec((1, ns, 1, KV_W), lambda b, i: (b, i, 0, 0)),
        ]
    scratch = [pltpu.VMEM((ns, TOKEN_TILE, PROJ_W), f32)]
    if want_gate:
        out_shape.append(jax.ShapeDtypeStruct((B, nt, N_KV_HEADS, nt, GROUP * TOKEN_TILE), f32))
        out_specs.append(pl.BlockSpec((1, ns, N_KV_HEADS, nt, GROUP * TOKEN_TILE), lambda b, i: (b, i, 0, 0, 0)))
        scratch.append(pltpu.VMEM((nt, KV_W), f32))
    return pl.pallas_call(
        functools.partial(_proj_kernel, layouts=layouts, want_q=want_q, want_gate=want_gate),
        grid=(B, T // tm),
        in_specs=[
            pl.BlockSpec((1, tm, D_MODEL), lambda b, i: (b, i, 0)),
            pl.BlockSpec((D_MODEL, PROJ_W), lambda b, i: (0, 0), pipeline_mode=pl.Buffered(1)),
            pl.BlockSpec((tm, LANES), lambda b, i: (i % n_tab, 0)),
            pl.BlockSpec((tm, LANES), lambda b, i: (i % n_tab, 0)),
        ],
        out_specs=out_specs,
        out_shape=out_shape,
        scratch_shapes=scratch,
        compiler_params=_cparams(2),
        name="proj_layouts" if layouts else "proj",
    )(x, w, cos, sin)


def _rope_tables(pos):
    half = HEAD_DIM // 2
    inv = jnp.power(jnp.float32(ROPE_THETA), -jnp.arange(half, dtype=f32) / half)
    ang = pos.astype(f32)[:, None] * inv[None, :]
    cos = jnp.cos(ang)
    sin = jnp.sin(ang)
    reps = LANES // HEAD_DIM
    return jnp.tile(jnp.concatenate([cos, cos], axis=1), (1, reps)), jnp.tile(jnp.concatenate([-sin, sin], axis=1), (1, reps))


def _memkv_kernel(m_ref, w_ref, kv_ref, k2_ref, vt_ref):
    y = _dot(m_ref[0].astype(bf16), w_ref[0])
    yt = y.T
    kv_ref[0, 0] = yt
    for p in range(MEM_W // LANES):
        k2_ref[0, 0, p] = y[:, p * LANES:(p + 1) * LANES].astype(bf16)
    ones_rows = _ones_row_block(N_MEM)
    for h in range(MEM_HEADS):
        vh = yt[MEM_W + h * HEAD_DIM:MEM_W + (h + 1) * HEAD_DIM, :]
        vt_ref[0, 0, h] = jnp.concatenate([vh, ones_rows], axis=0).astype(bf16)


def _memkv(mem, w):
    B = mem.shape[0]
    return pl.pallas_call(
        _memkv_kernel,
        grid=(DEPTH, B),
        in_specs=[
            pl.BlockSpec((1, N_MEM, D_MODEL), lambda l, b: (b, 0, 0)),
            pl.BlockSpec((1, D_MODEL, 2 * MEM_W), lambda l, b: (l, 0, 0)),
        ],
        out_specs=[
            pl.BlockSpec((1, 1, 2 * MEM_W, N_MEM), lambda l, b: (l, b, 0, 0)),
            pl.BlockSpec((1, 1, MEM_W // LANES, N_MEM, LANES), lambda l, b: (l, b, 0, 0, 0)),
            pl.BlockSpec((1, 1, MEM_HEADS, V_ROWS, N_MEM), lambda l, b: (l, b, 0, 0, 0)),
        ],
        out_shape=[
            jax.ShapeDtypeStruct((DEPTH, B, 2 * MEM_W, N_MEM), f32),
            jax.ShapeDtypeStruct((DEPTH, B, MEM_W // LANES, N_MEM, LANES), bf16),
            jax.ShapeDtypeStruct((DEPTH, B, MEM_HEADS, V_ROWS, N_MEM), bf16),
        ],
        compiler_params=_cparams(2),
        name="memkv",
    )(mem, w)


def _gelu(c):
    return 0.5 * c * (1.0 + lax.erf(c * (1.0 / math.sqrt(2.0))))


def _merge_ffn_kernel(x_ref, os_ref, om_ref, wos_ref, wom_ref, g1_ref, b1_ref, wg_ref, cw_ref, cb_ref, wu_ref,
                      wd_ref, g2_ref, b2_ref, *rest, sample):
    if sample:
        inj_ref, y_ref, u_ref, x1_s, u_s, up_s = rest
    else:
        y_ref, tail_ref, carry_ref, x1_s, u_s, up_s = rest
        @pl.when(pl.program_id(1) == 0)
        def _():
            carry_ref[...] = jnp.zeros_like(carry_ref)
    tm = TOKEN_TILE
    n_sub = x_ref.shape[1] // tm
    for t in range(n_sub):
        rows = slice(t * tm, (t + 1) * tm)
        attn = _dot(os_ref[0, rows, :], wos_ref[0]) + _dot(om_ref[0, rows, :], wom_ref[0])
        x1 = _layer_norm(ALPHA * x_ref[0, rows, :] + attn, g1_ref[0], b1_ref[0])
        x1_s[t] = x1
        x1b = x1.astype(bf16)
        u_s[t] = _dot(x1b, wg_ref[0])
        up_s[t] = _dot(x1b, wu_ref[0])
    row = lax.broadcasted_iota(jnp.int32, (tm, 1), 0)
    cw = cw_ref[0]
    for t in range(n_sub):
        rows = slice(t * tm, (t + 1) * tm)
        u = u_s[t]
        r1 = pltpu.roll(u, 1, 0)
        r2 = pltpu.roll(u, 2, 0)
        if sample:
            s = row % 4
            inj = inj_ref[0, rows, :]
            prev1 = jnp.where(s >= 1, r1, pltpu.roll(inj, tm - 1, 0))
            prev2 = jnp.where(s >= 2, r2, inj)
            u_ref[0, rows, :] = u
        else:
            c6 = carry_ref[6:7, :]
            c7 = carry_ref[7:8, :]
            prev1 = jnp.where(row == 0, c7, r1)
            prev2 = jnp.where(row == 0, c6, jnp.where(row == 1, c7, r2))
            carry_ref[...] = u[tm - 8:, :]
            if t == n_sub - 1:
                tail_ref[0] = u[tm - 8:, :]
        c = cb_ref[0] + prev2 * cw[0:1, :] + prev1 * cw[1:2, :] + u * cw[2:3, :]
        hid = (_gelu(c) * up_s[t]).astype(bf16)
        y_ref[0, rows, :] = _layer_norm(ALPHA * x1_s[t] + _dot(hid, wd_ref[0]), g2_ref[0], b2_ref[0])


FFN_SUBTILES = 2


def _merge_ffn(x, o_self, o_mem, lw, layer, inj=None):
    B, T, _ = x.shape
    tm = TOKEN_TILE * FFN_SUBTILES
    nt = T // tm
    sample = inj is not None
    tok = lambda w: pl.BlockSpec((1, tm, w), lambda b, i: (b, i, 0))
    const = lambda r, c, rb=0: pl.BlockSpec((1, r, c), lambda b, i: (layer, rb, 0), pipeline_mode=pl.Buffered(1))
    in_specs = [
        tok(D_MODEL), tok(Q_W), tok(MEM_W),
        const(Q_W, D_MODEL), const(MEM_W, D_MODEL, Q_W // MEM_W), const(1, D_MODEL), const(1, D_MODEL),
        const(D_MODEL, D_FF), const(CONV_W, D_FF), const(1, D_FF), const(D_MODEL, D_FF),
        const(D_FF, D_MODEL), const(1, D_MODEL), const(1, D_MODEL),
    ]
    args = [x, o_self, o_mem, lw["w_o"], lw["w_o"], lw["g1"], lw["b1"], lw["w_gate"], lw["conv_w"],
            lw["conv_b"], lw["w_up"], lw["w_down"], lw["g2"], lw["b2"]]
    if sample:
        in_specs += [tok(D_FF)]
        args += [inj]
        out_specs = [tok(D_MODEL), tok(D_FF)]
        out_shape = [jax.ShapeDtypeStruct((B, T, D_MODEL), f32), jax.ShapeDtypeStruct((B, T, D_FF), f32)]
        scratch = []
    else:
        out_specs = [tok(D_MODEL), pl.BlockSpec((1, 8, D_FF), lambda b, i: (b, 0, 0))]
        out_shape = [jax.ShapeDtypeStruct((B, T, D_MODEL), f32), jax.ShapeDtypeStruct((B, 8, D_FF), f32)]
        scratch = [pltpu.VMEM((8, D_FF), f32)]
    scratch += [pltpu.VMEM((FFN_SUBTILES, TOKEN_TILE, D_MODEL), f32), pltpu.VMEM((FFN_SUBTILES, TOKEN_TILE, D_FF), f32),
                pltpu.VMEM((FFN_SUBTILES, TOKEN_TILE, D_FF), f32)]
    return pl.pallas_call(
        functools.partial(_merge_ffn_kernel, sample=sample),
        grid=(B, nt),
        in_specs=in_specs,
        out_specs=out_specs,
        out_shape=out_shape,
        scratch_shapes=scratch,
        compiler_params=_cparams(2),
        name="merge_ffn_sample" if sample else "merge_ffn",
    )(*args)


def _store_heads_transposed(ot_ref, out_ref):
    out_ref[0] = ot_ref[...].T.astype(out_ref.dtype)


def _mem_scores(qm_ref, mk_ref, ms_ref):
    tm = qm_ref.shape[2]
    for p in range(MEM_W // LANES):
        ms_ref[p] = _dot_nt(mk_ref[0, p], qm_ref[0, 2 * p:2 * p + 2].reshape(2 * tm, LANES))


def _mem_consume(ms_ref, mvt_ref, omt_ref, om_ref):
    tm = om_ref.shape[1]
    for h in range(MEM_HEADS):
        s = ms_ref[h // 2, :, (h % 2) * tm:(h % 2 + 1) * tm]
        m = jnp.max(s, axis=0, keepdims=True)
        e = jnp.exp2(s - m).astype(bf16)
        ot = _dot(mvt_ref[0, h], e)
        omt_ref[h * HEAD_DIM:(h + 1) * HEAD_DIM, :] = ot[:HEAD_DIM, :] / ot[HEAD_DIM:HEAD_DIM + 1, :]
    _store_heads_transposed(omt_ref, om_ref)


def _mem_specs(tm):
    in_specs = [
        pl.BlockSpec((1, MEM_HEADS, tm, LANES), lambda b, i: (b, 0, i, 0)),
        pl.BlockSpec((1, MEM_W // LANES, N_MEM, LANES), lambda b, i: (b, 0, 0, 0)),
        pl.BlockSpec((1, MEM_HEADS, V_ROWS, N_MEM), lambda b, i: (b, 0, 0, 0)),
    ]
    out_spec = pl.BlockSpec((1, tm, MEM_W), lambda b, i: (b, i, 0))
    scratch = [pltpu.VMEM((MEM_W, tm), f32), pltpu.VMEM((MEM_W // LANES, N_MEM, 2 * tm), f32)]
    return in_specs, out_spec, scratch


def _swa_prompt_kernel(q_ref, kc_ref, kp_ref, vc_ref, vp_ref, sink_ref, qm_ref, mk_ref, mvt_ref, out_ref, om_ref,
                       ot_ref, sc_ref, sp_ref, omt_ref, ms_ref):
    i = pl.program_id(1)
    tm = TOKEN_TILE
    n_pairs = KV_W // LANES
    per_pair = N_HEADS // n_pairs
    key_c = lax.broadcasted_iota(jnp.int32, (tm, tm), 0)
    key_p = lax.broadcasted_iota(jnp.int32, (WINDOW, tm), 0)
    qc = lax.broadcasted_iota(jnp.int32, (1, tm), 1)
    valid_c = (key_c <= qc) & (key_c > qc - WINDOW)
    valid_p = (key_p > qc) & (i > 0)
    for p in range(n_pairs):
        qp = q_ref[0, 2 * p:2 * p + 2].reshape(per_pair * tm, LANES)
        sc_ref[p] = _dot_nt(kc_ref[0, p, 0], qp)
        sp_ref[p] = _dot_nt(kp_ref[0, p, 0], qp)
    _mem_scores(qm_ref, mk_ref, ms_ref)
    for c in range(N_HEADS):
        h, g = divmod(c, GROUP)
        cols = slice((c % per_pair) * tm, (c % per_pair + 1) * tm)
        s_c = jnp.where(valid_c, sc_ref[c // per_pair, :, cols], NEG_INF)
        s_p = jnp.where(valid_p, sp_ref[c // per_pair, :, cols], NEG_INF)
        sink = sink_ref[h, :, g * tm:(g + 1) * tm]
        m = jnp.maximum(jnp.maximum(jnp.max(s_c, axis=0, keepdims=True), jnp.max(s_p, axis=0, keepdims=True)), sink)
        e_c = jnp.exp2(s_c - m)
        e_p = jnp.exp2(s_p - m)
        ot = _dot(vc_ref[0, h, 0], e_c.astype(bf16)) + _dot(vp_ref[0, h, 0], e_p.astype(bf16))
        den = ot[HEAD_DIM:HEAD_DIM + 1, :] + jnp.exp2(sink - m)
        ot_ref[c * HEAD_DIM:(c + 1) * HEAD_DIM, :] = ot[:HEAD_DIM, :] / den
    _store_heads_transposed(ot_ref, out_ref)
    _mem_consume(ms_ref, mvt_ref, omt_ref, om_ref)


def _swa_prompt(qpad, k5, vt5, sink_cols, qmpad, mk2, mvt):
    B = qpad.shape[0]
    nt = k5.shape[2]
    tm = TOKEN_TILE
    T = nt * tm
    prev = lambda i: jnp.maximum(i - 1, 0)
    mem_in, mem_out, mem_scratch = _mem_specs(tm)
    return pl.pallas_call(
        _swa_prompt_kernel,
        grid=(B, nt),
        in_specs=[
            pl.BlockSpec((1, N_KV_HEADS, GROUP * tm, LANES), lambda b, i: (b, 0, i, 0)),
            pl.BlockSpec((1, KV_W // LANES, 1, tm, LANES), lambda b, i: (b, 0, i, 0, 0)),
            pl.BlockSpec((1, KV_W // LANES, 1, WINDOW, LANES), lambda b, i: (b, 0, prev(i), tm // WINDOW - 1, 0)),
            pl.BlockSpec((1, N_KV_HEADS, 1, V_ROWS, tm), lambda b, i: (b, 0, i, 0, 0)),
            pl.BlockSpec((1, N_KV_HEADS, 1, V_ROWS, WINDOW), lambda b, i: (b, 0, prev(i), 0, tm // WINDOW - 1)),
            pl.BlockSpec((N_KV_HEADS, 1, GROUP * tm), lambda b, i: (0, 0, 0)),
        ] + mem_in,
        out_specs=[pl.BlockSpec((1, tm, Q_W), lambda b, i: (b, i, 0)), mem_out],
        out_shape=[jax.ShapeDtypeStruct((B, T, Q_W), bf16), jax.ShapeDtypeStruct((B, T, MEM_W), bf16)],
        scratch_shapes=[pltpu.VMEM((Q_W, tm), f32),
                        pltpu.VMEM((KV_W // LANES, tm, N_HEADS * tm * LANES // KV_W), f32),
                        pltpu.VMEM((KV_W // LANES, WINDOW, N_HEADS * tm * LANES // KV_W), f32)] + mem_scratch,
        compiler_params=_cparams(2),
        name="swa_prompt",
    )(qpad, k5, k5, vt5, vt5, sink_cols, qmpad, mk2, mvt)


def _top3_bias(gate, eligible, axis):
    idx = lax.broadcasted_iota(jnp.int32, gate.shape, axis)
    n = gate.shape[axis]
    rest = jnp.where(eligible, gate, NEG_INF)
    chosen = jnp.zeros(gate.shape, dtype=jnp.bool_)
    for _ in range(MOBA_TOPK):
        best = jnp.max(rest, axis=axis, keepdims=True)
        first = jnp.min(jnp.where(rest == best, idx, n), axis=axis, keepdims=True)
        pick = (idx == first) & (best > NEG_INF)
        chosen = chosen | pick
        rest = jnp.where(pick, NEG_INF, rest)
    return jnp.where(chosen, 0.0, NEG_INF)


def _moba_gate_bias(q, km, block):
    nb = km.shape[0]
    q_hi = q.astype(bf16)
    q_lo = (q - q_hi.astype(f32)).astype(bf16)
    lane = lax.broadcasted_iota(jnp.int32, (1, LANES), 1)
    lo = lane < HEAD_DIM
    blk = lax.broadcasted_iota(jnp.int32, (nb, 1), 0)
    out = []
    for h in range(N_KV_HEADS):
        kt = km[:, (h // 2) * LANES:(h // 2 + 1) * LANES]
        gates = []
        for g in range(GROUP):
            hq = h * GROUP + g
            kk = kt if hq % 2 == h % 2 else pltpu.roll(kt, HEAD_DIM, 1)
            kk = jnp.where(lo if hq % 2 == 0 else jnp.logical_not(lo), kk, 0.0)
            kk_hi = kk.astype(bf16)
            kk_lo = (kk - kk_hi.astype(f32)).astype(bf16)
            cols = slice((hq // 2) * LANES, (hq // 2 + 1) * LANES)
            gates.append(_dot_nt(kk_hi, q_hi[:, cols]) + _dot_nt(kk_hi, q_lo[:, cols])
                         + _dot_nt(kk_lo, q_hi[:, cols]))
        gate = jnp.concatenate(gates, axis=1)
        out.append(_top3_bias(gate, blk < block, 0))
    return out


def _moba_prompt_kernel(q_ref, k_ref, vt_ref, bias_ref, qm_ref, mk_ref, mvt_ref, out_ref, om_ref,
                        acc_ref, m_ref, sa_ref, sb_ref, omt_ref, ms_ref):
    i = pl.program_id(1)
    tm = TOKEN_TILE
    nb = k_ref.shape[2]
    n_pairs = KV_W // LANES
    per_pair = N_HEADS // n_pairs
    key = lax.broadcasted_iota(jnp.int32, (tm, tm), 0)
    qc = lax.broadcasted_iota(jnp.int32, (1, tm), 1)
    causal = key <= qc

    def scores_into(j, s_ref):
        for p in range(n_pairs):
            qp = q_ref[0, 2 * p:2 * p + 2].reshape(per_pair * tm, LANES)
            s_ref[p] = _dot_nt(k_ref[0, p, j], qp)

    def chunk(s_ref, c):
        return s_ref[c // per_pair, :, (c % per_pair) * tm:(c % per_pair + 1) * tm]

    scores_into(i, sb_ref)
    scores_into(0, sa_ref)
    _mem_scores(qm_ref, mk_ref, ms_ref)
    for c in range(N_HEADS):
        s = jnp.where(causal, chunk(sb_ref, c), NEG_INF)
        m = jnp.max(s, axis=0, keepdims=True)
        e = jnp.exp2(s - m)
        m_ref[c] = m
        acc_ref[c] = _dot(vt_ref[0, c // GROUP, i], e.astype(bf16))
    _mem_consume(ms_ref, mvt_ref, omt_ref, om_ref)

    def consume(j, s_ref):
        for c in range(N_HEADS):
            h, g = divmod(c, GROUP)
            s = chunk(s_ref, c)
            bias = bias_ref[0, 0, h, pl.ds(j, 1), g * tm:(g + 1) * tm]
            m = m_ref[c]
            m_new = jnp.maximum(m, jnp.max(s, axis=0, keepdims=True) + bias)
            a = jnp.exp2(m - m_new)
            e = jnp.exp2(s - (m_new - bias))
            m_ref[c] = m_new
            acc_ref[c] = a * acc_ref[c] + _dot(vt_ref[0, h, j], e.astype(bf16))

    def body(jj, carry):
        j = 2 * jj
        scores_into(j + 1, sb_ref)
        consume(j, sa_ref)
        scores_into(jnp.minimum(j + 2, nb - 1), sa_ref)
        consume(j + 1, sb_ref)
        return carry

    lax.fori_loop(0, (i + 1) // 2, body, 0)
    heads = [acc_ref[c, :HEAD_DIM, :] / acc_ref[c, HEAD_DIM:HEAD_DIM + 1, :] for c in range(N_HEADS)]
    out_ref[0] = jnp.concatenate(heads, axis=0).T.astype(out_ref.dtype)


def _moba_prompt(qpad, k5, vt5, bias, qmpad, mk2, mvt):
    B = qpad.shape[0]
    nb = k5.shape[2]
    tm = TOKEN_TILE
    T = nb * tm
    mem_in, mem_out, mem_scratch = _mem_specs(tm)
    return pl.pallas_call(
        _moba_prompt_kernel,
        grid=(B, nb),
        in_specs=[
            pl.BlockSpec((1, N_KV_HEADS, GROUP * tm, LANES), lambda b, i: (b, 0, i, 0)),
            pl.BlockSpec((1, KV_W // LANES, nb, tm, LANES), lambda b, i: (b, 0, 0, 0, 0)),
            pl.BlockSpec((1, N_KV_HEADS, nb, V_ROWS, tm), lambda b, i: (b, 0, 0, 0, 0)),
            pl.BlockSpec((1, 1, N_KV_HEADS, nb, GROUP * tm), lambda b, i: (b, i, 0, 0, 0)),
        ] + mem_in,
        out_specs=[pl.BlockSpec((1, tm, Q_W), lambda b, i: (b, i, 0)), mem_out],
        out_shape=[jax.ShapeDtypeStruct((B, T, Q_W), bf16), jax.ShapeDtypeStruct((B, T, MEM_W), bf16)],
        scratch_shapes=[pltpu.VMEM((N_HEADS, V_ROWS, tm), f32), pltpu.VMEM((N_HEADS, 1, tm), f32),
                        pltpu.VMEM((KV_W // LANES, tm, N_HEADS * tm * LANES // KV_W), f32),
                        pltpu.VMEM((KV_W // LANES, tm, N_HEADS * tm * LANES // KV_W), f32)] + mem_scratch,
        compiler_params=_cparams(2),
        name="moba_prompt",
    )(qpad, k5, vt5, bias, qmpad, mk2, mvt)


SAMPLE_ROWS = N_HEADS * 4
NEW_PAD = 8
SEQ_GROUP = 8


def _pad_rows(a, n):
    return jnp.concatenate([a, jnp.zeros((n - a.shape[0], a.shape[1]), a.dtype)], axis=0)


def _swa_sample_kernel(q_ref, kc_ref, vc_ref, kn_ref, vn_ref, sink_ref, o_ref, nk_ref, nv_ref, s_ref):
    wb = kc_ref.shape[2]
    col = lax.broadcasted_iota(jnp.int32, (SAMPLE_ROWS, wb), 1)
    s_idx = lax.broadcasted_iota(jnp.int32, (SAMPLE_ROWS, 1), 0) % 4
    valid_c = col > s_idx
    valid_n = col <= s_idx
    lane = lax.broadcasted_iota(jnp.int32, (1, wb), 1)
    sink = sink_ref[...]
    for n in range(SEQ_GROUP):
        q = q_ref[n]
        kn = _pad_rows(kn_ref[n], wb)
        s_ref[n, :, :wb] = _dot(q, kc_ref[n].astype(bf16))
        s_ref[n, :, wb:] = _dot_nt(q, kn.astype(bf16))
    for n in range(SEQ_GROUP):
        vn = _pad_rows(vn_ref[n], wb)
        s_c = jnp.where(valid_c, s_ref[n, :, :wb], NEG_INF)
        s_n = jnp.where(valid_n, s_ref[n, :, wb:], NEG_INF)
        m = jnp.maximum(jnp.maximum(jnp.max(s_c, axis=1, keepdims=True), jnp.max(s_n, axis=1, keepdims=True)), sink)
        e_c = jnp.exp(s_c - m)
        e_n = jnp.exp(s_n - m)
        den = jnp.sum(e_c, axis=1, keepdims=True) + jnp.sum(e_n, axis=1, keepdims=True) + jnp.exp(sink - m)
        o = _dot_nt(e_c.astype(bf16), vc_ref[n].astype(bf16)) + _dot(e_n.astype(bf16), vn.astype(bf16))
        o_ref[n] = o / den
    for n in range(SEQ_GROUP):
        kn = _pad_rows(kn_ref[n], wb)
        vn = _pad_rows(vn_ref[n], wb)
        nk_ref[n] = jnp.where(lane < wb - 4, pltpu.roll(kc_ref[n], wb - 4, 1), pltpu.roll(kn.T, wb - 4, 1))
        nv_ref[n] = jnp.where(lane < wb - 4, pltpu.roll(vc_ref[n], wb - 4, 1), pltpu.roll(vn.T, wb - 4, 1))


def _swa_sample(qbd, kc, vc, kn, vn, sink_rows):
    n_seq, _, wb = kc.shape
    sg = SEQ_GROUP
    seq = lambda r, w: pl.BlockSpec((sg, r, w), lambda i: (i, 0, 0))
    return pl.pallas_call(
        _swa_sample_kernel,
        grid=(n_seq // sg,),
        in_specs=[seq(SAMPLE_ROWS, KV_W), seq(KV_W, wb), seq(KV_W, wb), seq(NEW_PAD, KV_W), seq(NEW_PAD, KV_W),
                  pl.BlockSpec((SAMPLE_ROWS, 1), lambda i: (0, 0))],
        out_specs=[seq(SAMPLE_ROWS, KV_W), seq(KV_W, wb), seq(KV_W, wb)],
        out_shape=[jax.ShapeDtypeStruct((n_seq, SAMPLE_ROWS, KV_W), f32),
                   jax.ShapeDtypeStruct((n_seq, KV_W, wb), f32),
                   jax.ShapeDtypeStruct((n_seq, KV_W, wb), f32)],
        scratch_shapes=[pltpu.VMEM((sg, SAMPLE_ROWS, 2 * wb), f32)],
        compiler_params=_cparams(1),
        name="swa_sample",
    )(qbd, kc, vc, kn, vn, sink_rows)


def _mem_sample_kernel(q_ref, k_ref, v_ref, o_ref):
    for n in range(SEQ_GROUP):
        s = _dot(q_ref[n], k_ref[0, n].astype(bf16))
        m = jnp.max(s, axis=1, keepdims=True)
        e = jnp.exp(s - m)
        den = jnp.sum(e, axis=1, keepdims=True)
        o_ref[n] = _dot_nt(e.astype(bf16), v_ref[0, n].astype(bf16)) / den


def _mem_sample(qbd, km, vm, layer):
    n_seq = km.shape[1]
    sg = SEQ_GROUP
    rows = qbd.shape[1]
    cache = pl.BlockSpec((1, sg, MEM_W, N_MEM), lambda i: (layer, i, 0, 0))
    return pl.pallas_call(
        _mem_sample_kernel,
        grid=(n_seq // sg,),
        in_specs=[pl.BlockSpec((sg, rows, MEM_W), lambda i: (i, 0, 0)), cache, cache],
        out_specs=pl.BlockSpec((sg, rows, MEM_W), lambda i: (i, 0, 0)),
        out_shape=jax.ShapeDtypeStruct((n_seq, rows, MEM_W), f32),
        compiler_params=_cparams(1),
        name="mem_sample",
    )(qbd, km, vm)


def _moba_sample_kernel(pt_ref, q_ref, qf_ref, kn_ref, vn_ref, kpool_ref, vpool_ref, o_ref,
                        kbuf, vbuf, sem, s_ref, e_ref):
    b = pl.program_id(0)
    n_seq = pl.num_programs(0)
    n_pages = pt_ref.shape[1]
    page = kpool_ref.shape[2]
    past = n_pages * page
    nb = past // MOBA_BLOCK
    slot = b % 2

    def page_copy(which, seq, slot_, pg):
        pool, buf = ((kpool_ref, kbuf), (vpool_ref, vbuf))[which]
        return pltpu.make_async_copy(pool.at[pt_ref[seq, pg]], buf.at[slot_, :, pg * page:(pg + 1) * page],
                                     sem.at[which, slot_])

    def fetch(seq, slot_):
        for pg in range(n_pages):
            page_copy(0, seq, slot_, pg).start()
        for pg in range(n_pages):
            page_copy(1, seq, slot_, pg).start()

    def wait(which, seq, slot_):
        for pg in range(n_pages):
            page_copy(which, seq, slot_, pg).wait()

    @pl.when(b == 0)
    def _():
        fetch(0, 0)

    @pl.when(b + 1 < n_seq)
    def _():
        fetch(b + 1, 1 - slot)

    wait(0, b, slot)
    q = q_ref[0]
    s_ref[...] = _dot(q, kbuf[slot].astype(bf16))
    lane = lax.broadcasted_iota(jnp.int32, (1, LANES), 1)
    kmean = jnp.zeros((KV_W, LANES), f32)
    for j in range(nb):
        part = kbuf[slot, :, j * MOBA_BLOCK:j * MOBA_BLOCK + LANES]
        for c in range(1, MOBA_BLOCK // LANES):
            part = part + kbuf[slot, :, j * MOBA_BLOCK + c * LANES:j * MOBA_BLOCK + (c + 1) * LANES]
        kmean = jnp.where(lane == j, jnp.sum(part, axis=1, keepdims=True), kmean)
    kmean = kmean * (1.0 / MOBA_BLOCK)
    gate = jnp.dot(qf_ref[0], kmean, preferred_element_type=f32, precision=lax.Precision.HIGHEST)
    bias = _top3_bias(gate, lane < nb, 1)
    col = lax.broadcasted_iota(jnp.int32, (SAMPLE_ROWS, LANES), 1)
    s_idx = lax.broadcasted_iota(jnp.int32, (SAMPLE_ROWS, 1), 0) % 4
    kn = _pad_rows(kn_ref[0], LANES).astype(bf16)
    vn = _pad_rows(vn_ref[0], LANES).astype(bf16)
    s_n = jnp.where(col <= s_idx, _dot_nt(q, kn), NEG_INF)
    mx = jnp.full((SAMPLE_ROWS, MOBA_BLOCK), NEG_INF, f32)
    for j in range(nb):
        mx = jnp.maximum(mx, s_ref[:, j * MOBA_BLOCK:(j + 1) * MOBA_BLOCK] + bias[:, j:j + 1])
    m = jnp.maximum(jnp.max(mx, axis=1, keepdims=True), jnp.max(s_n, axis=1, keepdims=True))
    acc = jnp.zeros((SAMPLE_ROWS, MOBA_BLOCK), f32)
    for j in range(nb):
        e = jnp.exp(s_ref[:, j * MOBA_BLOCK:(j + 1) * MOBA_BLOCK] - (m - bias[:, j:j + 1]))
        acc = acc + e
        e_ref[:, j * MOBA_BLOCK:(j + 1) * MOBA_BLOCK] = e.astype(bf16)
    e_n = jnp.exp(s_n - m)
    den = jnp.sum(acc, axis=1, keepdims=True) + jnp.sum(e_n, axis=1, keepdims=True)
    wait(1, b, slot)
    o = _dot_nt(e_ref[...], vbuf[slot].astype(bf16)) + _dot(e_n.astype(bf16), vn)
    o_ref[0] = o / den


def _moba_sample(page_table, qbd, qbd_f32, kn, vn, kpool, vpool):
    n_seq, n_pages = page_table.shape
    page = kpool.shape[2]
    past = n_pages * page
    seq = lambda r: pl.BlockSpec((1, r, KV_W), lambda i, pt: (i, 0, 0))
    grid_spec = pltpu.PrefetchScalarGridSpec(
        num_scalar_prefetch=1,
        grid=(n_seq,),
        in_specs=[seq(SAMPLE_ROWS), seq(SAMPLE_ROWS), seq(NEW_PAD), seq(NEW_PAD),
                  pl.BlockSpec(memory_space=pl.ANY), pl.BlockSpec(memory_space=pl.ANY)],
        out_specs=seq(SAMPLE_ROWS),
        scratch_shapes=[pltpu.VMEM((2, KV_W, past), f32), pltpu.VMEM((2, KV_W, past), f32),
                        pltpu.SemaphoreType.DMA((2, 2)),
                        pltpu.VMEM((SAMPLE_ROWS, past), f32), pltpu.VMEM((SAMPLE_ROWS, past), bf16)],
    )
    return pl.pallas_call(
        _moba_sample_kernel,
        grid_spec=grid_spec,
        out_shape=jax.ShapeDtypeStruct((n_seq, SAMPLE_ROWS, KV_W), f32),
        compiler_params=_cparams(1),
        name="moba_sample",
    )(page_table, qbd, qbd_f32, kn, vn, kpool, vpool)


def _block_diag_rows(a, n_heads, per_kv):
    n_seq, S, _ = a.shape
    n_kv = n_heads // per_kv
    a = a.reshape(n_seq, S, n_kv, per_kv, HEAD_DIM).transpose(0, 2, 3, 1, 4)
    eye = jnp.eye(n_kv, dtype=a.dtype)
    bd = a[:, :, :, :, None, :] * eye[None, :, None, None, :, None]
    return bd.reshape(n_seq, n_heads * S, n_kv * HEAD_DIM)


def _diag_rows_out(o, n_heads, per_kv, S):
    n_seq = o.shape[0]
    n_kv = n_heads // per_kv
    o = o.reshape(n_seq, n_kv, per_kv, S, n_kv, HEAD_DIM)
    o = jnp.stack([o[:, h, :, :, h, :] for h in range(n_kv)], axis=1)
    return o.transpose(0, 3, 1, 2, 4).reshape(n_seq, S, n_heads * HEAD_DIM)


def _stacked_weights(w_o, ln1_g, ln1_b, w_ffn_gate, conv_w, conv_b, w_ffn_up, w_ffn_down, ln2_g, ln2_b):
    row = lambda a: a[:, None, :]
    return dict(
        w_o=w_o.astype(bf16), g1=row(ln1_g), b1=row(ln1_b),
        w_gate=w_ffn_gate.astype(bf16), conv_w=conv_w, conv_b=row(conv_b),
        w_up=w_ffn_up.astype(bf16), w_down=w_ffn_down.astype(bf16),
        g2=row(ln2_g), b2=row(ln2_b),
    )


def _conv_injection(state):
    n_seq = state.shape[0]
    return jnp.pad(state, ((0, 0), (0, 2), (0, 0))).reshape(1, n_seq * 4, D_FF)


def kernel(x_prompt, x_sample, cache_win_k, cache_win_v, cache_moba_k, cache_moba_v, page_table, cache_mem_k, cache_mem_v, state_conv, mem_prompt, w_in_a, sink_a, w_in_b, w_kv_shared, w_mem_kv, w_o, ln1_g, ln1_b, w_ffn_gate, conv_w, conv_b, w_ffn_up, w_ffn_down, ln2_g, ln2_b):
    B, T, _ = x_prompt.shape
    Bd, S, _ = x_sample.shape
    n_pool, page = cache_moba_k.shape[:2]
    past = page_table.shape[1] * page
    wb = cache_win_k.shape[2]
    nb = T // TOKEN_TILE

    cos_p, sin_p = _rope_tables(jnp.arange(T, dtype=jnp.int32))
    cos_s, sin_s = _rope_tables(past + (jnp.arange(Bd * S, dtype=jnp.int32) % S))
    w_a = w_in_a[0].astype(bf16)
    w_b = jnp.concatenate([w_in_b[0][:, :Q_W], w_kv_shared, w_in_b[0][:, Q_W:]], axis=1).astype(bf16)
    lw = _stacked_weights(w_o, ln1_g, ln1_b, w_ffn_gate, conv_w, conv_b, w_ffn_up, w_ffn_down, ln2_g, ln2_b)
    xs = x_sample.reshape(1, Bd * S, D_MODEL)

    def feature_major(a):
        a = jnp.moveaxis(a, -3, -1)
        return a.reshape(a.shape[:-3] + (a.shape[-3] * HEAD_DIM, a.shape[-1]))

    def token_major(a, n_heads):
        a = a.reshape(a.shape[:-2] + (n_heads, HEAD_DIM, a.shape[-1]))
        return jnp.moveaxis(a, -1, -3)

    mem_kvt, mem_k2, mem_vt = _memkv(mem_prompt, w_mem_kv.astype(bf16))
    mem_k_out = token_major(mem_kvt[:, :, :MEM_W], MEM_HEADS)
    mem_v_out = token_major(mem_kvt[:, :, MEM_W:], MEM_HEADS)
    cache_mem_kt = feature_major(cache_mem_k)
    cache_mem_vt = feature_major(cache_mem_v)

    def sample_rows(a, n_heads, per_kv):
        return _block_diag_rows(a.reshape(Bd, S, n_heads * HEAD_DIM), n_heads, per_kv)

    def pad_new(a):
        return jnp.pad(a.reshape(Bd, S, KV_W), ((0, 0), (0, NEW_PAD - S), (0, 0)))

    def sample_mem(qm, layer):
        qbd = (sample_rows(qm, MEM_HEADS, 1) * SCALE).astype(bf16)
        o = _mem_sample(qbd, cache_mem_kt, cache_mem_vt, layer)
        return _diag_rows_out(o, MEM_HEADS, 1, S).reshape(1, Bd * S, MEM_W).astype(bf16)

    kat, vat, qpad, qmpad, k5, vt5, _ = _proj(x_prompt, w_a, cos_p, sin_p, True, want_q=False)
    sink_cols = jnp.repeat(sink_a[0].reshape(N_KV_HEADS, GROUP) * LOG2E, TOKEN_TILE, axis=1)[:, None, :]
    o_self, o_mem = _swa_prompt(qpad, k5, vt5, sink_cols, qmpad, mem_k2[0], mem_vt[0])
    xp, tail0 = _merge_ffn(x_prompt, o_self, o_mem, lw, 0)
    win_kp = token_major(kat[:, :, T - wb:], N_KV_HEADS)[None]
    win_vp = token_major(vat[:, :, T - wb:], N_KV_HEADS)[None]

    qs, ks, vs, qms = _proj(xs, w_a, cos_s, sin_s, False)
    qbd = (sample_rows(qs, N_HEADS, GROUP) * SCALE).astype(bf16)
    sink_rows = jnp.repeat(sink_a[0], S)[:, None]
    o_s, win_kst, win_vst = _swa_sample(qbd, feature_major(cache_win_k[0]), feature_major(cache_win_v[0]),
                                        pad_new(ks), pad_new(vs), sink_rows)
    os_self = _diag_rows_out(o_s, N_HEADS, GROUP, S).reshape(1, Bd * S, Q_W).astype(bf16)
    xs, u0 = _merge_ffn(xs, os_self, sample_mem(qms, 0), lw, 0, _conv_injection(state_conv[0]))

    kbt, vbt, qpad, qmpad, k5, vt5, _, bias = _proj(xp, w_b, cos_p, sin_p, True, want_q=False, want_gate=True)
    o_self, o_mem = _moba_prompt(qpad, k5, vt5, bias, qmpad, mem_k2[1], mem_vt[1])
    yp, tail1 = _merge_ffn(xp, o_self, o_mem, lw, 1)

    qs, ks, vs, qms = _proj(xs, w_b, cos_s, sin_s, False)
    qbd_f32 = sample_rows(qs, N_HEADS, GROUP)
    o_s = _moba_sample(page_table, (qbd_f32 * SCALE).astype(bf16), qbd_f32, pad_new(ks), pad_new(vs),
                       feature_major(cache_moba_k), feature_major(cache_moba_v))
    os_self = _diag_rows_out(o_s, N_HEADS, GROUP, S).reshape(1, Bd * S, Q_W).astype(bf16)
    ys, u1 = _merge_ffn(xs, os_self, sample_mem(qms, 1), lw, 1, _conv_injection(state_conv[1]))

    kv_heads = lambda a, lead: a.reshape(lead + (N_KV_HEADS, HEAD_DIM))
    conv_p = jnp.stack([tail0[:, 8 - (CONV_W - 1):], tail1[:, 8 - (CONV_W - 1):]])
    conv_s = jnp.stack([u.reshape(Bd, S, D_FF)[:, S - (CONV_W - 1):] for u in (u0, u1)])
    return (
        yp, ys.reshape(Bd, S, D_MODEL),
        win_kp, win_vp,
        token_major(kbt, N_KV_HEADS), token_major(vbt, N_KV_HEADS),
        mem_k_out, mem_v_out,
        conv_p,
        token_major(win_kst, N_KV_HEADS)[None], token_major(win_vst, N_KV_HEADS)[None],
        kv_heads(ks, (Bd, S)), kv_heads(vs, (Bd, S)),
        conv_s,
    )
```
